```python
import math, functools
import jax, jax.numpy as jnp
from jax import lax
import numpy as np

D_MODEL = 1024
BATCH = 8
SEQ = 2048
DEPTH = 1
DEC_BATCH = 128
DEC_SEQ = 1
PAST_LEN = 8192
PAGE_SIZE = 128

N_META = 16
MLA_HEADS = 8
QK_NOPE = 64
QK_ROPE = 32
V_DIM = 64
Q_LORA = 384
KV_LORA = 256
KV_WIDTH = KV_LORA + QK_ROPE
ROPE_BASE = 10000.0
Q_BLOCK = 128
SM_SCALE = (QK_NOPE + QK_ROPE) ** -0.5
RWKV_HEADS = 8
HEAD_SIZE = 64
RWKV_WIDTH = RWKV_HEADS * HEAD_SIZE
W_LORA = 64
A_LORA = 64
G_LORA = 128
RWKV_COLS = 3 * RWKV_WIDTH + W_LORA + A_LORA + G_LORA
RWKV_SPLITS = [RWKV_WIDTH, 2 * RWKV_WIDTH, 3 * RWKV_WIDTH, 3 * RWKV_WIDTH + W_LORA, 3 * RWKV_WIDTH + W_LORA + A_LORA]
GN_EPS = 64e-5
MLA_COLS = Q_LORA + KV_LORA + QK_ROPE
IN_COLS = MLA_COLS + RWKV_COLS + 2 * D_MODEL
IN_SPLITS = [Q_LORA, Q_LORA + KV_LORA, MLA_COLS, MLA_COLS + RWKV_COLS]
D_FF = 4 * D_MODEL
NORM_EPS = 1e-6
NEG_INF = -1e30

kernel_name = 'mla_rwkv7_gated_hybrid_step'


def rms_norm(x, g):
    xf = x.astype(jnp.float32)
    y = xf * lax.rsqrt(jnp.mean(xf * xf, axis=-1, keepdims=True) + NORM_EPS)
    return (y * g.astype(jnp.float32)).astype(x.dtype)


def rope_angles(pos):
    inv = ROPE_BASE ** (-jnp.arange(0, QK_ROPE, 2, dtype=jnp.float32) / QK_ROPE)
    ang = pos.astype(jnp.float32)[:, None] * inv[None, :]
    return jnp.cos(ang), jnp.sin(ang)


def apply_rope(x, cos, sin):
    xf = x.astype(jnp.float32)
    x1, x2 = xf[..., :QK_ROPE // 2], xf[..., QK_ROPE // 2:]
    return jnp.concatenate([x1 * cos - x2 * sin, x1 * sin + x2 * cos], axis=-1).astype(x.dtype)


def project_inputs(h, lw, pos):
    B, T = h.shape[:2]
    z = h @ lw['w_in']
    q_in, kv_in, kr_in, rw_cols, gate_in = jnp.split(z, IN_SPLITS, axis=-1)
    cos, sin = rope_angles(pos)
    q = (rms_norm(q_in, lw['g_q']) @ lw['w_uq']).reshape(B, T, MLA_HEADS, QK_NOPE + QK_ROPE)
    q_nope = q[..., :QK_NOPE]
    q_rope = apply_rope(q[..., QK_NOPE:], cos[:, None, :], sin[:, None, :])
    kv_rows = jnp.concatenate([rms_norm(kv_in, lw['g_kv']), apply_rope(kr_in, cos, sin)], axis=-1)
    return q_nope, q_rope, kv_rows, rw_cols, gate_in


def mla_attend_prompt(q_nope, q_rope, kv_rows, lw):
    f32 = jnp.float32
    B, T = kv_rows.shape[:2]
    c_kv = kv_rows[..., :KV_LORA].astype(f32)
    k_rope = kv_rows[..., KV_LORA:].astype(f32)
    k_nope = jnp.einsum('btc,chn->bthn', c_kv, lw['w_uk'].astype(f32).reshape(KV_LORA, MLA_HEADS, QK_NOPE))
    v = jnp.einsum('btc,chv->bthv', c_kv, lw['w_uv'].astype(f32).reshape(KV_LORA, MLA_HEADS, V_DIM))
    nb = -(-T // Q_BLOCK)
    tp = nb * Q_BLOCK

    def to_blocks(t):
        t = jnp.pad(t.astype(f32), [(0, 0), (0, tp - T)] + [(0, 0)] * (t.ndim - 2))
        return jnp.moveaxis(t.reshape((B, nb, Q_BLOCK) + t.shape[2:]), 1, 0)

    k_pos = jnp.arange(T)

    def one_block(args):
        qn, qr, i = args
        q_pos = i * Q_BLOCK + jnp.arange(Q_BLOCK)
        s = (jnp.einsum('bqhn,bkhn->bhqk', qn, k_nope) + jnp.einsum('bqhr,bkr->bhqk', qr, k_rope)) * SM_SCALE
        s = jnp.where(k_pos[None, :] <= q_pos[:, None], s, NEG_INF)
        return jnp.einsum('bhqk,bkhv->bqhv', jax.nn.softmax(s, axis=-1), v)

    o = lax.map(one_block, (to_blocks(q_nope), to_blocks(q_rope), jnp.arange(nb)))
    return jnp.moveaxis(o, 0, 1).reshape(B, tp, MLA_HEADS, V_DIM)[:, :T]


def softmax_merge(carry, s, vals):
    m, l, acc = carry
    m_new = jnp.maximum(m, s.max(axis=-1))
    alpha = jnp.exp(m - m_new)
    p = jnp.exp(s - m_new[..., None])
    l = l * alpha + p.sum(axis=-1)
    acc = acc * alpha[..., None] + jnp.einsum('bhqk,bkc->bhqc', p, vals)
    return (m_new, l, acc)


def mla_attend_sample(q_nope, q_rope, kv_rows, lw, cache_kv, page_table):
    f32 = jnp.float32
    Bd, S = kv_rows.shape[:2]
    w_uk = lw['w_uk'].astype(f32).reshape(KV_LORA, MLA_HEADS, QK_NOPE)
    w_uv = lw['w_uv'].astype(f32).reshape(KV_LORA, MLA_HEADS, V_DIM)
    q_lat = jnp.einsum('bqhn,chn->bqhc', q_nope.astype(f32), w_uk)
    qf = jnp.concatenate([q_lat, q_rope.astype(f32)], axis=-1) * SM_SCALE

    def page_step(carry, pages):
        rows = cache_kv[pages].astype(f32)
        s = jnp.einsum('bqhc,bkc->bhqk', qf, rows)
        return softmax_merge(carry, s, rows[..., :KV_LORA]), None

    init = (jnp.full((Bd, MLA_HEADS, S), NEG_INF, f32),
            jnp.zeros((Bd, MLA_HEADS, S), f32),
            jnp.zeros((Bd, MLA_HEADS, S, KV_LORA), f32))
    carry, _ = lax.scan(page_step, init, page_table.T)
    kvn = kv_rows.astype(f32)
    s = jnp.einsum('bqhc,bkc->bhqk', qf, kvn)
    s = jnp.where(jnp.tril(jnp.ones((S, S), bool)), s, NEG_INF)
    m, l, acc = softmax_merge(carry, s, kvn[..., :KV_LORA])
    o_lat = acc / l[..., None]
    return jnp.einsum('bhqc,chv->bqhv', o_lat, w_uv)


def wkv_scan(S0, r, w, k, v, a, b):
    def step(S, inp):
        r_t, w_t, k_t, v_t, a_t, b_t = inp
        sa = jnp.einsum('bhvk,bhk->bhv', S, a_t)
        S = S * w_t[:, :, None, :] + sa[..., None] * b_t[:, :, None, :] + v_t[..., None] * k_t[:, :, None, :]
        return S, jnp.einsum('bhvk,bhk->bhv', S, r_t)
    xs = (jnp.swapaxes(r, 0, 1), jnp.swapaxes(w, 0, 1), jnp.swapaxes(k, 0, 1),
          jnp.swapaxes(v, 0, 1), jnp.swapaxes(a, 0, 1), jnp.swapaxes(b, 0, 1))
    S, y = lax.scan(step, S0, xs)
    return S, jnp.swapaxes(y, 0, 1)


def rwkv_branch(cols, shift0, wkv0, lw):
    f32 = jnp.float32
    B, T = cols.shape[:2]
    c = cols.astype(f32)
    prev = jnp.concatenate([shift0[:, None].astype(f32), c[:, :-1]], axis=1)
    z = c + (prev - c) * lw['mu_shift'].astype(f32)
    r, k, v, w_in, a_in, g_in = jnp.split(z, RWKV_SPLITS, axis=-1)
    w_log = lw['w0'].astype(f32) + jnp.tanh(w_in) @ lw['w2'].astype(f32)
    decay = jnp.exp(-jnp.exp(-jax.nn.softplus(-w_log) - 0.5))
    a = jax.nn.sigmoid(lw['a0'].astype(f32) + a_in @ lw['a2'].astype(f32))
    g = jax.nn.sigmoid(g_in) @ lw['g2'].astype(f32)
    hs = (RWKV_HEADS, HEAD_SIZE)
    r, k, v, decay, a = [t.reshape(B, T, RWKV_HEADS, HEAD_SIZE) for t in (r, k, v, decay, a)]
    kk = k * lw['k_k'].astype(f32).reshape(hs)
    kk = kk / jnp.maximum(jnp.sqrt(jnp.sum(kk * kk, axis=-1, keepdims=True)), 1e-12)
    k = k * (1.0 + (a - 1.0) * lw['k_a'].astype(f32).reshape(hs))
    S, y = wkv_scan(wkv0.astype(f32), r, decay, k, v, -kk, kk * a)
    mu = jnp.mean(y, axis=-1, keepdims=True)
    var = jnp.mean(jnp.square(y - mu), axis=-1, keepdims=True)
    y = (y - mu) * lax.rsqrt(var + GN_EPS) * lw['ln_w'].astype(f32).reshape(hs) + lw['ln_b'].astype(f32).reshape(hs)
    y = y + jnp.sum(r * k * lw['r_k'].astype(f32), axis=-1, keepdims=True) * v
    return y.reshape(B, T, RWKV_WIDTH) * g, S


def hybrid_layer(x, pos, attend, wkv0, shift0, lw):
    B, T = x.shape[:2]
    h = rms_norm(x, lw['g_mix'])
    q_nope, q_rope, kv_rows, rw_cols, gate_in = project_inputs(h, lw, pos)
    o_mla = attend(q_nope, q_rope, kv_rows, lw).reshape(B, T, MLA_HEADS * V_DIM).astype(x.dtype) @ lw['w_o_mla']
    o_rwkv, wkv_new = rwkv_branch(rw_cols, shift0, wkv0, lw)
    o_rwkv = o_rwkv.astype(x.dtype) @ lw['w_o_rwkv']
    g_mla, g_rwkv = jnp.split(jax.nn.sigmoid(gate_in), 2, axis=-1)
    x = x + (g_mla * o_mla + g_rwkv * o_rwkv) @ lw['w_out']
    h2 = rms_norm(x, lw['g_ffn'])
    x = x + jnp.square(jax.nn.relu(h2 @ lw['w_up'])) @ lw['w_down']
    return x, kv_rows, wkv_new, rw_cols[:, -1]


def setup_inputs(seed: int = 0) -> dict:
    key = jax.random.key(seed)
    keys = iter(jax.random.split(key, 48))

    def nrm(shape, scale):
        return scale * jax.random.normal(next(keys), shape, jnp.float32)

    def gain(shape):
        return 1.0 + nrm(shape, 0.02)

    L = DEPTH
    n_pages = PAST_LEN // PAGE_SIZE
    n_used = DEC_BATCH * n_pages
    n_pool = (n_used * 5) // 4
    x_prompt = nrm((BATCH, SEQ, D_MODEL), 1.0)
    x_sample = nrm((DEC_BATCH, DEC_SEQ, D_MODEL), 1.0)
    cache_kv = nrm((L, n_pool, PAGE_SIZE, KV_WIDTH), 1.0)
    page_table = jax.random.permutation(next(keys), n_pool)[:n_used].reshape(DEC_BATCH, n_pages).astype(jnp.int32)
    state_wkv = nrm((L, DEC_BATCH, RWKV_HEADS, HEAD_SIZE, HEAD_SIZE), 0.5)
    state_shift = nrm((L, DEC_BATCH, RWKV_COLS), 1.0)
    return {
        'x_prompt': x_prompt,
        'x_sample': x_sample,
        'cache_kv': cache_kv,
        'page_table': page_table,
        'state_wkv': state_wkv,
        'state_shift': state_shift,
        'meta_tokens': nrm((N_META, D_MODEL), 1.0),
        'g_final': gain((D_MODEL,)),
        'g_mix': gain((L, D_MODEL)),
        'w_in': nrm((L, D_MODEL, IN_COLS), D_MODEL ** -0.5),
        'g_q': gain((L, Q_LORA)),
        'w_uq': nrm((L, Q_LORA, MLA_HEADS * (QK_NOPE + QK_ROPE)), Q_LORA ** -0.5),
        'g_kv': gain((L, KV_LORA)),
        'w_uk': nrm((L, KV_LORA, MLA_HEADS * QK_NOPE), KV_LORA ** -0.5),
        'w_uv': nrm((L, KV_LORA, MLA_HEADS * V_DIM), KV_LORA ** -0.5),
        'w_o_mla': nrm((L, MLA_HEADS * V_DIM, D_MODEL), (MLA_HEADS * V_DIM) ** -0.5),
        'mu_shift': jax.random.uniform(next(keys), (L, RWKV_COLS), jnp.float32),
        'w0': -3.0 + nrm((L, RWKV_WIDTH), 1.0),
        'w2': nrm((L, W_LORA, RWKV_WIDTH), 0.5 * W_LORA ** -0.5),
        'a0': nrm((L, RWKV_WIDTH), 0.3),
        'a2': nrm((L, A_LORA, RWKV_WIDTH), 0.5 * A_LORA ** -0.5),
        'g2': nrm((L, G_LORA, RWKV_WIDTH), G_LORA ** -0.5),
        'k_k': 0.85 + nrm((L, RWKV_WIDTH), 0.02),
        'k_a': 1.0 + nrm((L, RWKV_WIDTH), 0.02),
        'r_k': nrm((L, RWKV_HEADS, HEAD_SIZE), 0.1),
        'ln_w': gain((L, RWKV_WIDTH)),
        'ln_b': nrm((L, RWKV_WIDTH), 0.02),
        'w_o_rwkv': nrm((L, RWKV_WIDTH, D_MODEL), RWKV_WIDTH ** -0.5),
        'w_out': nrm((L, D_MODEL, D_MODEL), D_MODEL ** -0.5),
        'g_ffn': gain((L, D_MODEL)),
        'w_up': nrm((L, D_MODEL, D_FF), D_MODEL ** -0.5),
        'w_down': nrm((L, D_FF, D_MODEL), D_FF ** -0.5),
    }


def reference(x_prompt, x_sample, cache_kv, page_table, state_wkv, state_shift, meta_tokens, g_final,
              g_mix, w_in, g_q, w_uq, g_kv, w_uk, w_uv, w_o_mla, mu_shift, w0, w2, a0, a2, g2,
              k_k, k_a, r_k, ln_w, ln_b, w_o_rwkv, w_out, g_ffn, w_up, w_down):
    b_p = x_prompt.shape[0]
    s_s = x_sample.shape[1]
    meta = jnp.broadcast_to(meta_tokens[None].astype(x_prompt.dtype), (b_p, N_META, D_MODEL))
    xp = jnp.concatenate([meta, x_prompt], axis=1)
    xs = x_sample
    pos_p = jnp.arange(xp.shape[1])
    pos_s = PAST_LEN + jnp.arange(s_s)
    wkv_zero = jnp.zeros((b_p, RWKV_HEADS, HEAD_SIZE, HEAD_SIZE), jnp.float32)
    shift_zero = jnp.zeros((b_p, RWKV_COLS), jnp.float32)
    kv_p, wkv_p, sh_p, kv_s, wkv_s, sh_s = [], [], [], [], [], []
    for layer in range(DEPTH):
        lw = dict(g_mix=g_mix[layer], w_in=w_in[layer], g_q=g_q[layer], w_uq=w_uq[layer], g_kv=g_kv[layer],
                  w_uk=w_uk[layer], w_uv=w_uv[layer], w_o_mla=w_o_mla[layer], mu_shift=mu_shift[layer],
                  w0=w0[layer], w2=w2[layer], a0=a0[layer], a2=a2[layer], g2=g2[layer], k_k=k_k[layer],
                  k_a=k_a[layer], r_k=r_k[layer], ln_w=ln_w[layer], ln_b=ln_b[layer],
                  w_o_rwkv=w_o_rwkv[layer], w_out=w_out[layer], g_ffn=g_ffn[layer], w_up=w_up[layer],
                  w_down=w_down[layer])
        xp, kv, wkv, sh = hybrid_layer(xp, pos_p, mla_attend_prompt, wkv_zero, shift_zero, lw)
        kv_p.append(kv)
        wkv_p.append(wkv)
        sh_p.append(sh)
        attend_s = functools.partial(mla_attend_sample, cache_kv=cache_kv[layer], page_table=page_table)
        xs, kv, wkv, sh = hybrid_layer(xs, pos_s, attend_s, state_wkv[layer], state_shift[layer], lw)
        kv_s.append(kv)
        wkv_s.append(wkv)
        sh_s.append(sh)
    y_prompt = rms_norm(xp, g_final)[:, N_META:]
    y_sample = rms_norm(xs, g_final)
    return (y_prompt, y_sample,
            jnp.stack(kv_p).astype(cache_kv.dtype),
            jnp.stack(wkv_p).astype(state_wkv.dtype),
            jnp.stack(sh_p).astype(state_shift.dtype),
            jnp.stack(kv_s).astype(cache_kv.dtype),
            jnp.stack(wkv_s).astype(state_wkv.dtype),
            jnp.stack(sh_s).astype(state_shift.dtype))
```

```python
import functools

import jax
import jax.numpy as jnp
from jax import lax
from jax.experimental import pallas as pl
from jax.experimental.pallas import tpu as pltpu

F32 = jnp.float32
BF16 = jnp.bfloat16

D_MODEL = 1024
N_META = 16
PAGE = 128
HEADS = 8
NOPE = 64
ROPE = 32
VDIM = 64
Q_LORA = 384
KV_LORA = 256
KV_W = KV_LORA + ROPE
ROPE_BASE = 10000.0
SM_SCALE = (NOPE + ROPE) ** -0.5
HS = 64
RW_W = HEADS * HS
W_LORA = 64
A_LORA = 64
G_LORA = 128
RW_COLS = 3 * RW_W + W_LORA + A_LORA + G_LORA
GN_EPS = 64e-5
D_FF = 4 * D_MODEL
NORM_EPS = 1e-6
NEG_INF = -1e30
DECAY_SCALE = 0.6065306597126334

LANE = 128
HP = 128
VMEM_LIMIT = 56 * 1024 * 1024

C_GATE = 0
C_RW = 2 * D_MODEL
C_QIN = C_RW + RW_COLS
C_KVIN = C_QIN + Q_LORA
C_KR = C_KVIN + KV_LORA
IN_PERM = C_KR + 2 * LANE


def _dot(a, b):
    return jnp.dot(a, b, preferred_element_type=F32)


def _dot_nt(a, b):
    return lax.dot_general(a, b, (((1,), (1,)), ((), ())), preferred_element_type=F32)


def _dot_tn(a, b):
    return lax.dot_general(a, b, (((0,), (0,)), ((), ())), preferred_element_type=F32)


def _split2(x):
    hi = x.astype(BF16)
    lo = (x - hi.astype(F32)).astype(BF16)
    return hi, lo


def _dot3(a, b, dot=_dot):
    ah, al = _split2(a)
    bh, bl = _split2(b)
    return dot(ah, bh) + dot(al, bh) + dot(ah, bl)


def _rms(x, g):
    return x * lax.rsqrt(jnp.mean(x * x, axis=-1, keepdims=True) + NORM_EPS) * g


def _params(sem):
    return pltpu.CompilerParams(dimension_semantics=sem, vmem_limit_bytes=VMEM_LIMIT)


def _proj_kernel(x_ref, gmix_ref, win_ref, gq_ref, wuq_ref, gkv_ref, wukv_ref,
                 ctq_ref, stq_ref, ctk_ref, stk_ref,
                 gsig_ref, rw_ref, q_ref, kvrow_ref, kpad_ref, v_ref):
    h = _rms(x_ref[...], gmix_ref[...]).astype(BF16)
    step = 512
    for c in range(0, 2 * D_MODEL, step):
        gsig_ref[:, c:c + step] = jax.nn.sigmoid(_dot(h, win_ref[:, C_GATE + c:C_GATE + c + step]))
    for c in range(0, RW_COLS, 256):
        rw_ref[:, c:c + 256] = _dot(h, win_ref[:, C_RW + c:C_RW + c + 256])

    qn = _rms(_dot(h, win_ref[:, C_QIN:C_QIN + Q_LORA]), gq_ref[...]).astype(BF16)
    ctq = ctq_ref[...]
    stq = stq_ref[...]
    for hh in range(HEADS):
        a = _dot(qn, wuq_ref[:, hh * HP:(hh + 1) * HP])
        b = _dot(qn, wuq_ref[:, HEADS * HP + hh * HP:HEADS * HP + (hh + 1) * HP])
        q_ref[:, hh * HP:(hh + 1) * HP] = a * ctq + b * stq

    ckv = _rms(_dot(h, win_ref[:, C_KVIN:C_KVIN + KV_LORA]), gkv_ref[...])
    kvrow_ref[:, 0:KV_LORA] = ckv
    zkr = _dot(h, win_ref[:, C_KR:C_KR + 2 * LANE])
    kr = zkr[:, 0:LANE] * ctk_ref[...] + zkr[:, LANE:2 * LANE] * stk_ref[...]
    kvrow_ref[:, KV_LORA:KV_W] = kr[:, 0:ROPE]
    lane = lax.broadcasted_iota(jnp.int32, kr.shape, 1)
    kr_head = jnp.where((lane >= NOPE) & (lane < NOPE + ROPE), kr, 0.0)
    cb = ckv.astype(BF16)
    for hh in range(HEADS):
        kpad_ref[:, hh * HP:(hh + 1) * HP] = (_dot(cb, wukv_ref[:, hh * HP:(hh + 1) * HP]) + kr_head).astype(BF16)
    v_ref[...] = _dot(cb, wukv_ref[:, HEADS * HP:HEADS * HP + HEADS * VDIM]).astype(BF16)


def _proj(x, tables, wts, tm):
    rows = x.shape[0]
    nt = tables[0].shape[0] // tm
    row = lambda i: (i, 0)
    const = lambda i: (0, 0)
    tab = lambda i: (i % nt, 0)
    full = lambda a: pl.BlockSpec(a.shape, const)
    in_specs = [pl.BlockSpec((tm, D_MODEL), row), full(wts['g_mix']), full(wts['w_in']), full(wts['g_q']),
                full(wts['w_uq']), full(wts['g_kv']), full(wts['w_ukv'])]
    in_specs += [pl.BlockSpec((tm, LANE), tab)] * 4
    widths = [(2 * D_MODEL, F32), (RW_COLS, F32), (HEADS * HP, F32), (KV_W, F32), (HEADS * HP, BF16),
              (HEADS * VDIM, BF16)]
    return pl.pallas_call(
        _proj_kernel,
        grid=(rows // tm,),
        in_specs=in_specs,
        out_specs=[pl.BlockSpec((tm, w), row) for w, _ in widths],
        out_shape=[jax.ShapeDtypeStruct((rows, w), dt) for w, dt in widths],
        compiler_params=_params(("parallel",)),
        name="proj",
    )(x, wts['g_mix'], wts['w_in'], wts['g_q'], wts['w_uq'], wts['g_kv'], wts['w_ukv'], *tables)


def _attn_kernel(q_ref, k_ref, v_ref, km_ref, vm_ref, o_ref, m_scr, l_scr, acc_scr, *, tq, tk, n_meta):
    i = pl.program_id(2)
    j = pl.program_id(3)
    nk = pl.num_programs(3)

    @pl.when(j == 0)
    def _():
        col = lax.broadcasted_iota(jnp.int32, (tq, km_ref.shape[0]), 1)
        for hh in range(2):
            q = q_ref[:, hh * HP:(hh + 1) * HP].astype(BF16)
            s = _dot_nt(q, km_ref[:, hh * HP:(hh + 1) * HP]) * SM_SCALE
            s = jnp.where(col < n_meta, s, NEG_INF)
            m = jnp.max(s, axis=-1, keepdims=True)
            p = jnp.exp(s - m)
            m_scr[hh] = m
            l_scr[hh] = jnp.sum(p, axis=-1, keepdims=True)
            acc_scr[hh] = _dot(p.astype(BF16), vm_ref[:, hh * VDIM:(hh + 1) * VDIM])

    @pl.when(j <= i)
    def _():
        rowp = i * tq + lax.broadcasted_iota(jnp.int32, (tq, tk), 0)
        colp = j * tk + lax.broadcasted_iota(jnp.int32, (tq, tk), 1)
        keep = colp <= rowp
        for hh in range(2):
            q = q_ref[:, hh * HP:(hh + 1) * HP].astype(BF16)
            s = _dot_nt(q, k_ref[:, hh * HP:(hh + 1) * HP]) * SM_SCALE
            s = jnp.where(keep, s, NEG_INF)
            m_prev = m_scr[hh]
            m_new = jnp.maximum(m_prev, jnp.max(s, axis=-1, keepdims=True))
            alpha = jnp.exp(m_prev - m_new)
            p = jnp.exp(s - m_new)
            l_scr[hh] = l_scr[hh] * alpha + jnp.sum(p, axis=-1, keepdims=True)
            acc_scr[hh] = acc_scr[hh] * alpha + _dot(p.astype(BF16), v_ref[:, hh * VDIM:(hh + 1) * VDIM])
            m_scr[hh] = m_new

    @pl.when(j == nk - 1)
    def _():
        outs = [acc_scr[hh] / l_scr[hh] for hh in range(2)]
        o_ref[...] = jnp.concatenate(outs, axis=-1).astype(o_ref.dtype)


def _attn(q, kpad, v, km, vm, batch, seq, tq):
    tk = tq
    nq = seq // tq
    kern = functools.partial(_attn_kernel, tq=tq, tk=tk, n_meta=N_META)
    return pl.pallas_call(
        kern,
        grid=(batch, HEADS // 2, nq, nq),
        in_specs=[
            pl.BlockSpec((tq, 2 * HP), lambda b, h, i, j: (b * nq + i, h)),
            pl.BlockSpec((tk, 2 * HP), lambda b, h, i, j: (b * nq + jnp.minimum(j, i), h)),
            pl.BlockSpec((tk, 2 * VDIM), lambda b, h, i, j: (b * nq + jnp.minimum(j, i), h)),
            pl.BlockSpec((km.shape[0], 2 * HP), lambda b, h, i, j: (0, h)),
            pl.BlockSpec((vm.shape[0], 2 * VDIM), lambda b, h, i, j: (0, h)),
        ],
        out_specs=pl.BlockSpec((tq, 2 * VDIM), lambda b, h, i, j: (b * nq + i, h)),
        out_shape=jax.ShapeDtypeStruct((batch * seq, HEADS * VDIM), BF16),
        scratch_shapes=[pltpu.VMEM((2, tq, 1), F32), pltpu.VMEM((2, tq, 1), F32), pltpu.VMEM((2, tq, VDIM), F32)],
        compiler_params=_params(("parallel", "parallel", "parallel", "arbitrary")),
        name="attn",
    )(q, kpad, v, km, vm)


def _qlat_kernel(q_ref, wuk_ref, o_ref):
    for hh in range(HEADS):
        qn = q_ref[:, hh * HP:hh * HP + NOPE].astype(BF16)
        o_ref[hh, :, 0:KV_LORA] = _dot_nt(qn, wuk_ref[:, hh * NOPE:(hh + 1) * NOPE]) * SM_SCALE
        o_ref[hh, :, KV_LORA:KV_W] = q_ref[:, hh * HP + NOPE:hh * HP + NOPE + ROPE] * SM_SCALE


def _qlat(q, wuk):
    rows = q.shape[0]
    return pl.pallas_call(
        _qlat_kernel,
        out_shape=jax.ShapeDtypeStruct((HEADS, rows, KV_W), F32),
        name="qlat",
    )(q, wuk)


def _sattn_kernel(pt_ref, qf_ref, kvn_ref, wuv_ref, cache_ref, o_ref, buf, sem, *, n_pages, chunk):
    b = pl.program_id(0)
    nb = pl.num_programs(0)

    def page_copy(bb, slot, j):
        return pltpu.make_async_copy(cache_ref.at[pt_ref[bb, j]], buf.at[slot, pl.ds(j * PAGE, PAGE)], sem.at[slot])

    def fetch(bb, slot):
        def body(j, c):
            page_copy(bb, slot, j).start()
            return c
        lax.fori_loop(0, n_pages, body, 0)

    @pl.when(b == 0)
    def _():
        fetch(0, 0)

    @pl.when(b + 1 < nb)
    def _():
        fetch(b + 1, (b + 1) % 2)

    slot = b % 2

    def wait_body(j, c):
        page_copy(b, slot, j).wait()
        return c
    lax.fori_loop(0, n_pages, wait_body, 0)

    qf = qf_ref[0].astype(BF16)
    m = jnp.full((HEADS, 1), NEG_INF, F32)
    l = jnp.zeros((HEADS, 1), F32)
    acc = jnp.zeros((HEADS, KV_LORA), F32)
    for c0 in range(0, n_pages * PAGE, chunk):
        rows = buf[slot, c0:c0 + chunk, :].astype(BF16)
        s = _dot_nt(qf, rows)
        m_new = jnp.maximum(m, jnp.max(s, axis=-1, keepdims=True))
        alpha = jnp.exp(m - m_new)
        p = jnp.exp(s - m_new)
        l = l * alpha + jnp.sum(p, axis=-1, keepdims=True)
        acc = acc * alpha + _dot(p.astype(BF16), rows[:, 0:KV_LORA])
        m = m_new
    kvn = kvn_ref[0].astype(BF16).astype(F32)
    s = jnp.sum(qf.astype(F32) * kvn, axis=-1, keepdims=True)
    m_new = jnp.maximum(m, s)
    alpha = jnp.exp(m - m_new)
    p = jnp.exp(s - m_new)
    l = l * alpha + p
    acc = acc * alpha + p.astype(BF16).astype(F32) * kvn[:, 0:KV_LORA]
    o_lat = (acc / l).astype(BF16)
    full = _dot(o_lat, wuv_ref[...])
    r = lax.broadcasted_iota(jnp.int32, full.shape, 0)
    cidx = lax.broadcasted_iota(jnp.int32, full.shape, 1)
    o_ref[0] = jnp.sum(jnp.where(cidx // VDIM == r, full, 0.0), axis=0, keepdims=True).astype(o_ref.dtype)


def _sattn(page_table, qf, kvn, wuv, cache):
    nb, n_pages = page_table.shape
    chunk = min(2048, n_pages * PAGE)
    kern = functools.partial(_sattn_kernel, n_pages=n_pages, chunk=chunk)
    grid_spec = pltpu.PrefetchScalarGridSpec(
        num_scalar_prefetch=1,
        grid=(nb,),
        in_specs=[
            pl.BlockSpec((1, HEADS, KV_W), lambda b, pt: (b, 0, 0)),
            pl.BlockSpec((1, 1, KV_W), lambda b, pt: (b, 0, 0)),
            pl.BlockSpec(wuv.shape, lambda b, pt: (0, 0)),
            pl.BlockSpec(memory_space=pl.ANY),
        ],
        out_specs=pl.BlockSpec((1, 1, HEADS * VDIM), lambda b, pt: (b, 0, 0)),
        scratch_shapes=[pltpu.VMEM((2, n_pages * PAGE, KV_W), F32), pltpu.SemaphoreType.DMA((2,))],
    )
    return pl.pallas_call(
        kern,
        grid_spec=grid_spec,
        out_shape=jax.ShapeDtypeStruct((nb, 1, HEADS * VDIM), BF16),
        compiler_params=_params(("arbitrary",)),
        name="sattn",
    )(page_table, qf, kvn, wuv, cache)


def _rwkv_kernel(rw_ref, shift_ref, s0_ref, mu_ref, w0_ref, w2_ref, a0_ref, a2_ref, g2_ref, kk_ref, ka_ref,
                 rk_ref, lnw_ref, lnb_ref, seg_ref, tri_ref,
                 o_ref, sout_ref, prev_scr, s_scr, *, chunk, n_valid):
    c = pl.program_id(1)
    nc = pl.num_programs(1)

    @pl.when(c == 0)
    def _():
        prev_scr[...] = shift_ref[0]
        for hh in range(HEADS):
            s_scr[hh] = s0_ref[0, hh]

    cur = rw_ref[...]
    rowi = lax.broadcasted_iota(jnp.int32, cur.shape, 0)
    prev = jnp.where(rowi == 0, prev_scr[...], pltpu.roll(cur, 1, 0))
    prev_scr[...] = cur[chunk - 1:chunk, :]
    z = cur + (prev - cur) * mu_ref[...]
    r = z[:, 0:RW_W]
    k = z[:, RW_W:2 * RW_W]
    v = z[:, 2 * RW_W:3 * RW_W]
    w_in = z[:, 3 * RW_W:3 * RW_W + W_LORA]
    a_in = z[:, 3 * RW_W + W_LORA:3 * RW_W + W_LORA + A_LORA]
    g_in = z[:, 3 * RW_W + W_LORA + A_LORA:RW_COLS]

    seg = seg_ref[...]

    def seg_sum(x):
        hi, lo = _split2(x)
        return _dot(hi, seg) + _dot(lo, seg)

    w_log = w0_ref[...] + _dot(jnp.tanh(w_in).astype(BF16), w2_ref[...])
    logw = -DECAY_SCALE * jax.nn.sigmoid(w_log)
    a_sig = jax.nn.sigmoid(a0_ref[...] + _dot(a_in.astype(BF16), a2_ref[...]))
    g = _dot(jax.nn.sigmoid(g_in).astype(BF16), g2_ref[...])
    kk = k * kk_ref[...]
    kk = kk / jnp.maximum(jnp.sqrt(seg_sum(kk * kk)), 1e-12)
    k_mod = k * (1.0 + (a_sig - 1.0) * ka_ref[...])
    bonus = seg_sum(r * k_mod * rk_ref[...]) * v

    if n_valid is not None:
        live = (c * chunk + lax.broadcasted_iota(jnp.int32, (chunk, RW_W), 0)) < n_valid
        logw = jnp.where(live, logw, 0.0)
        kk = jnp.where(live, kk, 0.0)
        k_mod = jnp.where(live, k_mod, 0.0)
        v = jnp.where(live, v, 0.0)

    gcum = _dot_exact_rhs_t(tri_ref[...], logw)
    eg = jnp.exp(gcum)
    r_t = r * eg
    a_t = -kk * jnp.exp(gcum - logw)
    eng = jnp.exp(-gcum)
    b_t = kk * a_sig * eng
    k_t = k_mod * eng
    decay_c = eg[chunk - 1:chunk, :]

    ti = lax.broadcasted_iota(jnp.int32, (chunk, chunk), 0)
    si = lax.broadcasted_iota(jnp.int32, (chunk, chunk), 1)
    strict = si < ti
    incl = si <= ti
    eye = (si == ti).astype(F32)

    ys = []
    for hh in range(HEADS):
        sl = slice(hh * HS, (hh + 1) * HS)
        ah, rh, bh, kh, vh = a_t[:, sl], r_t[:, sl], b_t[:, sl], k_t[:, sl], v[:, sl]
        s0 = s_scr[hh]
        n = jnp.where(strict, _dot3(ah, bh, _dot_nt), 0.0)
        tinv = eye + n
        pw = n
        span = 2
        while span < chunk:
            pw = _dot3(pw, pw)
            tinv = tinv + _dot3(tinv, pw)
            span *= 2
        bhb, khb, vhb = bh.astype(BF16), kh.astype(BF16), vh.astype(BF16)
        a_ak = jnp.where(strict, _dot_nt(ah.astype(BF16), khb), 0.0)
        rhs = _dot3(ah, s0, _dot_nt) + _dot(a_ak.astype(BF16), vhb)
        u = _dot3(tinv, rhs)
        rhb = rh.astype(BF16)
        a_rb = jnp.where(incl, _dot_nt(rhb, bhb), 0.0)
        a_rk = jnp.where(incl, _dot_nt(rhb, khb), 0.0)
        y = _dot_nt(rhb, s0.astype(BF16)) + _dot(a_rb.astype(BF16), u.astype(BF16)) + _dot(a_rk.astype(BF16), vhb)
        ys.append(y)
        s_new = (s0 + _dot3(u, bh, _dot_tn) + _dot_tn(vhb, khb)) * decay_c[:, sl]
        s_scr[hh] = s_new

    y = jnp.concatenate(ys, axis=-1)
    mean = seg_sum(y) * (1.0 / HS)
    d = y - mean
    var = seg_sum(d * d) * (1.0 / HS)
    yn = d * lax.rsqrt(var + GN_EPS) * lnw_ref[...] + lnb_ref[...]
    o_ref[...] = ((yn + bonus) * g).astype(o_ref.dtype)

    @pl.when(c == nc - 1)
    def _():
        for hh in range(HEADS):
            sout_ref[0, hh] = s_scr[hh]


def _dot_exact_rhs_t(tri_bf16, x):
    x1 = x.astype(BF16)
    r1 = x - x1.astype(F32)
    x2 = r1.astype(BF16)
    x3 = (r1 - x2.astype(F32)).astype(BF16)
    return _dot(tri_bf16, x1) + _dot(tri_bf16, x2) + _dot(tri_bf16, x3)


def _rwkv(rw, shift0, s0, wts, batch, chunk, n_valid):
    rows = rw.shape[0]
    nc = rows // (batch * chunk)
    per_batch = s0.shape[0] == batch and batch > 1
    bsel = (lambda b, c: (b, 0, 0)) if per_batch else (lambda b, c: (0, 0, 0))
    ssel = (lambda b, c: (b, 0, 0, 0)) if per_batch else (lambda b, c: (0, 0, 0, 0))
    const = lambda b, c: (0, 0)
    names = ['mu_shift', 'w0', 'w2', 'a0', 'a2', 'g2', 'k_k', 'k_a', 'r_k', 'ln_w', 'ln_b', 'seg']
    consts = [wts[n] for n in names]
    tri = (jnp.arange(chunk)[:, None] >= jnp.arange(chunk)[None, :]).astype(BF16)
    kern = functools.partial(_rwkv_kernel, chunk=chunk, n_valid=n_valid)
    return pl.pallas_call(
        kern,
        grid=(batch, nc),
        in_specs=[pl.BlockSpec((chunk, RW_COLS), lambda b, c: (b * nc + c, 0)),
                  pl.BlockSpec((1, 1, RW_COLS), bsel),
                  pl.BlockSpec((1, HEADS, HS, HS), ssel)]
        + [pl.BlockSpec(a.shape, const) for a in consts]
        + [pl.BlockSpec(tri.shape, const)],
        out_specs=[pl.BlockSpec((chunk, RW_W), lambda b, c: (b * nc + c, 0)),
                   pl.BlockSpec((1, HEADS, HS, HS), lambda b, c: (b, 0, 0, 0))],
        out_shape=[jax.ShapeDtypeStruct((rows, RW_W), BF16),
                   jax.ShapeDtypeStruct((batch, HEADS, HS, HS), F32)],
        scratch_shapes=[pltpu.VMEM((1, RW_COLS), F32), pltpu.VMEM((HEADS, HS, HS), F32)],
        compiler_params=_params(("parallel", "arbitrary")),
        name="rwkv",
    )(rw, shift0, s0, *consts, tri)


def _merge_kernel(x_ref, om_ref, or_ref, gs_ref, wom_ref, wor_ref, wout_ref, gffn_ref, x1_ref, h2_ref):
    a = _dot(om_ref[...], wom_ref[...])
    r = _dot(or_ref[...], wor_ref[...])
    mix = gs_ref[:, 0:D_MODEL] * a + gs_ref[:, D_MODEL:2 * D_MODEL] * r
    x1 = x_ref[...] + _dot(mix.astype(BF16), wout_ref[...])
    x1_ref[...] = x1
    h2_ref[...] = _rms(x1, gffn_ref[...]).astype(BF16)


def _merge(x, om, orw, gsig, wts, tm):
    rows = x.shape[0]
    row = lambda i: (i, 0)
    const = lambda i: (0, 0)
    full = lambda a: pl.BlockSpec(a.shape, const)
    return pl.pallas_call(
        _merge_kernel,
        grid=(rows // tm,),
        in_specs=[pl.BlockSpec((tm, D_MODEL), row), pl.BlockSpec((tm, HEADS * VDIM), row),
                  pl.BlockSpec((tm, RW_W), row), pl.BlockSpec((tm, 2 * D_MODEL), row),
                  full(wts['w_o_mla']), full(wts['w_o_rwkv']), full(wts['w_out']), full(wts['g_ffn'])],
        out_specs=[pl.BlockSpec((tm, D_MODEL), row), pl.BlockSpec((tm, D_MODEL), row)],
        out_shape=[jax.ShapeDtypeStruct((rows, D_MODEL), F32), jax.ShapeDtypeStruct((rows, D_MODEL), BF16)],
        compiler_params=_params(("parallel",)),
        name="merge",
    )(x, om, orw, gsig, wts['w_o_mla'], wts['w_o_rwkv'], wts['w_out'], wts['g_ffn'])


def _ffn_kernel(x1_ref, h2_ref, wup_ref, wdn_ref, gfin_ref, y_ref):
    u = jnp.maximum(_dot(h2_ref[...], wup_ref[...]), 0.0)
    x2 = x1_ref[...] + _dot((u * u).astype(BF16), wdn_ref[...])
    y_ref[...] = _rms(x2, gfin_ref[...])


def _ffn(x1, h2, wts, tm):
    rows = x1.shape[0]
    row = lambda i: (i, 0)
    const = lambda i: (0, 0)
    full = lambda a: pl.BlockSpec(a.shape, const)
    return pl.pallas_call(
        _ffn_kernel,
        grid=(rows // tm,),
        in_specs=[pl.BlockSpec((tm, D_MODEL), row), pl.BlockSpec((tm, D_MODEL), row),
                  full(wts['w_up']), full(wts['w_down']), full(wts['g_final'])],
        out_specs=pl.BlockSpec((tm, D_MODEL), row),
        out_shape=jax.ShapeDtypeStruct((rows, D_MODEL), F32),
        compiler_params=_params(("parallel",)),
        name="ffn",
    )(x1, h2, wts['w_up'], wts['w_down'], wts['g_final'])


def _swap_halves(w):
    return jnp.concatenate([w[..., ROPE // 2:], w[..., :ROPE // 2]], axis=-1)


def _prep_weights(g_final, g_mix, w_in, g_q, w_uq, g_kv, w_uk, w_uv, w_o_mla, mu_shift, w0, w2, a0, a2, g2,
                  k_k, k_a, r_k, ln_w, ln_b, w_o_rwkv, w_out, g_ffn, w_up, w_down):
    row = lambda a: a.reshape(1, -1).astype(F32)
    q_in = w_in[:, 0:Q_LORA]
    kv_in = w_in[:, Q_LORA:Q_LORA + KV_LORA]
    kr = w_in[:, Q_LORA + KV_LORA:Q_LORA + KV_LORA + ROPE]
    o_rw = Q_LORA + KV_LORA + ROPE
    rw = w_in[:, o_rw:o_rw + RW_COLS]
    gate = w_in[:, o_rw + RW_COLS:]
    zr = jnp.zeros((D_MODEL, ROPE), F32)
    kr_a = jnp.concatenate([kr, zr, kr, zr], axis=1)
    krs = _swap_halves(kr)
    kr_b = jnp.concatenate([krs, zr, krs, zr], axis=1)
    w_in_p = jnp.concatenate([gate, rw, q_in, kv_in, kr_a, kr_b], axis=1).astype(BF16)

    uq = w_uq.reshape(Q_LORA, HEADS, NOPE + ROPE)
    zq = jnp.zeros((Q_LORA, HEADS, HP - NOPE - ROPE), F32)
    uq_pad = jnp.concatenate([uq, zq], axis=-1).reshape(Q_LORA, HEADS * HP)
    uq_sw = jnp.concatenate([jnp.zeros((Q_LORA, HEADS, NOPE), F32), _swap_halves(uq[..., NOPE:]), zq], axis=-1)
    w_uq_p = jnp.concatenate([uq_pad, uq_sw.reshape(Q_LORA, HEADS * HP)], axis=1).astype(BF16)

    uk = w_uk.reshape(KV_LORA, HEADS, NOPE)
    uk_pad = jnp.concatenate([uk, jnp.zeros((KV_LORA, HEADS, HP - NOPE), F32)], axis=-1).reshape(KV_LORA, HEADS * HP)
    w_ukv = jnp.concatenate([uk_pad, w_uv], axis=1).astype(BF16)

    head = jnp.arange(RW_W) // HS
    seg = (head[:, None] == head[None, :]).astype(BF16)
    return dict(
        g_mix=row(g_mix), w_in=w_in_p, g_q=row(g_q), w_uq=w_uq_p, g_kv=row(g_kv), w_ukv=w_ukv,
        w_uk=w_uk.astype(BF16), w_uv=w_uv.astype(BF16), w_o_mla=w_o_mla.astype(BF16),
        mu_shift=row(mu_shift), w0=row(w0), w2=w2.astype(BF16), a0=row(a0), a2=a2.astype(BF16),
        g2=g2.astype(BF16), k_k=row(k_k), k_a=row(k_a), r_k=row(r_k), ln_w=row(ln_w), ln_b=row(ln_b), seg=seg,
        w_o_rwkv=w_o_rwkv.astype(BF16), w_out=w_out.astype(BF16), g_ffn=row(g_ffn),
        w_up=w_up.astype(BF16), w_down=w_down.astype(BF16), g_final=row(g_final))


def _rope_tables(pos):
    inv = ROPE_BASE ** (-jnp.arange(0, ROPE, 2, dtype=F32) / ROPE)
    ang = pos.astype(F32)[:, None] * inv[None, :]
    cos, sin = jnp.cos(ang), jnp.sin(ang)
    n = pos.shape[0]
    cc = jnp.concatenate([cos, cos], axis=1)
    ss = jnp.concatenate([-sin, sin], axis=1)
    z32 = jnp.zeros((n, ROPE), F32)
    ctq = jnp.concatenate([jnp.ones((n, NOPE), F32), cc, z32], axis=1)
    stq = jnp.concatenate([jnp.zeros((n, NOPE), F32), ss, z32], axis=1)
    ctk = jnp.concatenate([cc, z32, cc, z32], axis=1)
    stk = jnp.concatenate([ss, z32, ss, z32], axis=1)
    return ctq, stq, ctk, stk


def _row_tile(rows, cap):
    tm = min(rows, cap)
    assert rows % tm == 0
    return tm


def kernel(x_prompt, x_sample, cache_kv, page_table, state_wkv, state_shift, meta_tokens, g_final, g_mix, w_in, g_q, w_uq, g_kv, w_uk, w_uv, w_o_mla, mu_shift, w0, w2, a0, a2, g2, k_k, k_a, r_k, ln_w, ln_b, w_o_rwkv, w_out, g_ffn, w_up, w_down):
    depth = g_mix.shape[0]
    assert depth == 1
    bp, seq, _ = x_prompt.shape
    bd, s_s, _ = x_sample.shape
    assert s_s == 1
    n_pages = page_table.shape[1]
    past_len = n_pages * PAGE
    wts = _prep_weights(g_final, g_mix[0], w_in[0], g_q[0], w_uq[0], g_kv[0], w_uk[0], w_uv[0], w_o_mla[0],
                        mu_shift[0], w0[0], w2[0], a0[0], a2[0], g2[0], k_k[0], k_a[0], r_k[0], ln_w[0], ln_b[0],
                        w_o_rwkv[0], w_out[0], g_ffn[0], w_up[0], w_down[0])

    tab_m = _rope_tables(jnp.arange(N_META))
    _, rw_m, _, kvrow_m, kpad_m, v_m = _proj(meta_tokens.astype(F32), tab_m, wts, N_META)
    c_meta = 64
    rw_m_pad = jnp.concatenate([rw_m, jnp.zeros((c_meta - N_META, RW_COLS), F32)], axis=0)
    zero_shift = jnp.zeros((1, 1, RW_COLS), F32)
    zero_state = jnp.zeros((1, HEADS, HS, HS), F32)
    _, s_meta = _rwkv(rw_m_pad, zero_shift, zero_state, wts, 1, c_meta, N_META)
    shift_meta = rw_m[N_META - 1:N_META].reshape(1, 1, RW_COLS)
    km = jnp.concatenate([kpad_m, jnp.zeros((LANE - N_META, HEADS * HP), BF16)], axis=0)
    vm = jnp.concatenate([v_m, jnp.zeros((LANE - N_META, HEADS * VDIM), BF16)], axis=0)

    rows_p = bp * seq
    tm = _row_tile(seq, 256)
    xp = x_prompt.reshape(rows_p, D_MODEL)
    tab_p = _rope_tables(N_META + jnp.arange(seq))
    gsig, rw, q, kvrow, kpad, v = _proj(xp, tab_p, wts, tm)
    o_mla = _attn(q, kpad, v, km, vm, bp, seq, tm)
    c_p = _row_tile(seq, 64)
    o_rwkv, s_p = _rwkv(rw, shift_meta, s_meta, wts, bp, c_p, None)
    x1, h2 = _merge(xp, o_mla, o_rwkv, gsig, wts, tm)
    y_prompt = _ffn(x1, h2, wts, tm).reshape(bp, seq, D_MODEL)
    kv_prompt = jnp.concatenate([jnp.broadcast_to(kvrow_m[None], (bp, N_META, KV_W)),
                                 kvrow.reshape(bp, seq, KV_W)], axis=1)[None]
    shift_prompt = rw.reshape(bp, seq, RW_COLS)[:, -1][None]

    xs = x_sample.reshape(bd, D_MODEL)
    tab_s = _rope_tables(jnp.full((bd,), past_len, jnp.int32))
    gsig_s, rw_s, q_s, kvrow_s, _, _ = _proj(xs, tab_s, wts, bd)
    qf = jnp.transpose(_qlat(q_s, wts['w_uk']), (1, 0, 2))
    cache = cache_kv.reshape(cache_kv.shape[1], PAGE, KV_W)
    o_mla_s = _sattn(page_table, qf, kvrow_s.reshape(bd, 1, KV_W), wts['w_uv'], cache).reshape(bd, HEADS * VDIM)
    c_s = 16
    rw_s_pad = jnp.concatenate([rw_s[:, None, :], jnp.zeros((bd, c_s - 1, RW_COLS), F32)], axis=1)
    o_rwkv_s, s_s_new = _rwkv(rw_s_pad.reshape(bd * c_s, RW_COLS), state_shift[0].reshape(bd, 1, RW_COLS),
                              state_wkv[0], wts, bd, c_s, 1)
    o_rwkv_s = o_rwkv_s.reshape(bd, c_s, RW_W)[:, 0]
    x1_s, h2_s = _merge(xs, o_mla_s, o_rwkv_s, gsig_s, wts, bd)
    y_sample = _ffn(x1_s, h2_s, wts, bd).reshape(bd, 1, D_MODEL)

    return (y_prompt, y_sample,
            kv_prompt.astype(cache_kv.dtype), s_p[None].astype(state_wkv.dtype),
            shift_prompt.astype(state_shift.dtype),
            kvrow_s.reshape(1, bd, 1, KV_W).astype(cache_kv.dtype), s_s_new[None].astype(state_wkv.dtype),
            rw_s[None].astype(state_shift.dtype))
```

```python
import functools

import jax
import jax.numpy as jnp
from jax import lax
from jax.experimental import pallas as pl
from jax.experimental.pallas import tpu as pltpu

F32 = jnp.float32
BF16 = jnp.bfloat16

D_MODEL = 1024
N_META = 16
PAGE = 128
HEADS = 8
NOPE = 64
ROPE = 32
VDIM = 64
Q_LORA = 384
KV_LORA = 256
KV_W = KV_LORA + ROPE
ROPE_BASE = 10000.0
SM_SCALE = (NOPE + ROPE) ** -0.5
HS = 64
RW_W = HEADS * HS
W_LORA = 64
A_LORA = 64
G_LORA = 128
RW_COLS = 3 * RW_W + W_LORA + A_LORA + G_LORA
GN_EPS = 64e-5
D_FF = 4 * D_MODEL
NORM_EPS = 1e-6
NEG_INF = -1e30
DECAY_SCALE = 0.6065306597126334

LANE = 128
HP = 128
VMEM_LIMIT = 56 * 1024 * 1024

C_GATE = 0
C_RW = 2 * D_MODEL
C_QIN = C_RW + RW_COLS
C_KVIN = C_QIN + Q_LORA
C_KR = C_KVIN + KV_LORA
IN_PERM = C_KR + 2 * LANE


def _dot(a, b):
    return jnp.dot(a, b, preferred_element_type=F32)


def _dot_nt(a, b):
    return lax.dot_general(a, b, (((1,), (1,)), ((), ())), preferred_element_type=F32)


def _dot_tn(a, b):
    return lax.dot_general(a, b, (((0,), (0,)), ((), ())), preferred_element_type=F32)


def _split2(x):
    hi = x.astype(BF16)
    lo = (x - hi.astype(F32)).astype(BF16)
    return hi, lo


def _split3(x):
    x1 = x.astype(BF16)
    r1 = x - x1.astype(F32)
    x2 = r1.astype(BF16)
    x3 = (r1 - x2.astype(F32)).astype(BF16)
    return x1, x2, x3


def _rms(x, g):
    return x * lax.rsqrt(jnp.mean(x * x, axis=-1, keepdims=True) + NORM_EPS) * g


def _params(sem):
    return pltpu.CompilerParams(dimension_semantics=sem, vmem_limit_bytes=VMEM_LIMIT)


def _proj_kernel(x_ref, gmix_ref, win_ref, gq_ref, wuq_ref, gkv_ref, wukv_ref, wuvt_ref,
                 ctq_ref, stq_ref, ctk_ref, stk_ref,
                 gsig_ref, rw_ref, q_ref, kvrow_ref, kpad_ref, vt_ref):
    h = _rms(x_ref[...], gmix_ref[...]).astype(BF16)
    step = 512
    for c in range(0, 2 * D_MODEL, step):
        gsig_ref[:, c:c + step] = jax.nn.sigmoid(_dot(h, win_ref[:, C_GATE + c:C_GATE + c + step]))
    for c in range(0, RW_COLS, 256):
        rw_ref[:, c:c + 256] = _dot(h, win_ref[:, C_RW + c:C_RW + c + 256])

    qn = _rms(_dot(h, win_ref[:, C_QIN:C_QIN + Q_LORA]), gq_ref[...]).astype(BF16)
    ctq = ctq_ref[...]
    stq = stq_ref[...]
    for hh in range(HEADS):
        a = _dot(qn, wuq_ref[:, hh * HP:(hh + 1) * HP])
        b = _dot(qn, wuq_ref[:, HEADS * HP + hh * HP:HEADS * HP + (hh + 1) * HP])
        q_ref[:, hh * HP:(hh + 1) * HP] = (a * ctq + b * stq).astype(q_ref.dtype)

    ckv = _rms(_dot(h, win_ref[:, C_KVIN:C_KVIN + KV_LORA]), gkv_ref[...])
    kvrow_ref[:, 0:KV_LORA] = ckv
    zkr = _dot(h, win_ref[:, C_KR:C_KR + 2 * LANE])
    kr = zkr[:, 0:LANE] * ctk_ref[...] + zkr[:, LANE:2 * LANE] * stk_ref[...]
    kvrow_ref[:, KV_LORA:KV_W] = kr[:, 0:ROPE]
    lane = lax.broadcasted_iota(jnp.int32, kr.shape, 1)
    kr_head = jnp.where((lane >= NOPE) & (lane < NOPE + ROPE), kr, 0.0)
    cb = ckv.astype(BF16)
    for hh in range(HEADS):
        kpad_ref[:, hh * HP:(hh + 1) * HP] = (_dot(cb, wukv_ref[:, hh * HP:(hh + 1) * HP]) + kr_head).astype(BF16)
    vt_ref[0] = _dot_nt(wuvt_ref[...], cb).astype(BF16)


def _proj(x, tables, wts, tm, q_dtype):
    rows = x.shape[0]
    nt = tables[0].shape[0] // tm
    row = lambda i: (i, 0)
    const = lambda i: (0, 0)
    tab = lambda i: (i % nt, 0)
    full = lambda a: pl.BlockSpec(a.shape, const)
    in_specs = [pl.BlockSpec((tm, D_MODEL), row), full(wts['g_mix']), full(wts['w_in']), full(wts['g_q']),
                full(wts['w_uq']), full(wts['g_kv']), full(wts['w_ukv']), full(wts['w_uvt'])]
    in_specs += [pl.BlockSpec((tm, LANE), tab)] * 4
    widths = [(2 * D_MODEL, F32), (RW_COLS, F32), (HEADS * HP, q_dtype), (KV_W, F32), (HEADS * HP, BF16)]
    nt_rows = rows // tm
    return pl.pallas_call(
        _proj_kernel,
        grid=(nt_rows,),
        in_specs=in_specs,
        out_specs=[pl.BlockSpec((tm, w), row) for w, _ in widths]
        + [pl.BlockSpec((1, HEADS * VDIM, tm), lambda i: (i, 0, 0))],
        out_shape=[jax.ShapeDtypeStruct((rows, w), dt) for w, dt in widths]
        + [jax.ShapeDtypeStruct((nt_rows, HEADS * VDIM, tm), BF16)],
        compiler_params=_params(("parallel",)),
        name="proj",
    )(x, wts['g_mix'], wts['w_in'], wts['g_q'], wts['w_uq'], wts['g_kv'], wts['w_ukv'], wts['w_uvt'], *tables)


def _attn_kernel(q_ref, k_ref, vt_ref, km_ref, vmt_ref, eye_ref, o_ref, *, tq):
    i = pl.program_id(2)
    qs = [q_ref[:, hh * HP:(hh + 1) * HP] for hh in range(2)]

    def update(carry, hh, k_tile, vt_tile, mask):
        m, l, acc = carry
        s = _dot_nt(k_tile, qs[hh]) * SM_SCALE
        if mask is not None:
            s = jnp.where(mask, s, NEG_INF)
        m_new = jnp.maximum(m, jnp.max(s, axis=0, keepdims=True))
        alpha = jnp.exp(m - m_new)
        p = jnp.exp(s - m_new)
        l = l * alpha + jnp.sum(p, axis=0, keepdims=True)
        acc = acc * alpha + _dot(vt_tile, p.astype(BF16))
        return m_new, l, acc

    init = (jnp.full((1, tq), NEG_INF, F32), jnp.zeros((1, tq), F32), jnp.zeros((VDIM, tq), F32))
    carry = tuple(update(init, hh, km_ref[:, hh * HP:(hh + 1) * HP], vmt_ref[0, hh * VDIM:(hh + 1) * VDIM, :], None)
                  for hh in range(2))

    def tile(j, carry, mask):
        rows = pl.ds(pl.multiple_of(j * tq, tq), tq)
        return tuple(update(carry[hh], hh, k_ref[rows, hh * HP:(hh + 1) * HP],
                            vt_ref[j, hh * VDIM:(hh + 1) * VDIM, :], mask) for hh in range(2))

    carry = lax.fori_loop(0, i, lambda j, c: tile(j, c, None), carry)
    key = lax.broadcasted_iota(jnp.int32, (tq, tq), 0)
    qry = lax.broadcasted_iota(jnp.int32, (tq, tq), 1)
    carry = tile(i, carry, key <= qry)

    outs = []
    for hh in range(2):
        m, l, acc = carry[hh]
        outs.append(_dot_tn((acc / l).astype(BF16), eye_ref[...]))
    o_ref[...] = jnp.concatenate(outs, axis=-1).astype(o_ref.dtype)


def _attn(q, kpad, vt, km, vmt, batch, seq, tq):
    nq = seq // tq
    eye = jnp.eye(VDIM, dtype=BF16)
    kern = functools.partial(_attn_kernel, tq=tq)
    return pl.pallas_call(
        kern,
        grid=(batch, HEADS // 2, nq),
        in_specs=[
            pl.BlockSpec((tq, 2 * HP), lambda b, h, i: (b * nq + i, h)),
            pl.BlockSpec((seq, 2 * HP), lambda b, h, i: (b, h)),
            pl.BlockSpec((nq, 2 * VDIM, tq), lambda b, h, i: (b, h, 0)),
            pl.BlockSpec((km.shape[0], 2 * HP), lambda b, h, i: (0, h)),
            pl.BlockSpec((1, 2 * VDIM, vmt.shape[2]), lambda b, h, i: (0, h, 0)),
            pl.BlockSpec(eye.shape, lambda b, h, i: (0, 0)),
        ],
        out_specs=pl.BlockSpec((tq, 2 * VDIM), lambda b, h, i: (b * nq + i, h)),
        out_shape=jax.ShapeDtypeStruct((batch * seq, HEADS * VDIM), BF16),
        compiler_params=_params(("parallel", "parallel", "arbitrary")),
        name="attn",
    )(q, kpad, vt, km, vmt, eye)


def _qlat_kernel(q_ref, wuk_ref, o_ref):
    for hh in range(HEADS):
        qn = q_ref[:, hh * HP:hh * HP + NOPE].astype(BF16)
        o_ref[hh, :, 0:KV_LORA] = _dot_nt(qn, wuk_ref[:, hh * NOPE:(hh + 1) * NOPE]) * SM_SCALE
        o_ref[hh, :, KV_LORA:KV_W] = q_ref[:, hh * HP + NOPE:hh * HP + NOPE + ROPE] * SM_SCALE


def _qlat(q, wuk):
    rows = q.shape[0]
    return pl.pallas_call(
        _qlat_kernel,
        out_shape=jax.ShapeDtypeStruct((HEADS, rows, KV_W), F32),
        name="qlat",
    )(q, wuk)


def _sattn_kernel(pt_ref, qf_ref, kvn_ref, wuv_ref, cache_ref, o_ref, buf, sem, *, n_pages, chunk):
    b = pl.program_id(0)
    nb = pl.num_programs(0)

    def page_copy(bb, slot, j):
        return pltpu.make_async_copy(cache_ref.at[pt_ref[bb, j]], buf.at[slot, :, pl.ds(j * PAGE, PAGE)], sem.at[slot])

    def fetch(bb, slot):
        def body(j, c):
            page_copy(bb, slot, j).start()
            return c
        lax.fori_loop(0, n_pages, body, 0)

    @pl.when(b == 0)
    def _():
        fetch(0, 0)

    @pl.when(b + 1 < nb)
    def _():
        fetch(b + 1, (b + 1) % 2)

    slot = b % 2

    def wait_body(j, c):
        page_copy(b, slot, j).wait()
        return c
    lax.fori_loop(0, n_pages, wait_body, 0)

    qf = qf_ref[0].astype(BF16)
    m = jnp.full((HEADS, 1), NEG_INF, F32)
    l = jnp.zeros((HEADS, 1), F32)
    acc = jnp.zeros((HEADS, KV_LORA), F32)
    for c0 in range(0, n_pages * PAGE, chunk):
        kt = buf[slot, :, c0:c0 + chunk].astype(BF16)
        s = _dot(qf, kt)
        m_new = jnp.maximum(m, jnp.max(s, axis=-1, keepdims=True))
        alpha = jnp.exp(m - m_new)
        p = jnp.exp(s - m_new)
        l = l * alpha + jnp.sum(p, axis=-1, keepdims=True)
        acc = acc * alpha + _dot_nt(p.astype(BF16), kt[0:KV_LORA, :])
        m = m_new
    kvn = kvn_ref[0].astype(BF16).astype(F32)
    s = jnp.sum(qf.astype(F32) * kvn, axis=-1, keepdims=True)
    m_new = jnp.maximum(m, s)
    alpha = jnp.exp(m - m_new)
    p = jnp.exp(s - m_new)
    l = l * alpha + p
    acc = acc * alpha + p.astype(BF16).astype(F32) * kvn[:, 0:KV_LORA]
    o_lat = (acc / l).astype(BF16)
    full = _dot(o_lat, wuv_ref[...])
    r = lax.broadcasted_iota(jnp.int32, full.shape, 0)
    cidx = lax.broadcasted_iota(jnp.int32, full.shape, 1)
    o_ref[0] = jnp.sum(jnp.where(cidx // VDIM == r, full, 0.0), axis=0, keepdims=True).astype(o_ref.dtype)


def _sattn(page_table, qf, kvn, wuv, cache):
    nb, n_pages = page_table.shape
    chunk = min(2048, n_pages * PAGE)
    kern = functools.partial(_sattn_kernel, n_pages=n_pages, chunk=chunk)
    grid_spec = pltpu.PrefetchScalarGridSpec(
        num_scalar_prefetch=1,
        grid=(nb,),
        in_specs=[
            pl.BlockSpec((1, HEADS, KV_W), lambda b, pt: (b, 0, 0)),
            pl.BlockSpec((1, 1, KV_W), lambda b, pt: (b, 0, 0)),
            pl.BlockSpec(wuv.shape, lambda b, pt: (0, 0)),
            pl.BlockSpec(memory_space=pl.ANY),
        ],
        out_specs=pl.BlockSpec((1, 1, HEADS * VDIM), lambda b, pt: (b, 0, 0)),
        scratch_shapes=[pltpu.VMEM((2, KV_W, n_pages * PAGE), F32), pltpu.SemaphoreType.DMA((2,))],
    )
    return pl.pallas_call(
        kern,
        grid_spec=grid_spec,
        out_shape=jax.ShapeDtypeStruct((nb, 1, HEADS * VDIM), BF16),
        compiler_params=_params(("arbitrary",)),
        name="sattn",
    )(page_table, qf, kvn, wuv, cache)


GROUP = 4
GW = GROUP * HS
RW_PARAM_NAMES = ['mu_shift', 'w0', 'w2', 'a0', 'a2', 'g2', 'k_k', 'k_a', 'r_k', 'ln_w', 'ln_b', 'seg']


def _seg_sum(x, seg):
    hi, lo = _split2(x)
    return _dot(hi, seg) + _dot(lo, seg)


def _rwkv_token_prep(cur, prev, p):
    z = cur + (prev - cur) * p['mu_shift']
    r = z[:, 0:RW_W]
    k = z[:, RW_W:2 * RW_W]
    v = z[:, 2 * RW_W:3 * RW_W]
    w_in = z[:, 3 * RW_W:3 * RW_W + W_LORA]
    a_in = z[:, 3 * RW_W + W_LORA:3 * RW_W + W_LORA + A_LORA]
    g_in = z[:, 3 * RW_W + W_LORA + A_LORA:RW_COLS]
    w_log = p['w0'] + _dot(jnp.tanh(w_in).astype(BF16), p['w2'])
    logw = -DECAY_SCALE * jax.nn.sigmoid(w_log)
    a_sig = jax.nn.sigmoid(p['a0'] + _dot(a_in.astype(BF16), p['a2']))
    g = _dot(jax.nn.sigmoid(g_in).astype(BF16), p['g2'])
    kk = k * p['k_k']
    kk = kk / jnp.maximum(jnp.sqrt(_seg_sum(kk * kk, p['seg'])), 1e-12)
    k_mod = k * (1.0 + (a_sig - 1.0) * p['k_a'])
    bonus = _seg_sum(r * k_mod * p['r_k'], p['seg']) * v
    return r, k_mod, v, logw, a_sig, g, kk, bonus


def _group_norm_out(y, bonus, g, p):
    mean = _seg_sum(y, p['seg']) * (1.0 / HS)
    d = y - mean
    var = _seg_sum(d * d, p['seg']) * (1.0 / HS)
    yn = d * lax.rsqrt(var + GN_EPS) * p['ln_w'] + p['ln_b']
    return (yn + bonus) * g


def _rwkv_kernel(rw_ref, shift_ref, m0_ref, *rest, n_valid):
    prm_refs = rest[:len(RW_PARAM_NAMES)]
    tri_ref, ones_ref, o_ref, mout_ref, prev_scr, m_scr = rest[len(RW_PARAM_NAMES):]
    p = {n: ref[...] for n, ref in zip(RW_PARAM_NAMES, prm_refs)}
    chunk = HS
    c = pl.program_id(1)
    nc = pl.num_programs(1)

    @pl.when(c == 0)
    def _():
        prev_scr[...] = shift_ref[0]
        m_scr[...] = m0_ref[0]

    cur = rw_ref[...]
    rowi = lax.broadcasted_iota(jnp.int32, cur.shape, 0)
    prev = jnp.where(rowi == 0, prev_scr[...], pltpu.roll(cur, 1, 0))
    prev_scr[...] = cur[chunk - 1:chunk, :]
    r, k_mod, v, logw, a_sig, g, kk, bonus = _rwkv_token_prep(cur, prev, p)

    if n_valid is not None:
        live = (c * chunk + lax.broadcasted_iota(jnp.int32, (chunk, RW_W), 0)) < n_valid
        logw = jnp.where(live, logw, 0.0)
        kk = jnp.where(live, kk, 0.0)
        k_mod = jnp.where(live, k_mod, 0.0)
        v = jnp.where(live, v, 0.0)

    gcum = sum(_dot(tri_ref[...], t) for t in _split3(logw))
    g_end = gcum[chunk - 1:chunk, :]
    r_t = r * jnp.exp(gcum)
    a_t = -kk * jnp.exp(gcum - logw)
    eng = jnp.exp(-gcum)
    b_t = kk * a_sig * eng
    k_t = k_mod * eng
    tail = jnp.exp(g_end - gcum)
    b_e = kk * a_sig * tail
    k_e = k_mod * tail
    decay_c = jnp.exp(g_end)

    ti = lax.broadcasted_iota(jnp.int32, (chunk, GW), 0)
    lane = lax.broadcasted_iota(jnp.int32, (chunk, GW), 1)
    si = lane % HS
    lane_head = lane // HS
    strict = si < ti
    incl = si <= ti
    eye = (si == ti).astype(F32)

    def bdr(zf):
        return jnp.concatenate([jnp.where(lane_head == hh, zf, 0.0) for hh in range(GROUP)], axis=0).astype(BF16)

    ys = []
    for gi in range(HEADS // GROUP):
        sl = slice(gi * GW, (gi + 1) * GW)
        a_g, r_g, v_g = a_t[:, sl], r_t[:, sl], v[:, sl]
        ar = jnp.concatenate([a_g, r_g], axis=0).astype(BF16)
        ab = _dot_nt(ar, bdr(b_t[:, sl]))
        ak = _dot_nt(ar, bdr(k_t[:, sl]))
        n = jnp.where(strict, ab[0:chunk], 0.0)
        a_rb = jnp.where(incl, ab[chunk:2 * chunk], 0.0)
        a_ak = jnp.where(strict, ak[0:chunk], 0.0)
        a_rk = jnp.where(incl, ak[chunk:2 * chunk], 0.0)

        x = eye + n
        pw = _dot(n.astype(BF16), bdr(n))
        for _ in range(4):
            z = _dot(jnp.concatenate([x, pw], axis=0).astype(BF16), bdr(pw))
            x = x + z[0:chunk]
            pw = z[chunk:2 * chunk]
        x = x + _dot(x.astype(BF16), bdr(pw))

        m0 = m_scr[gi]
        bm, bv = bdr(m0), bdr(v_g)
        rhs = _dot(jnp.concatenate([a_g, a_ak], axis=1).astype(BF16), jnp.concatenate([bm, bv], axis=0))
        u = _dot(x.astype(BF16), bdr(rhs))
        y = _dot(jnp.concatenate([r_g, a_rb, a_rk], axis=1).astype(BF16),
                 jnp.concatenate([bm, bdr(u), bv], axis=0))
        ys.append(y)
        full = _dot_tn(jnp.concatenate([b_e[:, sl], k_e[:, sl]], axis=0).astype(BF16),
                       jnp.concatenate([u, v_g], axis=0).astype(BF16))
        upd = jnp.where(lane_head == 0, full[0:HS], 0.0)
        for hh in range(1, GROUP):
            upd = upd + jnp.where(lane_head == hh, full[hh * HS:(hh + 1) * HS], 0.0)
        gm = sum(_dot(t, ones_ref[...]) for t in _split3(eye * decay_c[:, sl]))
        m_scr[gi] = gm * m0 + upd

    y = jnp.concatenate(ys, axis=-1)
    o_ref[...] = _group_norm_out(y, bonus, g, p).astype(o_ref.dtype)

    @pl.when(c == nc - 1)
    def _():
        mout_ref[0] = m_scr[...]


def _state_to_m(s):
    b = s.shape[0]
    return jnp.transpose(s.reshape(b, HEADS // GROUP, GROUP, HS, HS), (0, 1, 4, 2, 3)).reshape(b, HEADS // GROUP, HS, GW)


def _m_to_state(m):
    b = m.shape[0]
    return jnp.transpose(m.reshape(b, HEADS // GROUP, HS, GROUP, HS), (0, 1, 3, 4, 2)).reshape(b, HEADS, HS, HS)


def _rwkv(rw, shift0, s0, wts, batch, n_valid):
    chunk = HS
    rows = rw.shape[0]
    nc = rows // (batch * chunk)
    const2 = lambda b, c: (0, 0)
    consts = [wts[n] for n in RW_PARAM_NAMES]
    tri = (jnp.arange(chunk)[:, None] >= jnp.arange(chunk)[None, :]).astype(BF16)
    hd = jnp.arange(GW) // HS
    ones_bd = (hd[:, None] == hd[None, :]).astype(BF16)
    m0 = _state_to_m(s0)
    kern = functools.partial(_rwkv_kernel, n_valid=n_valid)
    o, m = pl.pallas_call(
        kern,
        grid=(batch, nc),
        in_specs=[pl.BlockSpec((chunk, RW_COLS), lambda b, c: (b * nc + c, 0)),
                  pl.BlockSpec((1, 1, RW_COLS), lambda b, c: (0, 0, 0)),
                  pl.BlockSpec((1, HEADS // GROUP, HS, GW), lambda b, c: (0, 0, 0, 0))]
        + [pl.BlockSpec(a.shape, const2) for a in consts]
        + [pl.BlockSpec(tri.shape, const2), pl.BlockSpec(ones_bd.shape, const2)],
        out_specs=[pl.BlockSpec((chunk, RW_W), lambda b, c: (b * nc + c, 0)),
                   pl.BlockSpec((1, HEADS // GROUP, HS, GW), lambda b, c: (b, 0, 0, 0))],
        out_shape=[jax.ShapeDtypeStruct((rows, RW_W), BF16),
                   jax.ShapeDtypeStruct((batch, HEADS // GROUP, HS, GW), F32)],
        scratch_shapes=[pltpu.VMEM((1, RW_COLS), F32), pltpu.VMEM((HEADS // GROUP, HS, GW), F32)],
        compiler_params=_params(("parallel", "arbitrary")),
        name="rwkv",
    )(rw, shift0, m0, *consts, tri, ones_bd)
    return o, _m_to_state(m)


def _rwkv_step_kernel(rw_ref, shift_ref, s_ref, *rest):
    prm_refs = rest[:len(RW_PARAM_NAMES)]
    o_ref, sout_ref, t_scr, y_scr, tok_scr = rest[len(RW_PARAM_NAMES):]
    h = pl.program_id(0)

    @pl.when(h == 0)
    def _():
        p = {n: ref[...] for n, ref in zip(RW_PARAM_NAMES, prm_refs)}
        r, k_mod, v, logw, a_sig, g, kk, bonus = _rwkv_token_prep(rw_ref[...], shift_ref[...], p)
        t_scr[0] = r.T
        t_scr[1] = jnp.exp(logw).T
        t_scr[2] = k_mod.T
        t_scr[3] = v.T
        t_scr[4] = (-kk).T
        t_scr[5] = (kk * a_sig).T
        tok_scr[0] = bonus
        tok_scr[1] = g

    rows = pl.ds(pl.multiple_of(h * HS, HS), HS)
    r_h, w_h, k_h = t_scr[0, rows, :], t_scr[1, rows, :], t_scr[2, rows, :]
    a_h, b_h = t_scr[4, rows, :], t_scr[5, rows, :]

    def body(vv, carry):
        sv = s_ref[0, vv]
        vrow = t_scr[3, pl.ds(h * HS + vv, 1), :]
        sa = jnp.sum(sv * a_h, axis=0, keepdims=True)
        sn = sv * w_h + sa * b_h + vrow * k_h
        sout_ref[0, vv] = sn
        y_scr[pl.ds(h * HS + vv, 1), :] = jnp.sum(sn * r_h, axis=0, keepdims=True)
        return carry
    lax.fori_loop(0, HS, body, 0)

    @pl.when(h == HEADS - 1)
    def _():
        p = {n: ref[...] for n, ref in zip(RW_PARAM_NAMES, prm_refs)}
        o_ref[...] = _group_norm_out(y_scr[...].T, tok_scr[0], tok_scr[1], p).astype(o_ref.dtype)


def _rwkv_step(rw, shift, s_t, wts):
    nb = rw.shape[0]
    consts = [wts[n] for n in RW_PARAM_NAMES]
    const2 = lambda h: (0, 0)
    return pl.pallas_call(
        _rwkv_step_kernel,
        grid=(HEADS,),
        in_specs=[pl.BlockSpec((nb, RW_COLS), const2), pl.BlockSpec((nb, RW_COLS), const2),
                  pl.BlockSpec((1, HS, HS, nb), lambda h: (h, 0, 0, 0))]
        + [pl.BlockSpec(a.shape, const2) for a in consts],
        out_specs=[pl.BlockSpec((nb, RW_W), const2), pl.BlockSpec((1, HS, HS, nb), lambda h: (h, 0, 0, 0))],
        out_shape=[jax.ShapeDtypeStruct((nb, RW_W), BF16), jax.ShapeDtypeStruct(s_t.shape, F32)],
        scratch_shapes=[pltpu.VMEM((6, RW_W, nb), F32), pltpu.VMEM((RW_W, nb), F32), pltpu.VMEM((2, nb, RW_W), F32)],
        compiler_params=_params(("arbitrary",)),
        name="rwkv_step",
    )(rw, shift, s_t, *consts)


def _merge_kernel(x_ref, om_ref, or_ref, gs_ref, wom_ref, wor_ref, wout_ref, gffn_ref, x1_ref, h2_ref):
    a = _dot(om_ref[...], wom_ref[...])
    r = _dot(or_ref[...], wor_ref[...])
    mix = gs_ref[:, 0:D_MODEL] * a + gs_ref[:, D_MODEL:2 * D_MODEL] * r
    x1 = x_ref[...] + _dot(mix.astype(BF16), wout_ref[...])
    x1_ref[...] = x1
    h2_ref[...] = _rms(x1, gffn_ref[...]).astype(BF16)


def _merge(x, om, orw, gsig, wts, tm):
    rows = x.shape[0]
    row = lambda i: (i, 0)
    const = lambda i: (0, 0)
    full = lambda a: pl.BlockSpec(a.shape, const)
    return pl.pallas_call(
        _merge_kernel,
        grid=(rows // tm,),
        in_specs=[pl.BlockSpec((tm, D_MODEL), row), pl.BlockSpec((tm, HEADS * VDIM), row),
                  pl.BlockSpec((tm, RW_W), row), pl.BlockSpec((tm, 2 * D_MODEL), row),
                  full(wts['w_o_mla']), full(wts['w_o_rwkv']), full(wts['w_out']), full(wts['g_ffn'])],
        out_specs=[pl.BlockSpec((tm, D_MODEL), row), pl.BlockSpec((tm, D_MODEL), row)],
        out_shape=[jax.ShapeDtypeStruct((rows, D_MODEL), F32), jax.ShapeDtypeStruct((rows, D_MODEL), BF16)],
        compiler_params=_params(("parallel",)),
        name="merge",
    )(x, om, orw, gsig, wts['w_o_mla'], wts['w_o_rwkv'], wts['w_out'], wts['g_ffn'])


def _ffn_kernel(x1_ref, h2_ref, wup_ref, wdn_ref, gfin_ref, y_ref):
    u = jnp.maximum(_dot(h2_ref[...], wup_ref[...]), 0.0)
    x2 = x1_ref[...] + _dot((u * u).astype(BF16), wdn_ref[...])
    y_ref[...] = _rms(x2, gfin_ref[...])


def _ffn(x1, h2, wts, tm):
    rows = x1.shape[0]
    row = lambda i: (i, 0)
    const = lambda i: (0, 0)
    full = lambda a: pl.BlockSpec(a.shape, const)
    return pl.pallas_call(
        _ffn_kernel,
        grid=(rows // tm,),
        in_specs=[pl.BlockSpec((tm, D_MODEL), row), pl.BlockSpec((tm, D_MODEL), row),
                  full(wts['w_up']), full(wts['w_down']), full(wts['g_final'])],
        out_specs=pl.BlockSpec((tm, D_MODEL), row),
        out_shape=jax.ShapeDtypeStruct((rows, D_MODEL), F32),
        compiler_params=_params(("parallel",)),
        name="ffn",
    )(x1, h2, wts['w_up'], wts['w_down'], wts['g_final'])


def _swap_halves(w):
    return jnp.concatenate([w[..., ROPE // 2:], w[..., :ROPE // 2]], axis=-1)


def _prep_weights(g_final, g_mix, w_in, g_q, w_uq, g_kv, w_uk, w_uv, w_o_mla, mu_shift, w0, w2, a0, a2, g2,
                  k_k, k_a, r_k, ln_w, ln_b, w_o_rwkv, w_out, g_ffn, w_up, w_down):
    row = lambda a: a.reshape(1, -1).astype(F32)
    q_in = w_in[:, 0:Q_LORA]
    kv_in = w_in[:, Q_LORA:Q_LORA + KV_LORA]
    kr = w_in[:, Q_LORA + KV_LORA:Q_LORA + KV_LORA + ROPE]
    o_rw = Q_LORA + KV_LORA + ROPE
    rw = w_in[:, o_rw:o_rw + RW_COLS]
    gate = w_in[:, o_rw + RW_COLS:]
    zr = jnp.zeros((D_MODEL, ROPE), F32)
    kr_a = jnp.concatenate([kr, zr, kr, zr], axis=1)
    krs = _swap_halves(kr)
    kr_b = jnp.concatenate([krs, zr, krs, zr], axis=1)
    w_in_p = jnp.concatenate([gate, rw, q_in, kv_in, kr_a, kr_b], axis=1).astype(BF16)

    uq = w_uq.reshape(Q_LORA, HEADS, NOPE + ROPE)
    zq = jnp.zeros((Q_LORA, HEADS, HP - NOPE - ROPE), F32)
    uq_pad = jnp.concatenate([uq, zq], axis=-1).reshape(Q_LORA, HEADS * HP)
    uq_sw = jnp.concatenate([jnp.zeros((Q_LORA, HEADS, NOPE), F32), _swap_halves(uq[..., NOPE:]), zq], axis=-1)
    w_uq_p = jnp.concatenate([uq_pad, uq_sw.reshape(Q_LORA, HEADS * HP)], axis=1).astype(BF16)

    uk = w_uk.reshape(KV_LORA, HEADS, NOPE)
    uk_pad = jnp.concatenate([uk, jnp.zeros((KV_LORA, HEADS, HP - NOPE), F32)], axis=-1).reshape(KV_LORA, HEADS * HP)
    w_ukv = uk_pad.astype(BF16)

    head = jnp.arange(RW_W) // HS
    seg = (head[:, None] == head[None, :]).astype(BF16)
    return dict(
        g_mix=row(g_mix), w_in=w_in_p, g_q=row(g_q), w_uq=w_uq_p, g_kv=row(g_kv), w_ukv=w_ukv,
        w_uk=w_uk.astype(BF16), w_uv=w_uv.astype(BF16), w_uvt=w_uv.T.astype(BF16), w_o_mla=w_o_mla.astype(BF16),
        mu_shift=row(mu_shift), w0=row(w0), w2=w2.astype(BF16), a0=row(a0), a2=a2.astype(BF16),
        g2=g2.astype(BF16), k_k=row(k_k), k_a=row(k_a), r_k=row(r_k), ln_w=row(ln_w), ln_b=row(ln_b), seg=seg,
        w_o_rwkv=w_o_rwkv.astype(BF16), w_out=w_out.astype(BF16), g_ffn=row(g_ffn),
        w_up=w_up.astype(BF16), w_down=w_down.astype(BF16), g_final=row(g_final))


def _rope_tables(pos):
    inv = ROPE_BASE ** (-jnp.arange(0, ROPE, 2, dtype=F32) / ROPE)
    ang = pos.astype(F32)[:, None] * inv[None, :]
    cos, sin = jnp.cos(ang), jnp.sin(ang)
    n = pos.shape[0]
    cc = jnp.concatenate([cos, cos], axis=1)
    ss = jnp.concatenate([-sin, sin], axis=1)
    z32 = jnp.zeros((n, ROPE), F32)
    ctq = jnp.concatenate([jnp.ones((n, NOPE), F32), cc, z32], axis=1)
    stq = jnp.concatenate([jnp.zeros((n, NOPE), F32), ss, z32], axis=1)
    ctk = jnp.concatenate([cc, z32, cc, z32], axis=1)
    stk = jnp.concatenate([ss, z32, ss, z32], axis=1)
    return ctq, stq, ctk, stk


def _row_tile(rows, cap):
    tm = min(rows, cap)
    assert rows % tm == 0
    return tm


def kernel(x_prompt, x_sample, cache_kv, page_table, state_wkv, state_shift, meta_tokens, g_final, g_mix, w_in, g_q, w_uq, g_kv, w_uk, w_uv, w_o_mla, mu_shift, w0, w2, a0, a2, g2, k_k, k_a, r_k, ln_w, ln_b, w_o_rwkv, w_out, g_ffn, w_up, w_down):
    depth = g_mix.shape[0]
    assert depth == 1
    bp, seq, _ = x_prompt.shape
    bd, s_s, _ = x_sample.shape
    assert s_s == 1
    n_pages = page_table.shape[1]
    past_len = n_pages * PAGE
    wts = _prep_weights(g_final, g_mix[0], w_in[0], g_q[0], w_uq[0], g_kv[0], w_uk[0], w_uv[0], w_o_mla[0],
                        mu_shift[0], w0[0], w2[0], a0[0], a2[0], g2[0], k_k[0], k_a[0], r_k[0], ln_w[0], ln_b[0],
                        w_o_rwkv[0], w_out[0], g_ffn[0], w_up[0], w_down[0])

    tab_m = _rope_tables(jnp.arange(N_META))
    _, rw_m, _, kvrow_m, kpad_m, vt_m = _proj(meta_tokens.astype(F32), tab_m, wts, N_META, BF16)
    rw_m_pad = jnp.concatenate([rw_m, jnp.zeros((HS - N_META, RW_COLS), F32)], axis=0)
    zero_shift = jnp.zeros((1, 1, RW_COLS), F32)
    zero_state = jnp.zeros((1, HEADS, HS, HS), F32)
    _, s_meta = _rwkv(rw_m_pad, zero_shift, zero_state, wts, 1, N_META)
    shift_meta = rw_m[N_META - 1:N_META].reshape(1, 1, RW_COLS)

    rows_p = bp * seq
    tm = _row_tile(seq, 256)
    xp = x_prompt.reshape(rows_p, D_MODEL)
    tab_p = _rope_tables(N_META + jnp.arange(seq))
    gsig, rw, q, kvrow, kpad, vt = _proj(xp, tab_p, wts, tm, BF16)
    o_mla = _attn(q, kpad, vt, kpad_m, vt_m, bp, seq, tm)
    o_rwkv, s_p = _rwkv(rw, shift_meta, s_meta, wts, bp, None)
    x1, h2 = _merge(xp, o_mla, o_rwkv, gsig, wts, tm)
    y_prompt = _ffn(x1, h2, wts, tm).reshape(bp, seq, D_MODEL)
    kv_prompt = jnp.concatenate([jnp.broadcast_to(kvrow_m[None], (bp, N_META, KV_W)),
                                 kvrow.reshape(bp, seq, KV_W)], axis=1)[None]
    shift_prompt = rw.reshape(bp, seq, RW_COLS)[:, -1][None]

    xs = x_sample.reshape(bd, D_MODEL)
    tab_s = _rope_tables(jnp.full((bd,), past_len, jnp.int32))
    gsig_s, rw_s, q_s, kvrow_s, _, _ = _proj(xs, tab_s, wts, bd, F32)
    qf = jnp.transpose(_qlat(q_s, wts['w_uk']), (1, 0, 2))
    cache = jnp.swapaxes(cache_kv.reshape(cache_kv.shape[1], PAGE, KV_W), 1, 2)
    o_mla_s = _sattn(page_table, qf, kvrow_s.reshape(bd, 1, KV_W), wts['w_uv'], cache).reshape(bd, HEADS * VDIM)
    s_t = jnp.transpose(state_wkv[0], (1, 2, 3, 0))
    o_rwkv_s, s_t_new = _rwkv_step(rw_s, state_shift[0], s_t, wts)
    s_s_new = jnp.transpose(s_t_new, (3, 0, 1, 2))
    x1_s, h2_s = _merge(xs, o_mla_s, o_rwkv_s, gsig_s, wts, bd)
    y_sample = _ffn(x1_s, h2_s, wts, bd).reshape(bd, 1, D_MODEL)

    return (y_prompt, y_sample,
            kv_prompt.astype(cache_kv.dtype), s_p[None].astype(state_wkv.dtype),
            shift_prompt.astype(state_shift.dtype),
            kvrow_s.reshape(1, bd, 1, KV_W).astype(cache_kv.dtype), s_s_new[None].astype(state_wkv.dtype),
            rw_s[None].astype(state_shift.dtype))
```

```python
import functools

import jax
import jax.numpy as jnp
from jax import lax
from jax.experimental import pallas as pl
from jax.experimental.pallas import tpu as pltpu

F32 = jnp.float32
BF16 = jnp.bfloat16

D_MODEL = 1024
N_META = 16
PAGE = 128
HEADS = 8
NOPE = 64
ROPE = 32
VDIM = 64
Q_LORA = 384
KV_LORA = 256
KV_W = KV_LORA + ROPE
ROPE_BASE = 10000.0
SM_SCALE = (NOPE + ROPE) ** -0.5
HS = 64
RW_W = HEADS * HS
W_LORA = 64
A_LORA = 64
G_LORA = 128
RW_COLS = 3 * RW_W + W_LORA + A_LORA + G_LORA
GN_EPS = 64e-5
D_FF = 4 * D_MODEL
NORM_EPS = 1e-6
NEG_INF = -1e30
DECAY_SCALE = 0.6065306597126334

LANE = 128
HP = 128
VMEM_LIMIT = 56 * 1024 * 1024

C_GATE = 0
C_RW = 2 * D_MODEL
C_QIN = C_RW + RW_COLS
C_KVIN = C_QIN + Q_LORA
C_KR = C_KVIN + KV_LORA
IN_PERM = C_KR + 2 * LANE


def _dot(a, b):
    return jnp.dot(a, b, preferred_element_type=F32)


def _dot_nt(a, b):
    return lax.dot_general(a, b, (((1,), (1,)), ((), ())), preferred_element_type=F32)


def _dot_tn(a, b):
    return lax.dot_general(a, b, (((0,), (0,)), ((), ())), preferred_element_type=F32)


def _split2(x):
    hi = x.astype(BF16)
    lo = (x - hi.astype(F32)).astype(BF16)
    return hi, lo


def _split3(x):
    x1 = x.astype(BF16)
    r1 = x - x1.astype(F32)
    x2 = r1.astype(BF16)
    x3 = (r1 - x2.astype(F32)).astype(BF16)
    return x1, x2, x3


def _rms(x, g):
    return x * lax.rsqrt(jnp.mean(x * x, axis=-1, keepdims=True) + NORM_EPS) * g


def _params(sem):
    return pltpu.CompilerParams(dimension_semantics=sem, vmem_limit_bytes=VMEM_LIMIT)


def _proj_kernel(x_ref, gmix_ref, win_ref, gq_ref, wuq_ref, gkv_ref, wukv_ref, wuvt_ref,
                 ctq_ref, stq_ref, ctk_ref, stk_ref,
                 gsig_ref, rw_ref, q_ref, kvrow_ref, kpad_ref, vt_ref):
    h = _rms(x_ref[...], gmix_ref[...]).astype(BF16)
    step = 512
    for c in range(0, 2 * D_MODEL, step):
        gsig_ref[:, c:c + step] = jax.nn.sigmoid(_dot(h, win_ref[:, C_GATE + c:C_GATE + c + step]))
    for c in range(0, RW_COLS, 256):
        rw_ref[:, c:c + 256] = _dot(h, win_ref[:, C_RW + c:C_RW + c + 256])

    qn = _rms(_dot(h, win_ref[:, C_QIN:C_QIN + Q_LORA]), gq_ref[...]).astype(BF16)
    ctq = ctq_ref[...]
    stq = stq_ref[...]
    for hh in range(HEADS):
        a = _dot(qn, wuq_ref[:, hh * HP:(hh + 1) * HP])
        b = _dot(qn, wuq_ref[:, HEADS * HP + hh * HP:HEADS * HP + (hh + 1) * HP])
        q_ref[:, hh * HP:(hh + 1) * HP] = (a * ctq + b * stq).astype(q_ref.dtype)

    ckv = _rms(_dot(h, win_ref[:, C_KVIN:C_KVIN + KV_LORA]), gkv_ref[...])
    kvrow_ref[:, 0:KV_LORA] = ckv
    zkr = _dot(h, win_ref[:, C_KR:C_KR + 2 * LANE])
    kr = zkr[:, 0:LANE] * ctk_ref[...] + zkr[:, LANE:2 * LANE] * stk_ref[...]
    kvrow_ref[:, KV_LORA:KV_W] = kr[:, 0:ROPE]
    lane = lax.broadcasted_iota(jnp.int32, kr.shape, 1)
    kr_head = jnp.where((lane >= NOPE) & (lane < NOPE + ROPE), kr, 0.0)
    cb = ckv.astype(BF16)
    for hh in range(HEADS):
        kpad_ref[:, hh * HP:(hh + 1) * HP] = (_dot(cb, wukv_ref[:, hh * HP:(hh + 1) * HP]) + kr_head).astype(BF16)
    vt_ref[0] = _dot_nt(wuvt_ref[...], cb).astype(BF16)


def _proj(x, tables, wts, tm, q_dtype):
    rows = x.shape[0]
    nt = tables[0].shape[0] // tm
    row = lambda i: (i, 0)
    const = lambda i: (0, 0)
    tab = lambda i: (i % nt, 0)
    full = lambda a: pl.BlockSpec(a.shape, const)
    in_specs = [pl.BlockSpec((tm, D_MODEL), row), full(wts['g_mix']), full(wts['w_in']), full(wts['g_q']),
                full(wts['w_uq']), full(wts['g_kv']), full(wts['w_ukv']), full(wts['w_uvt'])]
    in_specs += [pl.BlockSpec((tm, LANE), tab)] * 4
    widths = [(2 * D_MODEL, F32), (RW_COLS, F32), (HEADS * HP, q_dtype), (KV_W, F32), (HEADS * HP, BF16)]
    nt_rows = rows // tm
    return pl.pallas_call(
        _proj_kernel,
        grid=(nt_rows,),
        in_specs=in_specs,
        out_specs=[pl.BlockSpec((tm, w), row) for w, _ in widths]
        + [pl.BlockSpec((1, HEADS * VDIM, tm), lambda i: (i, 0, 0))],
        out_shape=[jax.ShapeDtypeStruct((rows, w), dt) for w, dt in widths]
        + [jax.ShapeDtypeStruct((nt_rows, HEADS * VDIM, tm), BF16)],
        compiler_params=_params(("parallel",)),
        name="proj",
    )(x, wts['g_mix'], wts['w_in'], wts['g_q'], wts['w_uq'], wts['g_kv'], wts['w_ukv'], wts['w_uvt'], *tables)


def _attn_kernel(q_ref, k_ref, vt_ref, km_ref, vmt_ref, eye_ref, o_ref, *, tq):
    i = pl.program_id(1)
    qs = [q_ref[:, hh * HP:(hh + 1) * HP] for hh in range(HEADS)]

    def update(carry, k_tiles, vt_tiles, mask):
        ss = [_dot_nt(k_tiles[hh], qs[hh]) for hh in range(HEADS)]
        ps, stats = [], []
        for hh in range(HEADS):
            m, l, _ = carry[hh]
            s = ss[hh] * SM_SCALE
            if mask is not None:
                s = jnp.where(mask, s, NEG_INF)
            m_new = jnp.maximum(m, jnp.max(s, axis=0, keepdims=True))
            alpha = jnp.exp(m - m_new)
            p = jnp.exp(s - m_new)
            stats.append((m_new, l * alpha + jnp.sum(p, axis=0, keepdims=True), alpha))
            ps.append(p.astype(BF16))
        pv = [_dot(vt_tiles[hh], ps[hh]) for hh in range(HEADS)]
        return tuple((stats[hh][0], stats[hh][1], carry[hh][2] * stats[hh][2] + pv[hh]) for hh in range(HEADS))

    init = (jnp.full((1, tq), NEG_INF, F32), jnp.zeros((1, tq), F32), jnp.zeros((VDIM, tq), F32))
    carry = update((init,) * HEADS, [km_ref[:, hh * HP:(hh + 1) * HP] for hh in range(HEADS)],
                   [vmt_ref[0, hh * VDIM:(hh + 1) * VDIM, :] for hh in range(HEADS)], None)

    def tile(j, carry, mask):
        rows = pl.ds(pl.multiple_of(j * tq, tq), tq)
        return update(carry, [k_ref[rows, hh * HP:(hh + 1) * HP] for hh in range(HEADS)],
                      [vt_ref[j, hh * VDIM:(hh + 1) * VDIM, :] for hh in range(HEADS)], mask)

    carry = lax.fori_loop(0, i, lambda j, c: tile(j, c, None), carry)
    key = lax.broadcasted_iota(jnp.int32, (tq, tq), 0)
    qry = lax.broadcasted_iota(jnp.int32, (tq, tq), 1)
    carry = tile(i, carry, key <= qry)

    outs = []
    for hh in range(HEADS):
        m, l, acc = carry[hh]
        outs.append(_dot_tn((acc / l).astype(BF16), eye_ref[...]))
    o_ref[...] = jnp.concatenate(outs, axis=-1).astype(o_ref.dtype)


def _attn(q, kpad, vt, km, vmt, batch, seq, tq):
    nq = seq // tq
    eye = jnp.eye(VDIM, dtype=BF16)
    kern = functools.partial(_attn_kernel, tq=tq)
    return pl.pallas_call(
        kern,
        grid=(batch, nq),
        in_specs=[
            pl.BlockSpec((tq, HEADS * HP), lambda b, i: (b * nq + i, 0)),
            pl.BlockSpec((seq, HEADS * HP), lambda b, i: (b, 0)),
            pl.BlockSpec((nq, HEADS * VDIM, tq), lambda b, i: (b, 0, 0)),
            pl.BlockSpec(km.shape, lambda b, i: (0, 0)),
            pl.BlockSpec(vmt.shape, lambda b, i: (0, 0, 0)),
            pl.BlockSpec(eye.shape, lambda b, i: (0, 0)),
        ],
        out_specs=pl.BlockSpec((tq, HEADS * VDIM), lambda b, i: (b * nq + i, 0)),
        out_shape=jax.ShapeDtypeStruct((batch * seq, HEADS * VDIM), BF16),
        compiler_params=_params(("parallel", "arbitrary")),
        name="attn",
    )(q, kpad, vt, km, vmt, eye)


def _qlat_kernel(q_ref, wuk_ref, o_ref):
    for hh in range(HEADS):
        qn = q_ref[:, hh * HP:hh * HP + NOPE].astype(BF16)
        o_ref[hh, :, 0:KV_LORA] = _dot_nt(qn, wuk_ref[:, hh * NOPE:(hh + 1) * NOPE]) * SM_SCALE
        o_ref[hh, :, KV_LORA:KV_W] = q_ref[:, hh * HP + NOPE:hh * HP + NOPE + ROPE] * SM_SCALE


def _qlat(q, wuk):
    rows = q.shape[0]
    return pl.pallas_call(
        _qlat_kernel,
        out_shape=jax.ShapeDtypeStruct((HEADS, rows, KV_W), F32),
        name="qlat",
    )(q, wuk)


def _sattn_kernel(pt_ref, qf_ref, kvn_ref, wuv_ref, cache_ref, o_ref, buf, sem, *, n_pages, chunk):
    b = pl.program_id(0)
    nb = pl.num_programs(0)

    def page_copy(bb, slot, j):
        return pltpu.make_async_copy(cache_ref.at[pt_ref[bb, j]], buf.at[slot, :, pl.ds(j * PAGE, PAGE)], sem.at[slot])

    def fetch(bb, slot):
        def body(j, c):
            page_copy(bb, slot, j).start()
            return c
        lax.fori_loop(0, n_pages, body, 0)

    @pl.when(b == 0)
    def _():
        fetch(0, 0)

    @pl.when(b + 1 < nb)
    def _():
        fetch(b + 1, (b + 1) % 2)

    slot = b % 2

    def wait_body(j, c):
        page_copy(b, slot, j).wait()
        return c
    lax.fori_loop(0, n_pages, wait_body, 0)

    qf = qf_ref[0].astype(BF16)
    m = jnp.full((HEADS, 1), NEG_INF, F32)
    l = jnp.zeros((HEADS, 1), F32)
    acc = jnp.zeros((HEADS, KV_LORA), F32)
    for c0 in range(0, n_pages * PAGE, chunk):
        kt = buf[slot, :, c0:c0 + chunk].astype(BF16)
        s = _dot(qf, kt)
        m_new = jnp.maximum(m, jnp.max(s, axis=-1, keepdims=True))
        alpha = jnp.exp(m - m_new)
        p = jnp.exp(s - m_new)
        l = l * alpha + jnp.sum(p, axis=-1, keepdims=True)
        acc = acc * alpha + _dot_nt(p.astype(BF16), kt[0:KV_LORA, :])
        m = m_new
    kvn = kvn_ref[0].astype(BF16).astype(F32)
    s = jnp.sum(qf.astype(F32) * kvn, axis=-1, keepdims=True)
    m_new = jnp.maximum(m, s)
    alpha = jnp.exp(m - m_new)
    p = jnp.exp(s - m_new)
    l = l * alpha + p
    acc = acc * alpha + p.astype(BF16).astype(F32) * kvn[:, 0:KV_LORA]
    o_lat = (acc / l).astype(BF16)
    full = _dot(o_lat, wuv_ref[...])
    r = lax.broadcasted_iota(jnp.int32, full.shape, 0)
    cidx = lax.broadcasted_iota(jnp.int32, full.shape, 1)
    o_ref[0] = jnp.sum(jnp.where(cidx // VDIM == r, full, 0.0), axis=0, keepdims=True).astype(o_ref.dtype)


def _sattn(page_table, qf, kvn, wuv, cache):
    nb, n_pages = page_table.shape
    chunk = min(2048, n_pages * PAGE)
    kern = functools.partial(_sattn_kernel, n_pages=n_pages, chunk=chunk)
    grid_spec = pltpu.PrefetchScalarGridSpec(
        num_scalar_prefetch=1,
        grid=(nb,),
        in_specs=[
            pl.BlockSpec((1, HEADS, KV_W), lambda b, pt: (b, 0, 0)),
            pl.BlockSpec((1, 1, KV_W), lambda b, pt: (b, 0, 0)),
            pl.BlockSpec(wuv.shape, lambda b, pt: (0, 0)),
            pl.BlockSpec(memory_space=pl.ANY),
        ],
        out_specs=pl.BlockSpec((1, 1, HEADS * VDIM), lambda b, pt: (b, 0, 0)),
        scratch_shapes=[pltpu.VMEM((2, KV_W, n_pages * PAGE), F32), pltpu.SemaphoreType.DMA((2,))],
    )
    return pl.pallas_call(
        kern,
        grid_spec=grid_spec,
        out_shape=jax.ShapeDtypeStruct((nb, 1, HEADS * VDIM), BF16),
        compiler_params=_params(("arbitrary",)),
        name="sattn",
    )(page_table, qf, kvn, wuv, cache)


GROUP = 4
RWKV_BATCHES_PER_STEP = 4
GW = GROUP * HS
RW_PARAM_NAMES = ['mu_shift', 'w0', 'w2', 'a0', 'a2', 'g2', 'k_k', 'k_a', 'r_k', 'ln_w', 'ln_b', 'seg']


def _seg_sum(x, seg):
    hi, lo = _split2(x)
    return _dot(hi, seg) + _dot(lo, seg)


def _rwkv_token_prep(cur, prev, p):
    z = cur + (prev - cur) * p['mu_shift']
    r = z[:, 0:RW_W]
    k = z[:, RW_W:2 * RW_W]
    v = z[:, 2 * RW_W:3 * RW_W]
    w_in = z[:, 3 * RW_W:3 * RW_W + W_LORA]
    a_in = z[:, 3 * RW_W + W_LORA:3 * RW_W + W_LORA + A_LORA]
    g_in = z[:, 3 * RW_W + W_LORA + A_LORA:RW_COLS]
    w_log = p['w0'] + _dot(jnp.tanh(w_in).astype(BF16), p['w2'])
    logw = -DECAY_SCALE * jax.nn.sigmoid(w_log)
    a_sig = jax.nn.sigmoid(p['a0'] + _dot(a_in.astype(BF16), p['a2']))
    g = _dot(jax.nn.sigmoid(g_in).astype(BF16), p['g2'])
    kk = k * p['k_k']
    kk = kk / jnp.maximum(jnp.sqrt(_seg_sum(kk * kk, p['seg'])), 1e-12)
    k_mod = k * (1.0 + (a_sig - 1.0) * p['k_a'])
    bonus = _seg_sum(r * k_mod * p['r_k'], p['seg']) * v
    return r, k_mod, v, logw, a_sig, g, kk, bonus


def _group_norm_out(y, bonus, g, p):
    mean = _seg_sum(y, p['seg']) * (1.0 / HS)
    d = y - mean
    var = _seg_sum(d * d, p['seg']) * (1.0 / HS)
    yn = d * lax.rsqrt(var + GN_EPS) * p['ln_w'] + p['ln_b']
    return (yn + bonus) * g


def _rwkv_kernel(rw_ref, shift_ref, m0_ref, *rest, n_valid, nbat):
    prm_refs = rest[:len(RW_PARAM_NAMES)]
    tri_ref, ones_ref, o_ref, mout_ref, prev_scr, m_scr = rest[len(RW_PARAM_NAMES):]
    p = {n: ref[...] for n, ref in zip(RW_PARAM_NAMES, prm_refs)}
    chunk = HS
    ngrp = HEADS // GROUP
    c = pl.program_id(1)
    nc = pl.num_programs(1)

    @pl.when(c == 0)
    def _():
        for bi in range(nbat):
            prev_scr[bi] = shift_ref[0]
            m_scr[bi] = m0_ref[0]

    curs, prevs = [], []
    for bi in range(nbat):
        cur_b = rw_ref[bi]
        rowi = lax.broadcasted_iota(jnp.int32, cur_b.shape, 0)
        prevs.append(jnp.where(rowi == 0, prev_scr[bi], pltpu.roll(cur_b, 1, 0)))
        prev_scr[bi] = cur_b[chunk - 1:chunk, :]
        curs.append(cur_b)
    r, k_mod, v, logw, a_sig, g, kk, bonus = _rwkv_token_prep(
        jnp.concatenate(curs, axis=0), jnp.concatenate(prevs, axis=0), p)

    if n_valid is not None:
        step = lax.broadcasted_iota(jnp.int32, (nbat * chunk, RW_W), 0) % chunk
        live = (c * chunk + step) < n_valid
        logw = jnp.where(live, logw, 0.0)
        kk = jnp.where(live, kk, 0.0)
        k_mod = jnp.where(live, k_mod, 0.0)
        v = jnp.where(live, v, 0.0)

    lw3 = _split3(logw)
    seq_rows = [slice(bi * chunk, (bi + 1) * chunk) for bi in range(nbat)]
    gcum = jnp.concatenate([sum(_dot(tri_ref[...], t[rs]) for t in lw3) for rs in seq_rows], axis=0)
    g_end = jnp.concatenate([jnp.broadcast_to(gcum[rs][chunk - 1:chunk, :], (chunk, RW_W)) for rs in seq_rows], axis=0)
    r_t = r * jnp.exp(gcum)
    a_t = -kk * jnp.exp(gcum - logw)
    eng = jnp.exp(-gcum)
    b_t = kk * a_sig * eng
    k_t = k_mod * eng
    tail = jnp.exp(g_end - gcum)
    b_e = kk * a_sig * tail
    k_e = k_mod * tail
    decay_end = jnp.exp(g_end)

    ti = lax.broadcasted_iota(jnp.int32, (chunk, GW), 0)
    lane = lax.broadcasted_iota(jnp.int32, (chunk, GW), 1)
    si = lane % HS
    lane_head = lane // HS
    strict = si < ti
    incl = si <= ti
    eye = (si == ti).astype(F32)

    def bdr(zf):
        return jnp.concatenate([jnp.where(lane_head == hh, zf, 0.0) for hh in range(GROUP)], axis=0).astype(BF16)

    streams = [(bi, gi) for bi in range(nbat) for gi in range(ngrp)]
    blk = lambda t, s: t[s[0] * chunk:(s[0] + 1) * chunk, s[1] * GW:(s[1] + 1) * GW]
    cat0 = lambda xs: jnp.concatenate(xs, axis=0).astype(BF16)
    cat1 = lambda xs: jnp.concatenate(xs, axis=1).astype(BF16)

    ar = [cat0([blk(a_t, s), blk(r_t, s)]) for s in streams]
    ab = [_dot_nt(ar[i], bdr(blk(b_t, s))) for i, s in enumerate(streams)]
    ak = [_dot_nt(ar[i], bdr(blk(k_t, s))) for i, s in enumerate(streams)]
    n = [jnp.where(strict, t[0:chunk], 0.0) for t in ab]
    a_rb = [jnp.where(incl, t[chunk:2 * chunk], 0.0) for t in ab]
    a_ak = [jnp.where(strict, t[0:chunk], 0.0) for t in ak]
    a_rk = [jnp.where(incl, t[chunk:2 * chunk], 0.0) for t in ak]

    x = [eye + t for t in n]
    pw = [_dot(t.astype(BF16), bdr(t)) for t in n]
    for _ in range(4):
        z = [_dot(cat0([xi, pi]), bdr(pi)) for xi, pi in zip(x, pw)]
        x = [xi + zi[0:chunk] for xi, zi in zip(x, z)]
        pw = [zi[chunk:2 * chunk] for zi in z]
    z = [_dot(xi.astype(BF16), bdr(pi)) for xi, pi in zip(x, pw)]
    x = [xi + zi for xi, zi in zip(x, z)]

    m0 = [m_scr[s[0], s[1]] for s in streams]
    bm = [bdr(t) for t in m0]
    bv = [bdr(blk(v, s)) for s in streams]
    rhs = [_dot(cat1([blk(a_t, s), a_ak[i]]), jnp.concatenate([bm[i], bv[i]], axis=0)) for i, s in enumerate(streams)]
    u = [_dot(xi.astype(BF16), bdr(ri)) for xi, ri in zip(x, rhs)]
    y = [_dot(cat1([blk(r_t, s), a_rb[i], a_rk[i]]), jnp.concatenate([bm[i], bdr(u[i]), bv[i]], axis=0))
         for i, s in enumerate(streams)]
    full = [_dot_tn(cat0([blk(b_e, s), blk(k_e, s)]), cat0([u[i], blk(v, s)])) for i, s in enumerate(streams)]
    gm = [sum(_dot(t, ones_ref[...]) for t in _split3(eye * blk(decay_end, s)[0:1])) for s in streams]
    for i, s in enumerate(streams):
        upd = jnp.where(lane_head == 0, full[i][0:HS], 0.0)
        for hh in range(1, GROUP):
            upd = upd + jnp.where(lane_head == hh, full[i][hh * HS:(hh + 1) * HS], 0.0)
        m_scr[s[0], s[1]] = gm[i] * m0[i] + upd

    y_all = jnp.concatenate([jnp.concatenate([y[bi * ngrp + gi] for gi in range(ngrp)], axis=1)
                             for bi in range(nbat)], axis=0)
    out = _group_norm_out(y_all, bonus, g, p).astype(o_ref.dtype)
    for bi in range(nbat):
        o_ref[bi] = out[seq_rows[bi]]

    @pl.when(c == nc - 1)
    def _():
        mout_ref[...] = m_scr[...]


def _state_to_m(s):
    b = s.shape[0]
    return jnp.transpose(s.reshape(b, HEADS // GROUP, GROUP, HS, HS), (0, 1, 4, 2, 3)).reshape(b, HEADS // GROUP, HS, GW)


def _m_to_state(m):
    b = m.shape[0]
    return jnp.transpose(m.reshape(b, HEADS // GROUP, HS, GROUP, HS), (0, 1, 3, 4, 2)).reshape(b, HEADS, HS, HS)


def _rwkv(rw, shift0, s0, wts, batch, n_valid):
    chunk = HS
    rows = rw.shape[0]
    seq = rows // batch
    nc = seq // chunk
    nbat = RWKV_BATCHES_PER_STEP if batch % RWKV_BATCHES_PER_STEP == 0 else 1
    const2 = lambda b, c: (0, 0)
    consts = [wts[n] for n in RW_PARAM_NAMES]
    tri = (jnp.arange(chunk)[:, None] >= jnp.arange(chunk)[None, :]).astype(BF16)
    hd = jnp.arange(GW) // HS
    ones_bd = (hd[:, None] == hd[None, :]).astype(BF16)
    m0 = _state_to_m(s0)
    ngrp = HEADS // GROUP
    kern = functools.partial(_rwkv_kernel, n_valid=n_valid, nbat=nbat)
    o, m = pl.pallas_call(
        kern,
        grid=(batch // nbat, nc),
        in_specs=[pl.BlockSpec((nbat, chunk, RW_COLS), lambda b, c: (b, c, 0)),
                  pl.BlockSpec((1, 1, RW_COLS), lambda b, c: (0, 0, 0)),
                  pl.BlockSpec((1, ngrp, HS, GW), lambda b, c: (0, 0, 0, 0))]
        + [pl.BlockSpec(a.shape, const2) for a in consts]
        + [pl.BlockSpec(tri.shape, const2), pl.BlockSpec(ones_bd.shape, const2)],
        out_specs=[pl.BlockSpec((nbat, chunk, RW_W), lambda b, c: (b, c, 0)),
                   pl.BlockSpec((nbat, ngrp, HS, GW), lambda b, c: (b, 0, 0, 0))],
        out_shape=[jax.ShapeDtypeStruct((batch, seq, RW_W), BF16),
                   jax.ShapeDtypeStruct((batch, ngrp, HS, GW), F32)],
        scratch_shapes=[pltpu.VMEM((nbat, 1, RW_COLS), F32), pltpu.VMEM((nbat, ngrp, HS, GW), F32)],
        compiler_params=_params(("parallel", "arbitrary")),
        name="rwkv",
    )(rw.reshape(batch, seq, RW_COLS), shift0, m0, *consts, tri, ones_bd)
    return o.reshape(rows, RW_W), _m_to_state(m)


def _rwkv_step_kernel(rw_ref, shift_ref, s_ref, *rest):
    prm_refs = rest[:len(RW_PARAM_NAMES)]
    o_ref, sout_ref, t_scr, y_scr, tok_scr = rest[len(RW_PARAM_NAMES):]
    h = pl.program_id(0)

    @pl.when(h == 0)
    def _():
        p = {n: ref[...] for n, ref in zip(RW_PARAM_NAMES, prm_refs)}
        r, k_mod, v, logw, a_sig, g, kk, bonus = _rwkv_token_prep(rw_ref[...], shift_ref[...], p)
        t_scr[0] = r.T
        t_scr[1] = jnp.exp(logw).T
        t_scr[2] = k_mod.T
        t_scr[3] = v.T
        t_scr[4] = (-kk).T
        t_scr[5] = (kk * a_sig).T
        tok_scr[0] = bonus
        tok_scr[1] = g

    rows = pl.ds(pl.multiple_of(h * HS, HS), HS)
    r_h, w_h, k_h = t_scr[0, rows, :], t_scr[1, rows, :], t_scr[2, rows, :]
    a_h, b_h = t_scr[4, rows, :], t_scr[5, rows, :]

    def body(vv, carry):
        sv = s_ref[0, vv]
        vrow = t_scr[3, pl.ds(h * HS + vv, 1), :]
        sa = jnp.sum(sv * a_h, axis=0, keepdims=True)
        sn = sv * w_h + sa * b_h + vrow * k_h
        sout_ref[0, vv] = sn
        y_scr[pl.ds(h * HS + vv, 1), :] = jnp.sum(sn * r_h, axis=0, keepdims=True)
        return carry
    lax.fori_loop(0, HS, body, 0)

    @pl.when(h == HEADS - 1)
    def _():
        p = {n: ref[...] for n, ref in zip(RW_PARAM_NAMES, prm_refs)}
        o_ref[...] = _group_norm_out(y_scr[...].T, tok_scr[0], tok_scr[1], p).astype(o_ref.dtype)


def _rwkv_step(rw, shift, s_t, wts):
    nb = rw.shape[0]
    consts = [wts[n] for n in RW_PARAM_NAMES]
    const2 = lambda h: (0, 0)
    return pl.pallas_call(
        _rwkv_step_kernel,
        grid=(HEADS,),
        in_specs=[pl.BlockSpec((nb, RW_COLS), const2), pl.BlockSpec((nb, RW_COLS), const2),
                  pl.BlockSpec((1, HS, HS, nb), lambda h: (h, 0, 0, 0))]
        + [pl.BlockSpec(a.shape, const2) for a in consts],
        out_specs=[pl.BlockSpec((nb, RW_W), const2), pl.BlockSpec((1, HS, HS, nb), lambda h: (h, 0, 0, 0))],
        out_shape=[jax.ShapeDtypeStruct((nb, RW_W), BF16), jax.ShapeDtypeStruct(s_t.shape, F32)],
        scratch_shapes=[pltpu.VMEM((6, RW_W, nb), F32), pltpu.VMEM((RW_W, nb), F32), pltpu.VMEM((2, nb, RW_W), F32)],
        compiler_params=_params(("arbitrary",)),
        name="rwkv_step",
    )(rw, shift, s_t, *consts)


def _merge_kernel(x_ref, om_ref, or_ref, gs_ref, wom_ref, wor_ref, wout_ref, gffn_ref, x1_ref, h2_ref):
    a = _dot(om_ref[...], wom_ref[...])
    r = _dot(or_ref[...], wor_ref[...])
    mix = gs_ref[:, 0:D_MODEL] * a + gs_ref[:, D_MODEL:2 * D_MODEL] * r
    x1 = x_ref[...] + _dot(mix.astype(BF16), wout_ref[...])
    x1_ref[...] = x1
    h2_ref[...] = _rms(x1, gffn_ref[...]).astype(BF16)


def _merge(x, om, orw, gsig, wts, tm):
    rows = x.shape[0]
    row = lambda i: (i, 0)
    const = lambda i: (0, 0)
    full = lambda a: pl.BlockSpec(a.shape, const)
    return pl.pallas_call(
        _merge_kernel,
        grid=(rows // tm,),
        in_specs=[pl.BlockSpec((tm, D_MODEL), row), pl.BlockSpec((tm, HEADS * VDIM), row),
                  pl.BlockSpec((tm, RW_W), row), pl.BlockSpec((tm, 2 * D_MODEL), row),
                  full(wts['w_o_mla']), full(wts['w_o_rwkv']), full(wts['w_out']), full(wts['g_ffn'])],
        out_specs=[pl.BlockSpec((tm, D_MODEL), row), pl.BlockSpec((tm, D_MODEL), row)],
        out_shape=[jax.ShapeDtypeStruct((rows, D_MODEL), F32), jax.ShapeDtypeStruct((rows, D_MODEL), BF16)],
        compiler_params=_params(("parallel",)),
        name="merge",
    )(x, om, orw, gsig, wts['w_o_mla'], wts['w_o_rwkv'], wts['w_out'], wts['g_ffn'])


def _ffn_kernel(x1_ref, h2_ref, wup_ref, wdn_ref, gfin_ref, y_ref):
    u = jnp.maximum(_dot(h2_ref[...], wup_ref[...]), 0.0)
    x2 = x1_ref[...] + _dot((u * u).astype(BF16), wdn_ref[...])
    y_ref[...] = _rms(x2, gfin_ref[...])


def _ffn(x1, h2, wts, tm):
    rows = x1.shape[0]
    row = lambda i: (i, 0)
    const = lambda i: (0, 0)
    full = lambda a: pl.BlockSpec(a.shape, const)
    return pl.pallas_call(
        _ffn_kernel,
        grid=(rows // tm,),
        in_specs=[pl.BlockSpec((tm, D_MODEL), row), pl.BlockSpec((tm, D_MODEL), row),
                  full(wts['w_up']), full(wts['w_down']), full(wts['g_final'])],
        out_specs=pl.BlockSpec((tm, D_MODEL), row),
        out_shape=jax.ShapeDtypeStruct((rows, D_MODEL), F32),
        compiler_params=_params(("parallel",)),
        name="ffn",
    )(x1, h2, wts['w_up'], wts['w_down'], wts['g_final'])


def _swap_halves(w):
    return jnp.concatenate([w[..., ROPE // 2:], w[..., :ROPE // 2]], axis=-1)


def _prep_weights(g_final, g_mix, w_in, g_q, w_uq, g_kv, w_uk, w_uv, w_o_mla, mu_shift, w0, w2, a0, a2, g2,
                  k_k, k_a, r_k, ln_w, ln_b, w_o_rwkv, w_out, g_ffn, w_up, w_down):
    row = lambda a: a.reshape(1, -1).astype(F32)
    q_in = w_in[:, 0:Q_LORA]
    kv_in = w_in[:, Q_LORA:Q_LORA + KV_LORA]
    kr = w_in[:, Q_LORA + KV_LORA:Q_LORA + KV_LORA + ROPE]
    o_rw = Q_LORA + KV_LORA + ROPE
    rw = w_in[:, o_rw:o_rw + RW_COLS]
    gate = w_in[:, o_rw + RW_COLS:]
    zr = jnp.zeros((D_MODEL, ROPE), F32)
    kr_a = jnp.concatenate([kr, zr, kr, zr], axis=1)
    krs = _swap_halves(kr)
    kr_b = jnp.concatenate([krs, zr, krs, zr], axis=1)
    w_in_p = jnp.concatenate([gate, rw, q_in, kv_in, kr_a, kr_b], axis=1).astype(BF16)

    uq = w_uq.reshape(Q_LORA, HEADS, NOPE + ROPE)
    zq = jnp.zeros((Q_LORA, HEADS, HP - NOPE - ROPE), F32)
    uq_pad = jnp.concatenate([uq, zq], axis=-1).reshape(Q_LORA, HEADS * HP)
    uq_sw = jnp.concatenate([jnp.zeros((Q_LORA, HEADS, NOPE), F32), _swap_halves(uq[..., NOPE:]), zq], axis=-1)
    w_uq_p = jnp.concatenate([uq_pad, uq_sw.reshape(Q_LORA, HEADS * HP)], axis=1).astype(BF16)

    uk = w_uk.reshape(KV_LORA, HEADS, NOPE)
    uk_pad = jnp.concatenate([uk, jnp.zeros((KV_LORA, HEADS, HP - NOPE), F32)], axis=-1).reshape(KV_LORA, HEADS * HP)
    w_ukv = uk_pad.astype(BF16)

    head = jnp.arange(RW_W) // HS
    seg = (head[:, None] == head[None, :]).astype(BF16)
    return dict(
        g_mix=row(g_mix), w_in=w_in_p, g_q=row(g_q), w_uq=w_uq_p, g_kv=row(g_kv), w_ukv=w_ukv,
        w_uk=w_uk.astype(BF16), w_uv=w_uv.astype(BF16), w_uvt=w_uv.T.astype(BF16), w_o_mla=w_o_mla.astype(BF16),
        mu_shift=row(mu_shift), w0=row(w0), w2=w2.astype(BF16), a0=row(a0), a2=a2.astype(BF16),
        g2=g2.astype(BF16), k_k=row(k_k), k_a=row(k_a), r_k=row(r_k), ln_w=row(ln_w), ln_b=row(ln_b), seg=seg,
        w_o_rwkv=w_o_rwkv.astype(BF16), w_out=w_out.astype(BF16), g_ffn=row(g_ffn),
        w_up=w_up.astype(BF16), w_down=w_down.astype(BF16), g_final=row(g_final))


def _rope_tables(pos):
    inv = ROPE_BASE ** (-jnp.arange(0, ROPE, 2, dtype=F32) / ROPE)
    ang = pos.astype(F32)[:, None] * inv[None, :]
    cos, sin = jnp.cos(ang), jnp.sin(ang)
    n = pos.shape[0]
    cc = jnp.concatenate([cos, cos], axis=1)
    ss = jnp.concatenate([-sin, sin], axis=1)
    z32 = jnp.zeros((n, ROPE), F32)
    ctq = jnp.concatenate([jnp.ones((n, NOPE), F32), cc, z32], axis=1)
    stq = jnp.concatenate([jnp.zeros((n, NOPE), F32), ss, z32], axis=1)
    ctk = jnp.concatenate([cc, z32, cc, z32], axis=1)
    stk = jnp.concatenate([ss, z32, ss, z32], axis=1)
    return ctq, stq, ctk, stk


def _row_tile(rows, cap):
    tm = min(rows, cap)
    assert rows % tm == 0
    return tm


def kernel(x_prompt, x_sample, cache_kv, page_table, state_wkv, state_shift, meta_tokens, g_final, g_mix, w_in, g_q, w_uq, g_kv, w_uk, w_uv, w_o_mla, mu_shift, w0, w2, a0, a2, g2, k_k, k_a, r_k, ln_w, ln_b, w_o_rwkv, w_out, g_ffn, w_up, w_down):
    depth = g_mix.shape[0]
    assert depth == 1
    bp, seq, _ = x_prompt.shape
    bd, s_s, _ = x_sample.shape
    assert s_s == 1
    n_pages = page_table.shape[1]
    past_len = n_pages * PAGE
    wts = _prep_weights(g_final, g_mix[0], w_in[0], g_q[0], w_uq[0], g_kv[0], w_uk[0], w_uv[0], w_o_mla[0],
                        mu_shift[0], w0[0], w2[0], a0[0], a2[0], g2[0], k_k[0], k_a[0], r_k[0], ln_w[0], ln_b[0],
                        w_o_rwkv[0], w_out[0], g_ffn[0], w_up[0], w_down[0])

    tab_m = _rope_tables(jnp.arange(N_META))
    _, rw_m, _, kvrow_m, kpad_m, vt_m = _proj(meta_tokens.astype(F32), tab_m, wts, N_META, BF16)
    rw_m_pad = jnp.concatenate([rw_m, jnp.zeros((HS - N_META, RW_COLS), F32)], axis=0)
    zero_shift = jnp.zeros((1, 1, RW_COLS), F32)
    zero_state = jnp.zeros((1, HEADS, HS, HS), F32)
    _, s_meta = _rwkv(rw_m_pad, zero_shift, zero_state, wts, 1, N_META)
    shift_meta = rw_m[N_META - 1:N_META].reshape(1, 1, RW_COLS)

    rows_p = bp * seq
    tm = _row_tile(seq, 256)
    xp = x_prompt.reshape(rows_p, D_MODEL)
    tab_p = _rope_tables(N_META + jnp.arange(seq))
    gsig, rw, q, kvrow, kpad, vt = _proj(xp, tab_p, wts, tm, BF16)
    o_mla = _attn(q, kpad, vt, kpad_m, vt_m, bp, seq, tm)
    o_rwkv, s_p = _rwkv(rw, shift_meta, s_meta, wts, bp, None)
    x1, h2 = _merge(xp, o_mla, o_rwkv, gsig, wts, tm)
    y_prompt = _ffn(x1, h2, wts, tm).reshape(bp, seq, D_MODEL)
    kv_prompt = jnp.concatenate([jnp.broadcast_to(kvrow_m[None], (bp, N_META, KV_W)),
                                 kvrow.reshape(bp, seq, KV_W)], axis=1)[None]
    shift_prompt = rw.reshape(bp, seq, RW_COLS)[:, -1][None]

    xs = x_sample.reshape(bd, D_MODEL)
    tab_s = _rope_tables(jnp.full((bd,), past_len, jnp.int32))
    gsig_s, rw_s, q_s, kvrow_s, _, _ = _proj(xs, tab_s, wts, bd, F32)
    qf = jnp.transpose(_qlat(q_s, wts['w_uk']), (1, 0, 2))
    cache = jnp.swapaxes(cache_kv.reshape(cache_kv.shape[1], PAGE, KV_W), 1, 2)
    o_mla_s = _sattn(page_table, qf, kvrow_s.reshape(bd, 1, KV_W), wts['w_uv'], cache).reshape(bd, HEADS * VDIM)
    s_t = jnp.transpose(state_wkv[0], (1, 2, 3, 0))
    o_rwkv_s, s_t_new = _rwkv_step(rw_s, state_shift[0], s_t, wts)
    s_s_new = jnp.transpose(s_t_new, (3, 0, 1, 2))
    x1_s, h2_s = _merge(xs, o_mla_s, o_rwkv_s, gsig_s, wts, bd)
    y_sample = _ffn(x1_s, h2_s, wts, bd).reshape(bd, 1, D_MODEL)

    return (y_prompt, y_sample,
            kv_prompt.astype(cache_kv.dtype), s_p[None].astype(state_wkv.dtype),
            shift_prompt.astype(state_shift.dtype),
            kvrow_s.reshape(1, bd, 1, KV_W).astype(cache_kv.dtype), s_s_new[None].astype(state_wkv.dtype),
            rw_s[None].astype(state_shift.dtype))
```

```python
import functools

import jax
import jax.numpy as jnp
from jax import lax
from jax.experimental import pallas as pl
from jax.experimental.pallas import tpu as pltpu

F32 = jnp.float32
BF16 = jnp.bfloat16

D_MODEL = 1024
N_META = 16
PAGE = 128
HEADS = 8
NOPE = 64
ROPE = 32
VDIM = 64
Q_LORA = 384
KV_LORA = 256
KV_W = KV_LORA + ROPE
ROPE_BASE = 10000.0
SM_SCALE = (NOPE + ROPE) ** -0.5
HS = 64
RW_W = HEADS * HS
W_LORA = 64
A_LORA = 64
G_LORA = 128
RW_COLS = 3 * RW_W + W_LORA + A_LORA + G_LORA
GN_EPS = 64e-5
D_FF = 4 * D_MODEL
NORM_EPS = 1e-6
NEG_INF = -1e30
DECAY_SCALE = 0.6065306597126334

LANE = 128
HP = 128
VMEM_LIMIT = 56 * 1024 * 1024

C_GATE = 0
C_RW = 2 * D_MODEL
C_QIN = C_RW + RW_COLS
C_KVIN = C_QIN + Q_LORA
C_KR = C_KVIN + KV_LORA
IN_PERM = C_KR + 2 * LANE


def _dot(a, b):
    return jnp.dot(a, b, preferred_element_type=F32)


def _dot_nt(a, b):
    return lax.dot_general(a, b, (((1,), (1,)), ((), ())), preferred_element_type=F32)


def _dot_tn(a, b):
    return lax.dot_general(a, b, (((0,), (0,)), ((), ())), preferred_element_type=F32)


def _split2(x):
    hi = x.astype(BF16)
    lo = (x - hi.astype(F32)).astype(BF16)
    return hi, lo


def _split3(x):
    x1 = x.astype(BF16)
    r1 = x - x1.astype(F32)
    x2 = r1.astype(BF16)
    x3 = (r1 - x2.astype(F32)).astype(BF16)
    return x1, x2, x3


def _rms(x, g):
    return x * lax.rsqrt(jnp.mean(x * x, axis=-1, keepdims=True) + NORM_EPS) * g


def _params(sem):
    return pltpu.CompilerParams(dimension_semantics=sem, vmem_limit_bytes=VMEM_LIMIT)


def _proj_kernel(x_ref, gmix_ref, win_ref, gq_ref, wuq_ref, gkv_ref, wukv_ref, wuvt_ref,
                 ctq_ref, stq_ref, ctk_ref, stk_ref,
                 gsig_ref, rw_ref, q_ref, kvrow_ref, kpad_ref, vt_ref):
    h = _rms(x_ref[...], gmix_ref[...]).astype(BF16)
    step = 512
    for c in range(0, 2 * D_MODEL, step):
        gsig_ref[:, c:c + step] = jax.nn.sigmoid(_dot(h, win_ref[:, C_GATE + c:C_GATE + c + step])).astype(BF16)
    for c in range(0, RW_COLS, 256):
        rw_ref[:, c:c + 256] = _dot(h, win_ref[:, C_RW + c:C_RW + c + 256])

    qn = _rms(_dot(h, win_ref[:, C_QIN:C_QIN + Q_LORA]), gq_ref[...]).astype(BF16)
    ctq = ctq_ref[...]
    stq = stq_ref[...]
    qa = _dot(qn, wuq_ref[:, 0:HEADS * HP])
    qb = _dot(qn, wuq_ref[:, HEADS * HP:2 * HEADS * HP])
    for hh in range(HEADS):
        hs = slice(hh * HP, (hh + 1) * HP)
        q_ref[:, hs] = (qa[:, hs] * ctq + qb[:, hs] * stq).astype(q_ref.dtype)

    ckv = _rms(_dot(h, win_ref[:, C_KVIN:C_KVIN + KV_LORA]), gkv_ref[...])
    kvrow_ref[:, 0:KV_LORA] = ckv
    zkr = _dot(h, win_ref[:, C_KR:C_KR + 2 * LANE])
    kr = zkr[:, 0:LANE] * ctk_ref[...] + zkr[:, LANE:2 * LANE] * stk_ref[...]
    kvrow_ref[:, KV_LORA:KV_W] = kr[:, 0:ROPE]
    lane = lax.broadcasted_iota(jnp.int32, kr.shape, 1)
    kr_head = jnp.where((lane >= NOPE) & (lane < NOPE + ROPE), kr, 0.0)
    cb = ckv.astype(BF16)
    kn = _dot(cb, wukv_ref[...])
    for hh in range(HEADS):
        hs = slice(hh * HP, (hh + 1) * HP)
        kpad_ref[:, hs] = (kn[:, hs] + kr_head).astype(BF16)
    vt_ref[0] = _dot_nt(wuvt_ref[...], cb).astype(BF16)


def _proj(x, tables, wts, tm, q_dtype):
    rows = x.shape[0]
    nt = tables[0].shape[0] // tm
    row = lambda i: (i, 0)
    const = lambda i: (0, 0)
    tab = lambda i: (i % nt, 0)
    full = lambda a: pl.BlockSpec(a.shape, const)
    in_specs = [pl.BlockSpec((tm, D_MODEL), row), full(wts['g_mix']), full(wts['w_in']), full(wts['g_q']),
                full(wts['w_uq']), full(wts['g_kv']), full(wts['w_ukv']), full(wts['w_uvt'])]
    in_specs += [pl.BlockSpec((tm, LANE), tab)] * 4
    widths = [(2 * D_MODEL, BF16), (RW_COLS, F32), (HEADS * HP, q_dtype), (KV_W, F32), (HEADS * HP, BF16)]
    nt_rows = rows // tm
    return pl.pallas_call(
        _proj_kernel,
        grid=(nt_rows,),
        in_specs=in_specs,
        out_specs=[pl.BlockSpec((tm, w), row) for w, _ in widths]
        + [pl.BlockSpec((1, HEADS * VDIM, tm), lambda i: (i, 0, 0))],
        out_shape=[jax.ShapeDtypeStruct((rows, w), dt) for w, dt in widths]
        + [jax.ShapeDtypeStruct((nt_rows, HEADS * VDIM, tm), BF16)],
        compiler_params=_params(("parallel",)),
        name="proj",
    )(x, wts['g_mix'], wts['w_in'], wts['g_q'], wts['w_uq'], wts['g_kv'], wts['w_ukv'], wts['w_uvt'], *tables)


def _attn_kernel(q_ref, k_ref, vt_ref, km_ref, vmt_ref, eye_ref, o_ref, *, tq):
    i = pl.program_id(1)
    qs = [q_ref[:, hh * HP:(hh + 1) * HP] for hh in range(HEADS)]

    def update(carry, k_tiles, vt_tiles, mask):
        ss = [_dot_nt(k_tiles[hh], qs[hh]) for hh in range(HEADS)]
        ps, stats = [], []
        for hh in range(HEADS):
            m, l, _ = carry[hh]
            s = ss[hh] * SM_SCALE
            if mask is not None:
                s = jnp.where(mask, s, NEG_INF)
            m_new = jnp.maximum(m, jnp.max(s, axis=0, keepdims=True))
            alpha = jnp.exp(m - m_new)
            p = jnp.exp(s - m_new)
            stats.append((m_new, l * alpha + jnp.sum(p, axis=0, keepdims=True), alpha))
            ps.append(p.astype(BF16))
        pv = [_dot(vt_tiles[hh], ps[hh]) for hh in range(HEADS)]
        return tuple((stats[hh][0], stats[hh][1], carry[hh][2] * stats[hh][2] + pv[hh]) for hh in range(HEADS))

    init = (jnp.full((1, tq), NEG_INF, F32), jnp.zeros((1, tq), F32), jnp.zeros((VDIM, tq), F32))
    carry = update((init,) * HEADS, [km_ref[:, hh * HP:(hh + 1) * HP] for hh in range(HEADS)],
                   [vmt_ref[0, hh * VDIM:(hh + 1) * VDIM, :] for hh in range(HEADS)], None)

    def tile(j, carry, mask):
        rows = pl.ds(pl.multiple_of(j * tq, tq), tq)
        return update(carry, [k_ref[rows, hh * HP:(hh + 1) * HP] for hh in range(HEADS)],
                      [vt_ref[j, hh * VDIM:(hh + 1) * VDIM, :] for hh in range(HEADS)], mask)

    carry = lax.fori_loop(0, i, lambda j, c: tile(j, c, None), carry)
    key = lax.broadcasted_iota(jnp.int32, (tq, tq), 0)
    qry = lax.broadcasted_iota(jnp.int32, (tq, tq), 1)
    carry = tile(i, carry, key <= qry)

    outs = []
    for hh in range(HEADS):
        m, l, acc = carry[hh]
        outs.append(_dot_tn((acc / l).astype(BF16), eye_ref[...]))
    o_ref[...] = jnp.concatenate(outs, axis=-1).astype(o_ref.dtype)


def _attn(q, kpad, vt, km, vmt, batch, seq, tq):
    nq = seq // tq
    eye = jnp.eye(VDIM, dtype=BF16)
    kern = functools.partial(_attn_kernel, tq=tq)
    return pl.pallas_call(
        kern,
        grid=(batch, nq),
        in_specs=[
            pl.BlockSpec((tq, HEADS * HP), lambda b, i: (b * nq + i, 0)),
            pl.BlockSpec((seq, HEADS * HP), lambda b, i: (b, 0)),
            pl.BlockSpec((nq, HEADS * VDIM, tq), lambda b, i: (b, 0, 0)),
            pl.BlockSpec(km.shape, lambda b, i: (0, 0)),
            pl.BlockSpec(vmt.shape, lambda b, i: (0, 0, 0)),
            pl.BlockSpec(eye.shape, lambda b, i: (0, 0)),
        ],
        out_specs=pl.BlockSpec((tq, HEADS * VDIM), lambda b, i: (b * nq + i, 0)),
        out_shape=jax.ShapeDtypeStruct((batch * seq, HEADS * VDIM), BF16),
        compiler_params=_params(("parallel", "arbitrary")),
        name="attn",
    )(q, kpad, vt, km, vmt, eye)


def _qlat_kernel(q_ref, wuk_ref, o_ref):
    for hh in range(HEADS):
        qn = q_ref[:, hh * HP:hh * HP + NOPE].astype(BF16)
        o_ref[hh, :, 0:KV_LORA] = _dot_nt(qn, wuk_ref[:, hh * NOPE:(hh + 1) * NOPE]) * SM_SCALE
        o_ref[hh, :, KV_LORA:KV_W] = q_ref[:, hh * HP + NOPE:hh * HP + NOPE + ROPE] * SM_SCALE


def _qlat(q, wuk):
    rows = q.shape[0]
    return pl.pallas_call(
        _qlat_kernel,
        out_shape=jax.ShapeDtypeStruct((HEADS, rows, KV_W), F32),
        name="qlat",
    )(q, wuk)


SATTN_SLOTS = 3


def _sattn_kernel(pt_ref, qf_ref, kvn_ref, wuv_ref, cache_ref, o_ref, buf, sem, *, n_pages, chunk):
    b = pl.program_id(0)
    nb = pl.num_programs(0)

    def page_copy(bb, slot, j):
        return pltpu.make_async_copy(cache_ref.at[pt_ref[bb, j]], buf.at[slot, :, pl.ds(j * PAGE, PAGE)], sem.at[slot])

    def fetch(bb, slot):
        def body(j, c):
            page_copy(bb, slot, j).start()
            return c
        lax.fori_loop(0, n_pages, body, 0)

    @pl.when(b == 0)
    def _():
        for ahead in range(SATTN_SLOTS - 1):
            @pl.when(ahead < nb)
            def _():
                fetch(ahead, ahead)

    @pl.when(b + SATTN_SLOTS - 1 < nb)
    def _():
        fetch(b + SATTN_SLOTS - 1, (b + SATTN_SLOTS - 1) % SATTN_SLOTS)

    slot = b % SATTN_SLOTS

    def wait_body(j, c):
        page_copy(b, slot, j).wait()
        return c
    lax.fori_loop(0, n_pages, wait_body, 0)

    qf = qf_ref[0].astype(BF16)
    m = jnp.full((HEADS, 1), NEG_INF, F32)
    l = jnp.zeros((HEADS, 1), F32)
    acc = jnp.zeros((HEADS, KV_LORA), F32)
    for c0 in range(0, n_pages * PAGE, chunk):
        kt = buf[slot, :, c0:c0 + chunk].astype(BF16)
        s = _dot(qf, kt)
        m_new = jnp.maximum(m, jnp.max(s, axis=-1, keepdims=True))
        alpha = jnp.exp(m - m_new)
        p = jnp.exp(s - m_new)
        l = l * alpha + jnp.sum(p, axis=-1, keepdims=True)
        acc = acc * alpha + _dot_nt(p.astype(BF16), kt[0:KV_LORA, :])
        m = m_new
    kvn = kvn_ref[0].astype(BF16).astype(F32)
    s = jnp.sum(qf.astype(F32) * kvn, axis=-1, keepdims=True)
    m_new = jnp.maximum(m, s)
    alpha = jnp.exp(m - m_new)
    p = jnp.exp(s - m_new)
    l = l * alpha + p
    acc = acc * alpha + p.astype(BF16).astype(F32) * kvn[:, 0:KV_LORA]
    o_lat = (acc / l).astype(BF16)
    full = _dot(o_lat, wuv_ref[...])
    r = lax.broadcasted_iota(jnp.int32, full.shape, 0)
    cidx = lax.broadcasted_iota(jnp.int32, full.shape, 1)
    o_ref[0] = jnp.sum(jnp.where(cidx // VDIM == r, full, 0.0), axis=0, keepdims=True).astype(o_ref.dtype)


def _sattn(page_table, qf, kvn, wuv, cache):
    nb, n_pages = page_table.shape
    chunk = min(2048, n_pages * PAGE)
    kern = functools.partial(_sattn_kernel, n_pages=n_pages, chunk=chunk)
    grid_spec = pltpu.PrefetchScalarGridSpec(
        num_scalar_prefetch=1,
        grid=(nb,),
        in_specs=[
            pl.BlockSpec((1, HEADS, KV_W), lambda b, pt: (b, 0, 0)),
            pl.BlockSpec((1, 1, KV_W), lambda b, pt: (b, 0, 0)),
            pl.BlockSpec(wuv.shape, lambda b, pt: (0, 0)),
            pl.BlockSpec(memory_space=pl.ANY),
        ],
        out_specs=pl.BlockSpec((1, 1, HEADS * VDIM), lambda b, pt: (b, 0, 0)),
        scratch_shapes=[pltpu.VMEM((SATTN_SLOTS, KV_W, n_pages * PAGE), F32),
                        pltpu.SemaphoreType.DMA((SATTN_SLOTS,))],
    )
    return pl.pallas_call(
        kern,
        grid_spec=grid_spec,
        out_shape=jax.ShapeDtypeStruct((nb, 1, HEADS * VDIM), BF16),
        compiler_params=_params(("arbitrary",)),
        name="sattn",
    )(page_table, qf, kvn, wuv, cache)


GROUP = 4
RWKV_BATCHES_PER_STEP = 4
GW = GROUP * HS
RW_PARAM_NAMES = ['mu_shift', 'w0', 'w2', 'a0', 'a2', 'g2', 'k_k', 'k_a', 'r_k', 'ln_w', 'ln_b', 'seg']


def _seg_sum(x, seg):
    hi, lo = _split2(x)
    return _dot(hi, seg) + _dot(lo, seg)


def _rwkv_token_prep(cur, prev, p):
    z = cur + (prev - cur) * p['mu_shift']
    r = z[:, 0:RW_W]
    k = z[:, RW_W:2 * RW_W]
    v = z[:, 2 * RW_W:3 * RW_W]
    w_in = z[:, 3 * RW_W:3 * RW_W + W_LORA]
    a_in = z[:, 3 * RW_W + W_LORA:3 * RW_W + W_LORA + A_LORA]
    g_in = z[:, 3 * RW_W + W_LORA + A_LORA:RW_COLS]
    w_log = p['w0'] + _dot(jnp.tanh(w_in).astype(BF16), p['w2'])
    logw = -DECAY_SCALE * jax.nn.sigmoid(w_log)
    a_sig = jax.nn.sigmoid(p['a0'] + _dot(a_in.astype(BF16), p['a2']))
    g = _dot(jax.nn.sigmoid(g_in).astype(BF16), p['g2'])
    kk = k * p['k_k']
    kk = kk / jnp.maximum(jnp.sqrt(_seg_sum(kk * kk, p['seg'])), 1e-12)
    k_mod = k * (1.0 + (a_sig - 1.0) * p['k_a'])
    bonus = _seg_sum(r * k_mod * p['r_k'], p['seg']) * v
    return r, k_mod, v, logw, a_sig, g, kk, bonus


def _group_norm_out(y, bonus, g, p):
    mean = _seg_sum(y, p['seg']) * (1.0 / HS)
    d = y - mean
    var = _seg_sum(d * d, p['seg']) * (1.0 / HS)
    yn = d * lax.rsqrt(var + GN_EPS) * p['ln_w'] + p['ln_b']
    return (yn + bonus) * g


def _rwkv_kernel(rw_ref, shift_ref, m0_ref, *rest, n_valid, nbat):
    prm_refs = rest[:len(RW_PARAM_NAMES)]
    tri_ref, ones_ref, o_ref, mout_ref, prev_scr, m_scr = rest[len(RW_PARAM_NAMES):]
    p = {n: ref[...] for n, ref in zip(RW_PARAM_NAMES, prm_refs)}
    chunk = HS
    ngrp = HEADS // GROUP
    c = pl.program_id(1)
    nc = pl.num_programs(1)

    @pl.when(c == 0)
    def _():
        for bi in range(nbat):
            prev_scr[bi] = shift_ref[0]
            m_scr[bi] = m0_ref[0]

    curs, prevs = [], []
    for bi in range(nbat):
        cur_b = rw_ref[bi]
        rowi = lax.broadcasted_iota(jnp.int32, cur_b.shape, 0)
        prevs.append(jnp.where(rowi == 0, prev_scr[bi], pltpu.roll(cur_b, 1, 0)))
        prev_scr[bi] = cur_b[chunk - 1:chunk, :]
        curs.append(cur_b)
    r, k_mod, v, logw, a_sig, g, kk, bonus = _rwkv_token_prep(
        jnp.concatenate(curs, axis=0), jnp.concatenate(prevs, axis=0), p)

    if n_valid is not None:
        step = lax.broadcasted_iota(jnp.int32, (nbat * chunk, RW_W), 0) % chunk
        live = (c * chunk + step) < n_valid
        logw = jnp.where(live, logw, 0.0)
        kk = jnp.where(live, kk, 0.0)
        k_mod = jnp.where(live, k_mod, 0.0)
        v = jnp.where(live, v, 0.0)

    lw3 = _split3(logw)
    seq_rows = [slice(bi * chunk, (bi + 1) * chunk) for bi in range(nbat)]
    gcum = jnp.concatenate([sum(_dot(tri_ref[...], t[rs]) for t in lw3) for rs in seq_rows], axis=0)
    g_end = jnp.concatenate([jnp.broadcast_to(gcum[rs][chunk - 1:chunk, :], (chunk, RW_W)) for rs in seq_rows], axis=0)
    r_t = r * jnp.exp(gcum)
    a_t = -kk * jnp.exp(gcum - logw)
    eng = jnp.exp(-gcum)
    b_t = kk * a_sig * eng
    k_t = k_mod * eng
    tail = jnp.exp(g_end - gcum)
    b_e = kk * a_sig * tail
    k_e = k_mod * tail
    decay_end = jnp.exp(g_end)

    ti = lax.broadcasted_iota(jnp.int32, (chunk, GW), 0)
    lane = lax.broadcasted_iota(jnp.int32, (chunk, GW), 1)
    si = lane % HS
    lane_head = lane // HS
    strict = si < ti
    incl = si <= ti
    eye = (si == ti).astype(F32)

    def bdr(zf):
        return jnp.concatenate([jnp.where(lane_head == hh, zf, 0.0) for hh in range(GROUP)], axis=0).astype(BF16)

    streams = [(bi, gi) for bi in range(nbat) for gi in range(ngrp)]
    blk = lambda t, s: t[s[0] * chunk:(s[0] + 1) * chunk, s[1] * GW:(s[1] + 1) * GW]
    cat0 = lambda xs: jnp.concatenate(xs, axis=0).astype(BF16)
    cat1 = lambda xs: jnp.concatenate(xs, axis=1).astype(BF16)

    ar = [cat0([blk(a_t, s), blk(r_t, s)]) for s in streams]
    ab = [_dot_nt(ar[i], bdr(blk(b_t, s))) for i, s in enumerate(streams)]
    ak = [_dot_nt(ar[i], bdr(blk(k_t, s))) for i, s in enumerate(streams)]
    n = [jnp.where(strict, t[0:chunk], 0.0) for t in ab]
    a_rb = [jnp.where(incl, t[chunk:2 * chunk], 0.0) for t in ab]
    a_ak = [jnp.where(strict, t[0:chunk], 0.0) for t in ak]
    a_rk = [jnp.where(incl, t[chunk:2 * chunk], 0.0) for t in ak]

    x = [eye + t for t in n]
    pw = [_dot(t.astype(BF16), bdr(t)) for t in n]
    for _ in range(4):
        z = [_dot(cat0([xi, pi]), bdr(pi)) for xi, pi in zip(x, pw)]
        x = [xi + zi[0:chunk] for xi, zi in zip(x, z)]
        pw = [zi[chunk:2 * chunk] for zi in z]
    z = [_dot(xi.astype(BF16), bdr(pi)) for xi, pi in zip(x, pw)]
    x = [xi + zi for xi, zi in zip(x, z)]

    m0 = [m_scr[s[0], s[1]] for s in streams]
    bm = [bdr(t) for t in m0]
    bv = [bdr(blk(v, s)) for s in streams]
    rhs = [_dot(cat1([blk(a_t, s), a_ak[i]]), jnp.concatenate([bm[i], bv[i]], axis=0)) for i, s in enumerate(streams)]
    u = [_dot(xi.astype(BF16), bdr(ri)) for xi, ri in zip(x, rhs)]
    y = [_dot(cat1([blk(r_t, s), a_rb[i], a_rk[i]]), jnp.concatenate([bm[i], bdr(u[i]), bv[i]], axis=0))
         for i, s in enumerate(streams)]
    full = [_dot_tn(cat0([blk(b_e, s), blk(k_e, s)]), cat0([u[i], blk(v, s)])) for i, s in enumerate(streams)]
    gm = [sum(_dot(t, ones_ref[...]) for t in _split3(eye * blk(decay_end, s)[0:1])) for s in streams]
    for i, s in enumerate(streams):
        upd = jnp.where(lane_head == 0, full[i][0:HS], 0.0)
        for hh in range(1, GROUP):
            upd = upd + jnp.where(lane_head == hh, full[i][hh * HS:(hh + 1) * HS], 0.0)
        m_scr[s[0], s[1]] = gm[i] * m0[i] + upd

    y_all = jnp.concatenate([jnp.concatenate([y[bi * ngrp + gi] for gi in range(ngrp)], axis=1)
                             for bi in range(nbat)], axis=0)
    out = _group_norm_out(y_all, bonus, g, p).astype(o_ref.dtype)
    for bi in range(nbat):
        o_ref[bi] = out[seq_rows[bi]]

    @pl.when(c == nc - 1)
    def _():
        mout_ref[...] = m_scr[...]


def _state_to_m(s):
    b = s.shape[0]
    return jnp.transpose(s.reshape(b, HEADS // GROUP, GROUP, HS, HS), (0, 1, 4, 2, 3)).reshape(b, HEADS // GROUP, HS, GW)


def _m_to_state(m):
    b = m.shape[0]
    return jnp.transpose(m.reshape(b, HEADS // GROUP, HS, GROUP, HS), (0, 1, 3, 4, 2)).reshape(b, HEADS, HS, HS)


def _rwkv(rw, shift0, s0, wts, batch, n_valid):
    chunk = HS
    rows = rw.shape[0]
    seq = rows // batch
    nc = seq // chunk
    nbat = RWKV_BATCHES_PER_STEP if batch % RWKV_BATCHES_PER_STEP == 0 else 1
    const2 = lambda b, c: (0, 0)
    consts = [wts[n] for n in RW_PARAM_NAMES]
    tri = (jnp.arange(chunk)[:, None] >= jnp.arange(chunk)[None, :]).astype(BF16)
    hd = jnp.arange(GW) // HS
    ones_bd = (hd[:, None] == hd[None, :]).astype(BF16)
    m0 = _state_to_m(s0)
    ngrp = HEADS // GROUP
    kern = functools.partial(_rwkv_kernel, n_valid=n_valid, nbat=nbat)
    o, m = pl.pallas_call(
        kern,
        grid=(batch // nbat, nc),
        in_specs=[pl.BlockSpec((nbat, chunk, RW_COLS), lambda b, c: (b, c, 0)),
                  pl.BlockSpec((1, 1, RW_COLS), lambda b, c: (0, 0, 0)),
                  pl.BlockSpec((1, ngrp, HS, GW), lambda b, c: (0, 0, 0, 0))]
        + [pl.BlockSpec(a.shape, const2) for a in consts]
        + [pl.BlockSpec(tri.shape, const2), pl.BlockSpec(ones_bd.shape, const2)],
        out_specs=[pl.BlockSpec((nbat, chunk, RW_W), lambda b, c: (b, c, 0)),
                   pl.BlockSpec((nbat, ngrp, HS, GW), lambda b, c: (b, 0, 0, 0))],
        out_shape=[jax.ShapeDtypeStruct((batch, seq, RW_W), BF16),
                   jax.ShapeDtypeStruct((batch, ngrp, HS, GW), F32)],
        scratch_shapes=[pltpu.VMEM((nbat, 1, RW_COLS), F32), pltpu.VMEM((nbat, ngrp, HS, GW), F32)],
        compiler_params=_params(("parallel", "arbitrary")),
        name="rwkv",
    )(rw.reshape(batch, seq, RW_COLS), shift0, m0, *consts, tri, ones_bd)
    return o.reshape(rows, RW_W), _m_to_state(m)


def _rwkv_step_kernel(rw_ref, shift_ref, s_ref, *rest):
    prm_refs = rest[:len(RW_PARAM_NAMES)]
    o_ref, sout_ref, t_scr, y_scr, tok_scr = rest[len(RW_PARAM_NAMES):]
    h = pl.program_id(0)

    @pl.when(h == 0)
    def _():
        p = {n: ref[...] for n, ref in zip(RW_PARAM_NAMES, prm_refs)}
        r, k_mod, v, logw, a_sig, g, kk, bonus = _rwkv_token_prep(rw_ref[...], shift_ref[...], p)
        t_scr[0] = r.T
        t_scr[1] = jnp.exp(logw).T
        t_scr[2] = k_mod.T
        t_scr[3] = v.T
        t_scr[4] = (-kk).T
        t_scr[5] = (kk * a_sig).T
        tok_scr[0] = bonus
        tok_scr[1] = g

    rows = pl.ds(pl.multiple_of(h * HS, HS), HS)
    r_h, w_h, k_h = t_scr[0, rows, :], t_scr[1, rows, :], t_scr[2, rows, :]
    a_h, b_h = t_scr[4, rows, :], t_scr[5, rows, :]

    def body(vv, carry):
        sv = s_ref[0, vv]
        vrow = t_scr[3, pl.ds(h * HS + vv, 1), :]
        sa = jnp.sum(sv * a_h, axis=0, keepdims=True)
        sn = sv * w_h + sa * b_h + vrow * k_h
        sout_ref[0, vv] = sn
        y_scr[pl.ds(h * HS + vv, 1), :] = jnp.sum(sn * r_h, axis=0, keepdims=True)
        return carry
    lax.fori_loop(0, HS, body, 0)

    @pl.when(h == HEADS - 1)
    def _():
        p = {n: ref[...] for n, ref in zip(RW_PARAM_NAMES, prm_refs)}
        o_ref[...] = _group_norm_out(y_scr[...].T, tok_scr[0], tok_scr[1], p).astype(o_ref.dtype)


def _rwkv_step(rw, shift, s_t, wts):
    nb = rw.shape[0]
    consts = [wts[n] for n in RW_PARAM_NAMES]
    const2 = lambda h: (0, 0)
    return pl.pallas_call(
        _rwkv_step_kernel,
        grid=(HEADS,),
        in_specs=[pl.BlockSpec((nb, RW_COLS), const2), pl.BlockSpec((nb, RW_COLS), const2),
                  pl.BlockSpec((1, HS, HS, nb), lambda h: (h, 0, 0, 0))]
        + [pl.BlockSpec(a.shape, const2) for a in consts],
        out_specs=[pl.BlockSpec((nb, RW_W), const2), pl.BlockSpec((1, HS, HS, nb), lambda h: (h, 0, 0, 0))],
        out_shape=[jax.ShapeDtypeStruct((nb, RW_W), BF16), jax.ShapeDtypeStruct(s_t.shape, F32)],
        scratch_shapes=[pltpu.VMEM((6, RW_W, nb), F32), pltpu.VMEM((RW_W, nb), F32), pltpu.VMEM((2, nb, RW_W), F32)],
        compiler_params=_params(("arbitrary",)),
        name="rwkv_step",
    )(rw, shift, s_t, *consts)


def _mlp_kernel(x_ref, om_ref, or_ref, gs_ref, wom_ref, wor_ref, wout_ref, gffn_ref, wup_ref, wdn_ref, gfin_ref,
                y_ref):
    a = _dot(om_ref[...], wom_ref[...])
    r = _dot(or_ref[...], wor_ref[...])
    mix = gs_ref[:, 0:D_MODEL] * a + gs_ref[:, D_MODEL:2 * D_MODEL] * r
    x1 = x_ref[...] + _dot(mix.astype(BF16), wout_ref[...])
    h2 = _rms(x1, gffn_ref[...]).astype(BF16)
    u = jnp.maximum(_dot(h2, wup_ref[...]), 0.0)
    x2 = x1 + _dot((u * u).astype(BF16), wdn_ref[...])
    y_ref[...] = _rms(x2, gfin_ref[...])


def _mlp(x, om, orw, gsig, wts, tm):
    rows = x.shape[0]
    row = lambda i: (i, 0)
    const = lambda i: (0, 0)
    full = lambda a: pl.BlockSpec(a.shape, const, pipeline_mode=pl.Buffered(1))
    names = ['w_o_mla', 'w_o_rwkv', 'w_out', 'g_ffn', 'w_up', 'w_down', 'g_final']
    return pl.pallas_call(
        _mlp_kernel,
        grid=(rows // tm,),
        in_specs=[pl.BlockSpec((tm, D_MODEL), row), pl.BlockSpec((tm, HEADS * VDIM), row),
                  pl.BlockSpec((tm, RW_W), row), pl.BlockSpec((tm, 2 * D_MODEL), row)]
        + [full(wts[n]) for n in names],
        out_specs=pl.BlockSpec((tm, D_MODEL), row),
        out_shape=jax.ShapeDtypeStruct((rows, D_MODEL), F32),
        compiler_params=_params(("parallel",)),
        name="mlp",
    )(x, om, orw, gsig, *[wts[n] for n in names])


def _swap_halves(w):
    return jnp.concatenate([w[..., ROPE // 2:], w[..., :ROPE // 2]], axis=-1)


def _prep_weights(g_final, g_mix, w_in, g_q, w_uq, g_kv, w_uk, w_uv, w_o_mla, mu_shift, w0, w2, a0, a2, g2,
                  k_k, k_a, r_k, ln_w, ln_b, w_o_rwkv, w_out, g_ffn, w_up, w_down):
    row = lambda a: a.reshape(1, -1).astype(F32)
    q_in = w_in[:, 0:Q_LORA]
    kv_in = w_in[:, Q_LORA:Q_LORA + KV_LORA]
    kr = w_in[:, Q_LORA + KV_LORA:Q_LORA + KV_LORA + ROPE]
    o_rw = Q_LORA + KV_LORA + ROPE
    rw = w_in[:, o_rw:o_rw + RW_COLS]
    gate = w_in[:, o_rw + RW_COLS:]
    zr = jnp.zeros((D_MODEL, ROPE), F32)
    kr_a = jnp.concatenate([kr, zr, kr, zr], axis=1)
    krs = _swap_halves(kr)
    kr_b = jnp.concatenate([krs, zr, krs, zr], axis=1)
    w_in_p = jnp.concatenate([gate, rw, q_in, kv_in, kr_a, kr_b], axis=1).astype(BF16)

    uq = w_uq.reshape(Q_LORA, HEADS, NOPE + ROPE)
    zq = jnp.zeros((Q_LORA, HEADS, HP - NOPE - ROPE), F32)
    uq_pad = jnp.concatenate([uq, zq], axis=-1).reshape(Q_LORA, HEADS * HP)
    uq_sw = jnp.concatenate([jnp.zeros((Q_LORA, HEADS, NOPE), F32), _swap_halves(uq[..., NOPE:]), zq], axis=-1)
    w_uq_p = jnp.concatenate([uq_pad, uq_sw.reshape(Q_LORA, HEADS * HP)], axis=1).astype(BF16)

    uk = w_uk.reshape(KV_LORA, HEADS, NOPE)
    uk_pad = jnp.concatenate([uk, jnp.zeros((KV_LORA, HEADS, HP - NOPE), F32)], axis=-1).reshape(KV_LORA, HEADS * HP)
    w_ukv = uk_pad.astype(BF16)

    head = jnp.arange(RW_W) // HS
    seg = (head[:, None] == head[None, :]).astype(BF16)
    return dict(
        g_mix=row(g_mix), w_in=w_in_p, g_q=row(g_q), w_uq=w_uq_p, g_kv=row(g_kv), w_ukv=w_ukv,
        w_uk=w_uk.astype(BF16), w_uv=w_uv.astype(BF16), w_uvt=w_uv.T.astype(BF16), w_o_mla=w_o_mla.astype(BF16),
        mu_shift=row(mu_shift), w0=row(w0), w2=w2.astype(BF16), a0=row(a0), a2=a2.astype(BF16),
        g2=g2.astype(BF16), k_k=row(k_k), k_a=row(k_a), r_k=row(r_k), ln_w=row(ln_w), ln_b=row(ln_b), seg=seg,
        w_o_rwkv=w_o_rwkv.astype(BF16), w_out=w_out.astype(BF16), g_ffn=row(g_ffn),
        w_up=w_up.astype(BF16), w_down=w_down.astype(BF16), g_final=row(g_final))


def _rope_tables(pos):
    inv = ROPE_BASE ** (-jnp.arange(0, ROPE, 2, dtype=F32) / ROPE)
    ang = pos.astype(F32)[:, None] * inv[None, :]
    cos, sin = jnp.cos(ang), jnp.sin(ang)
    n = pos.shape[0]
    cc = jnp.concatenate([cos, cos], axis=1)
    ss = jnp.concatenate([-sin, sin], axis=1)
    z32 = jnp.zeros((n, ROPE), F32)
    ctq = jnp.concatenate([jnp.ones((n, NOPE), F32), cc, z32], axis=1)
    stq = jnp.concatenate([jnp.zeros((n, NOPE), F32), ss, z32], axis=1)
    ctk = jnp.concatenate([cc, z32, cc, z32], axis=1)
    stk = jnp.concatenate([ss, z32, ss, z32], axis=1)
    return ctq, stq, ctk, stk


def _row_tile(rows, cap):
    tm = min(rows, cap)
    assert rows % tm == 0
    return tm


def kernel(x_prompt, x_sample, cache_kv, page_table, state_wkv, state_shift, meta_tokens, g_final, g_mix, w_in, g_q, w_uq, g_kv, w_uk, w_uv, w_o_mla, mu_shift, w0, w2, a0, a2, g2, k_k, k_a, r_k, ln_w, ln_b, w_o_rwkv, w_out, g_ffn, w_up, w_down):
    depth = g_mix.shape[0]
    assert depth == 1
    bp, seq, _ = x_prompt.shape
    bd, s_s, _ = x_sample.shape
    assert s_s == 1
    n_pages = page_table.shape[1]
    past_len = n_pages * PAGE
    wts = _prep_weights(g_final, g_mix[0], w_in[0], g_q[0], w_uq[0], g_kv[0], w_uk[0], w_uv[0], w_o_mla[0],
                        mu_shift[0], w0[0], w2[0], a0[0], a2[0], g2[0], k_k[0], k_a[0], r_k[0], ln_w[0], ln_b[0],
                        w_o_rwkv[0], w_out[0], g_ffn[0], w_up[0], w_down[0])

    tab_m = _rope_tables(jnp.arange(N_META))
    _, rw_m, _, kvrow_m, kpad_m, vt_m = _proj(meta_tokens.astype(F32), tab_m, wts, N_META, BF16)
    rw_m_pad = jnp.concatenate([rw_m, jnp.zeros((HS - N_META, RW_COLS), F32)], axis=0)
    zero_shift = jnp.zeros((1, 1, RW_COLS), F32)
    zero_state = jnp.zeros((1, HEADS, HS, HS), F32)
    _, s_meta = _rwkv(rw_m_pad, zero_shift, zero_state, wts, 1, N_META)
    shift_meta = rw_m[N_META - 1:N_META].reshape(1, 1, RW_COLS)

    rows_p = bp * seq
    tm = _row_tile(seq, 256)
    xp = x_prompt.reshape(rows_p, D_MODEL)
    tab_p = _rope_tables(N_META + jnp.arange(seq))
    gsig, rw, q, kvrow, kpad, vt = _proj(xp, tab_p, wts, tm, BF16)
    o_mla = _attn(q, kpad, vt, kpad_m, vt_m, bp, seq, tm)
    o_rwkv, s_p = _rwkv(rw, shift_meta, s_meta, wts, bp, None)
    y_prompt = _mlp(xp, o_mla, o_rwkv, gsig, wts, tm).reshape(bp, seq, D_MODEL)
    kv_prompt = jnp.concatenate([jnp.broadcast_to(kvrow_m[None], (bp, N_META, KV_W)),
                                 kvrow.reshape(bp, seq, KV_W)], axis=1)[None]
    shift_prompt = rw.reshape(bp, seq, RW_COLS)[:, -1][None]

    xs = x_sample.reshape(bd, D_MODEL)
    tab_s = _rope_tables(jnp.full((bd,), past_len, jnp.int32))
    gsig_s, rw_s, q_s, kvrow_s, _, _ = _proj(xs, tab_s, wts, bd, F32)
    qf = jnp.transpose(_qlat(q_s, wts['w_uk']), (1, 0, 2))
    cache = jnp.swapaxes(cache_kv.reshape(cache_kv.shape[1], PAGE, KV_W), 1, 2)
    o_mla_s = _sattn(page_table, qf, kvrow_s.reshape(bd, 1, KV_W), wts['w_uv'], cache).reshape(bd, HEADS * VDIM)
    s_t = jnp.transpose(state_wkv[0], (1, 2, 3, 0))
    o_rwkv_s, s_t_new = _rwkv_step(rw_s, state_shift[0], s_t, wts)
    s_s_new = jnp.transpose(s_t_new, (3, 0, 1, 2))
    y_sample = _mlp(xs, o_mla_s, o_rwkv_s, gsig_s, wts, bd).reshape(bd, 1, D_MODEL)

    return (y_prompt, y_sample,
            kv_prompt.astype(cache_kv.dtype), s_p[None].astype(state_wkv.dtype),
            shift_prompt.astype(state_shift.dtype),
            kvrow_s.reshape(1, bd, 1, KV_W).astype(cache_kv.dtype), s_s_new[None].astype(state_wkv.dtype),
            rw_s[None].astype(state_shift.dtype))
```

```python
import functools

import jax
import jax.numpy as jnp
from jax import lax
from jax.experimental import pallas as pl
from jax.experimental.pallas import tpu as pltpu

F32 = jnp.float32
BF16 = jnp.bfloat16

D_MODEL = 1024
N_META = 16
PAGE = 128
HEADS = 8
NOPE = 64
ROPE = 32
VDIM = 64
Q_LORA = 384
KV_LORA = 256
KV_W = KV_LORA + ROPE
ROPE_BASE = 10000.0
SM_SCALE = (NOPE + ROPE) ** -0.5
HS = 64
RW_W = HEADS * HS
W_LORA = 64
A_LORA = 64
G_LORA = 128
RW_COLS = 3 * RW_W + W_LORA + A_LORA + G_LORA
GN_EPS = 64e-5
D_FF = 4 * D_MODEL
NORM_EPS = 1e-6
NEG_INF = -1e30
DECAY_SCALE = 0.6065306597126334
Q_PRESCALE = SM_SCALE * 1.4426950408889634

LANE = 128
HP = 128
VMEM_LIMIT = 56 * 1024 * 1024

C_GATE = 0
C_RW = 2 * D_MODEL
C_QIN = C_RW + RW_COLS
C_KVIN = C_QIN + Q_LORA
C_KR = C_KVIN + KV_LORA
IN_PERM = C_KR + 2 * LANE


def _dot(a, b):
    return jnp.dot(a, b, preferred_element_type=F32)


def _dot_nt(a, b):
    return lax.dot_general(a, b, (((1,), (1,)), ((), ())), preferred_element_type=F32)


def _dot_tn(a, b):
    return lax.dot_general(a, b, (((0,), (0,)), ((), ())), preferred_element_type=F32)


def _split2(x):
    hi = x.astype(BF16)
    lo = (x - hi.astype(F32)).astype(BF16)
    return hi, lo


def _split3(x):
    x1 = x.astype(BF16)
    r1 = x - x1.astype(F32)
    x2 = r1.astype(BF16)
    x3 = (r1 - x2.astype(F32)).astype(BF16)
    return x1, x2, x3


def _rms(x, g):
    return x * lax.rsqrt(jnp.mean(x * x, axis=-1, keepdims=True) + NORM_EPS) * g


def _params(sem):
    return pltpu.CompilerParams(dimension_semantics=sem, vmem_limit_bytes=VMEM_LIMIT)


def _proj_kernel(x_ref, gmix_ref, win_ref, gq_ref, wuq_ref, gkv_ref, wukv_ref, wuvt_ref,
                 ctq_ref, stq_ref, ctk_ref, stk_ref,
                 gsig_ref, rw_ref, q_ref, kvrow_ref, kpad_ref, vt_ref):
    h = _rms(x_ref[...], gmix_ref[...]).astype(BF16)
    step = 512
    for c in range(0, 2 * D_MODEL, step):
        gsig_ref[:, c:c + step] = jax.nn.sigmoid(_dot(h, win_ref[:, C_GATE + c:C_GATE + c + step])).astype(BF16)
    for c in range(0, RW_COLS, 256):
        rw_ref[:, c:c + 256] = _dot(h, win_ref[:, C_RW + c:C_RW + c + 256])

    qn = _rms(_dot(h, win_ref[:, C_QIN:C_QIN + Q_LORA]), gq_ref[...]).astype(BF16)
    ctq = ctq_ref[...]
    stq = stq_ref[...]
    qa = _dot(qn, wuq_ref[:, 0:HEADS * HP])
    qb = _dot(qn, wuq_ref[:, HEADS * HP:2 * HEADS * HP])
    for hh in range(HEADS):
        hs = slice(hh * HP, (hh + 1) * HP)
        q_ref[:, hs] = (qa[:, hs] * ctq + qb[:, hs] * stq).astype(q_ref.dtype)

    ckv = _rms(_dot(h, win_ref[:, C_KVIN:C_KVIN + KV_LORA]), gkv_ref[...])
    kvrow_ref[:, 0:KV_LORA] = ckv
    zkr = _dot(h, win_ref[:, C_KR:C_KR + 2 * LANE])
    kr = zkr[:, 0:LANE] * ctk_ref[...] + zkr[:, LANE:2 * LANE] * stk_ref[...]
    kvrow_ref[:, KV_LORA:KV_W] = kr[:, 0:ROPE]
    lane = lax.broadcasted_iota(jnp.int32, kr.shape, 1)
    kr_head = jnp.where((lane >= NOPE) & (lane < NOPE + ROPE), kr, 0.0)
    cb = ckv.astype(BF16)
    kn = _dot(cb, wukv_ref[...])
    for hh in range(HEADS):
        hs = slice(hh * HP, (hh + 1) * HP)
        kpad_ref[:, hs] = (kn[:, hs] + kr_head).astype(BF16)
    vt_ref[0] = _dot_nt(wuvt_ref[...], cb).astype(BF16)


def _proj(x, tables, wts, tm, q_dtype):
    rows = x.shape[0]
    nt = tables[0].shape[0] // tm
    row = lambda i: (i, 0)
    const = lambda i: (0, 0)
    tab = lambda i: (i % nt, 0)
    full = lambda a: pl.BlockSpec(a.shape, const)
    in_specs = [pl.BlockSpec((tm, D_MODEL), row), full(wts['g_mix']), full(wts['w_in']), full(wts['g_q']),
                full(wts['w_uq']), full(wts['g_kv']), full(wts['w_ukv']), full(wts['w_uvt'])]
    in_specs += [pl.BlockSpec((tm, LANE), tab)] * 4
    widths = [(2 * D_MODEL, BF16), (RW_COLS, F32), (HEADS * HP, q_dtype), (KV_W, F32), (HEADS * HP, BF16)]
    nt_rows = rows // tm
    return pl.pallas_call(
        _proj_kernel,
        grid=(nt_rows,),
        in_specs=in_specs,
        out_specs=[pl.BlockSpec((tm, w), row) for w, _ in widths]
        + [pl.BlockSpec((1, HEADS * VDIM, tm), lambda i: (i, 0, 0))],
        out_shape=[jax.ShapeDtypeStruct((rows, w), dt) for w, dt in widths]
        + [jax.ShapeDtypeStruct((nt_rows, HEADS * VDIM, tm), BF16)],
        compiler_params=_params(("parallel",)),
        name="proj",
    )(x, wts['g_mix'], wts['w_in'], wts['g_q'], wts['w_uq'], wts['g_kv'], wts['w_ukv'], wts['w_uvt'], *tables)


def _attn_kernel(q_ref, k_ref, vt_ref, km_ref, vmt_ref, eye_ref, o_ref, *, tq):
    i = pl.program_id(1)
    qs = [q_ref[:, hh * HP:(hh + 1) * HP] for hh in range(HEADS)]

    def update(carry, k_tiles, vt_tiles, mask):
        ss = [_dot_nt(k_tiles[hh], qs[hh]) for hh in range(HEADS)]
        ps, stats = [], []
        for hh in range(HEADS):
            m, l, _ = carry[hh]
            s = ss[hh]
            if mask is not None:
                s = jnp.where(mask, s, NEG_INF)
            m_new = jnp.maximum(m, jnp.max(s, axis=0, keepdims=True))
            alpha = jnp.exp2(m - m_new)
            p = jnp.exp2(s - m_new)
            stats.append((m_new, l * alpha + jnp.sum(p, axis=0, keepdims=True), alpha))
            ps.append(p.astype(BF16))
        pv = [_dot(vt_tiles[hh], ps[hh]) for hh in range(HEADS)]
        return tuple((stats[hh][0], stats[hh][1], carry[hh][2] * stats[hh][2] + pv[hh]) for hh in range(HEADS))

    init = (jnp.full((1, tq), NEG_INF, F32), jnp.zeros((1, tq), F32), jnp.zeros((VDIM, tq), F32))
    carry = update((init,) * HEADS, [km_ref[:, hh * HP:(hh + 1) * HP] for hh in range(HEADS)],
                   [vmt_ref[0, hh * VDIM:(hh + 1) * VDIM, :] for hh in range(HEADS)], None)

    def tile(j, carry, mask):
        rows = pl.ds(pl.multiple_of(j * tq, tq), tq)
        return update(carry, [k_ref[rows, hh * HP:(hh + 1) * HP] for hh in range(HEADS)],
                      [vt_ref[j, hh * VDIM:(hh + 1) * VDIM, :] for hh in range(HEADS)], mask)

    carry = lax.fori_loop(0, i, lambda j, c: tile(j, c, None), carry)
    key = lax.broadcasted_iota(jnp.int32, (tq, tq), 0)
    qry = lax.broadcasted_iota(jnp.int32, (tq, tq), 1)
    carry = tile(i, carry, key <= qry)

    outs = []
    for hh in range(HEADS):
        m, l, acc = carry[hh]
        outs.append(_dot_tn((acc / l).astype(BF16), eye_ref[...]))
    o_ref[...] = jnp.concatenate(outs, axis=-1).astype(o_ref.dtype)


def _attn(q, kpad, vt, km, vmt, batch, seq, tq):
    nq = seq // tq
    eye = jnp.eye(VDIM, dtype=BF16)
    kern = functools.partial(_attn_kernel, tq=tq)
    return pl.pallas_call(
        kern,
        grid=(batch, nq),
        in_specs=[
            pl.BlockSpec((tq, HEADS * HP), lambda b, i: (b * nq + i, 0)),
            pl.BlockSpec((seq, HEADS * HP), lambda b, i: (b, 0)),
            pl.BlockSpec((nq, HEADS * VDIM, tq), lambda b, i: (b, 0, 0)),
            pl.BlockSpec(km.shape, lambda b, i: (0, 0)),
            pl.BlockSpec(vmt.shape, lambda b, i: (0, 0, 0)),
            pl.BlockSpec(eye.shape, lambda b, i: (0, 0)),
        ],
        out_specs=pl.BlockSpec((tq, HEADS * VDIM), lambda b, i: (b * nq + i, 0)),
        out_shape=jax.ShapeDtypeStruct((batch * seq, HEADS * VDIM), BF16),
        compiler_params=_params(("parallel", "arbitrary")),
        name="attn",
    )(q, kpad, vt, km, vmt, eye)


def _qlat_kernel(q_ref, wuk_ref, o_ref):
    for hh in range(HEADS):
        qn = q_ref[:, hh * HP:hh * HP + NOPE].astype(BF16)
        o_ref[hh, :, 0:KV_LORA] = _dot_nt(qn, wuk_ref[:, hh * NOPE:(hh + 1) * NOPE]) * SM_SCALE
        o_ref[hh, :, KV_LORA:KV_W] = q_ref[:, hh * HP + NOPE:hh * HP + NOPE + ROPE] * SM_SCALE


def _qlat(q, wuk):
    rows = q.shape[0]
    return pl.pallas_call(
        _qlat_kernel,
        out_shape=jax.ShapeDtypeStruct((HEADS, rows, KV_W), F32),
        name="qlat",
    )(q, wuk)


SATTN_SLOTS = 3


def _sattn_kernel(pt_ref, qf_ref, kvn_ref, wuv_ref, cache_ref, o_ref, buf, sem, *, n_pages, chunk):
    b = pl.program_id(0)
    nb = pl.num_programs(0)

    def page_copy(bb, slot, j):
        return pltpu.make_async_copy(cache_ref.at[pt_ref[bb, j]], buf.at[slot, :, pl.ds(j * PAGE, PAGE)], sem.at[slot])

    def fetch(bb, slot):
        def body(jj, c):
            page_copy(bb, slot, 2 * jj).start(priority=0)
            page_copy(bb, slot, 2 * jj + 1).start(priority=1)
            return c
        lax.fori_loop(0, n_pages // 2, body, 0)

    @pl.when(b == 0)
    def _():
        for ahead in range(SATTN_SLOTS - 1):
            @pl.when(ahead < nb)
            def _():
                fetch(ahead, ahead)

    @pl.when(b + SATTN_SLOTS - 1 < nb)
    def _():
        fetch(b + SATTN_SLOTS - 1, (b + SATTN_SLOTS - 1) % SATTN_SLOTS)

    slot = b % SATTN_SLOTS

    def wait_body(j, c):
        page_copy(b, slot, j).wait()
        return c
    lax.fori_loop(0, n_pages, wait_body, 0)

    qf = qf_ref[0].astype(BF16)
    m = jnp.full((HEADS, 1), NEG_INF, F32)
    l = jnp.zeros((HEADS, 1), F32)
    acc = jnp.zeros((HEADS, KV_LORA), F32)
    for c0 in range(0, n_pages * PAGE, chunk):
        kt = buf[slot, :, c0:c0 + chunk].astype(BF16)
        s = _dot(qf, kt)
        m_new = jnp.maximum(m, jnp.max(s, axis=-1, keepdims=True))
        alpha = jnp.exp(m - m_new)
        p = jnp.exp(s - m_new)
        l = l * alpha + jnp.sum(p, axis=-1, keepdims=True)
        acc = acc * alpha + _dot_nt(p.astype(BF16), kt[0:KV_LORA, :])
        m = m_new
    kvn = kvn_ref[0].astype(BF16).astype(F32)
    s = jnp.sum(qf.astype(F32) * kvn, axis=-1, keepdims=True)
    m_new = jnp.maximum(m, s)
    alpha = jnp.exp(m - m_new)
    p = jnp.exp(s - m_new)
    l = l * alpha + p
    acc = acc * alpha + p.astype(BF16).astype(F32) * kvn[:, 0:KV_LORA]
    o_lat = (acc / l).astype(BF16)
    full = _dot(o_lat, wuv_ref[...])
    r = lax.broadcasted_iota(jnp.int32, full.shape, 0)
    cidx = lax.broadcasted_iota(jnp.int32, full.shape, 1)
    o_ref[0] = jnp.sum(jnp.where(cidx // VDIM == r, full, 0.0), axis=0, keepdims=True).astype(o_ref.dtype)


def _sattn(page_table, qf, kvn, wuv, cache):
    nb, n_pages = page_table.shape
    assert n_pages % 2 == 0
    chunk = min(2048, n_pages * PAGE)
    kern = functools.partial(_sattn_kernel, n_pages=n_pages, chunk=chunk)
    grid_spec = pltpu.PrefetchScalarGridSpec(
        num_scalar_prefetch=1,
        grid=(nb,),
        in_specs=[
            pl.BlockSpec((1, HEADS, KV_W), lambda b, pt: (b, 0, 0)),
            pl.BlockSpec((1, 1, KV_W), lambda b, pt: (b, 0, 0)),
            pl.BlockSpec(wuv.shape, lambda b, pt: (0, 0)),
            pl.BlockSpec(memory_space=pl.ANY),
        ],
        out_specs=pl.BlockSpec((1, 1, HEADS * VDIM), lambda b, pt: (b, 0, 0)),
        scratch_shapes=[pltpu.VMEM((SATTN_SLOTS, KV_W, n_pages * PAGE), F32),
                        pltpu.SemaphoreType.DMA((SATTN_SLOTS,))],
    )
    return pl.pallas_call(
        kern,
        grid_spec=grid_spec,
        out_shape=jax.ShapeDtypeStruct((nb, 1, HEADS * VDIM), BF16),
        compiler_params=_params(("arbitrary",)),
        name="sattn",
    )(page_table, qf, kvn, wuv, cache)


GROUP = 4
RWKV_BATCHES_PER_STEP = 4
GW = GROUP * HS
RW_PARAM_NAMES = ['mu_shift', 'w0', 'w2', 'a0', 'a2', 'g2', 'k_k', 'k_a', 'r_k', 'ln_w', 'ln_b', 'seg']


def _seg_sum(x, seg):
    return _dot(x.astype(BF16), seg)


def _rwkv_token_prep(cur, prev, p):
    z = cur + (prev - cur) * p['mu_shift']
    r = z[:, 0:RW_W]
    k = z[:, RW_W:2 * RW_W]
    v = z[:, 2 * RW_W:3 * RW_W]
    w_in = z[:, 3 * RW_W:3 * RW_W + W_LORA]
    a_in = z[:, 3 * RW_W + W_LORA:3 * RW_W + W_LORA + A_LORA]
    g_in = z[:, 3 * RW_W + W_LORA + A_LORA:RW_COLS]
    w_log = p['w0'] + _dot(jnp.tanh(w_in).astype(BF16), p['w2'])
    logw = -DECAY_SCALE * jax.nn.sigmoid(w_log)
    a_sig = jax.nn.sigmoid(p['a0'] + _dot(a_in.astype(BF16), p['a2']))
    g = _dot(jax.nn.sigmoid(g_in).astype(BF16), p['g2'])
    kk = k * p['k_k']
    kk = kk * lax.rsqrt(jnp.maximum(_seg_sum(kk * kk, p['seg']), 1e-24))
    k_mod = k * (1.0 + (a_sig - 1.0) * p['k_a'])
    bonus = _seg_sum(r * k_mod * p['r_k'], p['seg']) * v
    return r, k_mod, v, logw, a_sig, g, kk, bonus


def _group_norm_out(y, bonus, g, p):
    mean = _seg_sum(y, p['seg']) * (1.0 / HS)
    d = y - mean
    var = _seg_sum(d * d, p['seg']) * (1.0 / HS)
    yn = d * lax.rsqrt(var + GN_EPS) * p['ln_w'] + p['ln_b']
    return (yn + bonus) * g


def _rwkv_kernel(rw_ref, shift_ref, m0_ref, *rest, n_valid, nbat):
    prm_refs = rest[:len(RW_PARAM_NAMES)]
    tri_ref, ones_ref, o_ref, mout_ref, prev_scr, m_scr = rest[len(RW_PARAM_NAMES):]
    p = {n: ref[...] for n, ref in zip(RW_PARAM_NAMES, prm_refs)}
    chunk = HS
    ngrp = HEADS // GROUP
    c = pl.program_id(1)
    nc = pl.num_programs(1)

    @pl.when(c == 0)
    def _():
        for bi in range(nbat):
            prev_scr[bi] = shift_ref[0]
            m_scr[bi] = m0_ref[0]

    curs, prevs = [], []
    for bi in range(nbat):
        cur_b = rw_ref[bi]
        rowi = lax.broadcasted_iota(jnp.int32, cur_b.shape, 0)
        prevs.append(jnp.where(rowi == 0, prev_scr[bi], pltpu.roll(cur_b, 1, 0)))
        prev_scr[bi] = cur_b[chunk - 1:chunk, :]
        curs.append(cur_b)
    r, k_mod, v, logw, a_sig, g, kk, bonus = _rwkv_token_prep(
        jnp.concatenate(curs, axis=0), jnp.concatenate(prevs, axis=0), p)

    if n_valid is not None:
        step = lax.broadcasted_iota(jnp.int32, (nbat * chunk, RW_W), 0) % chunk
        live = (c * chunk + step) < n_valid
        logw = jnp.where(live, logw, 0.0)
        kk = jnp.where(live, kk, 0.0)
        k_mod = jnp.where(live, k_mod, 0.0)
        v = jnp.where(live, v, 0.0)

    lw3 = _split3(logw)
    seq_rows = [slice(bi * chunk, (bi + 1) * chunk) for bi in range(nbat)]
    gcum = jnp.concatenate([sum(_dot(tri_ref[...], t[rs]) for t in lw3) for rs in seq_rows], axis=0)
    g_end = jnp.concatenate([jnp.broadcast_to(gcum[rs][chunk - 1:chunk, :], (chunk, RW_W)) for rs in seq_rows], axis=0)
    r_t = r * jnp.exp(gcum)
    a_t = -kk * jnp.exp(gcum - logw)
    eng = jnp.exp(-gcum)
    b_t = kk * a_sig * eng
    k_t = k_mod * eng
    tail = jnp.exp(g_end - gcum)
    b_e = kk * a_sig * tail
    k_e = k_mod * tail
    decay_end = jnp.exp(g_end)

    ti = lax.broadcasted_iota(jnp.int32, (chunk, GW), 0)
    lane = lax.broadcasted_iota(jnp.int32, (chunk, GW), 1)
    si = lane % HS
    lane_head = lane // HS
    strict = si < ti
    incl = si <= ti
    eye = (si == ti).astype(F32)

    def bdr(zf):
        zb = zf.astype(BF16)
        return jnp.concatenate([jnp.where(lane_head == hh, zb, jnp.zeros_like(zb)) for hh in range(GROUP)], axis=0)

    streams = [(bi, gi) for bi in range(nbat) for gi in range(ngrp)]
    blk = lambda t, s: t[s[0] * chunk:(s[0] + 1) * chunk, s[1] * GW:(s[1] + 1) * GW]
    cat0 = lambda xs: jnp.concatenate(xs, axis=0).astype(BF16)
    cat1 = lambda xs: jnp.concatenate(xs, axis=1).astype(BF16)

    ar = [cat0([blk(a_t, s), blk(r_t, s)]) for s in streams]
    ab = [_dot_nt(ar[i], bdr(blk(b_t, s))) for i, s in enumerate(streams)]
    ak = [_dot_nt(ar[i], bdr(blk(k_t, s))) for i, s in enumerate(streams)]
    n = [jnp.where(strict, t[0:chunk], 0.0) for t in ab]
    a_rb = [jnp.where(incl, t[chunk:2 * chunk], 0.0) for t in ab]
    a_ak = [jnp.where(strict, t[0:chunk], 0.0) for t in ak]
    a_rk = [jnp.where(incl, t[chunk:2 * chunk], 0.0) for t in ak]

    x = [eye + t for t in n]
    pw = [_dot(t.astype(BF16), bdr(t)) for t in n]
    for _ in range(4):
        z = [_dot(cat0([xi, pi]), bdr(pi)) for xi, pi in zip(x, pw)]
        x = [xi + zi[0:chunk] for xi, zi in zip(x, z)]
        pw = [zi[chunk:2 * chunk] for zi in z]
    z = [_dot(xi.astype(BF16), bdr(pi)) for xi, pi in zip(x, pw)]
    x = [xi + zi for xi, zi in zip(x, z)]

    m0 = [m_scr[s[0], s[1]] for s in streams]
    bm = [bdr(t) for t in m0]
    bv = [bdr(blk(v, s)) for s in streams]
    rhs = [_dot(cat1([blk(a_t, s), a_ak[i]]), jnp.concatenate([bm[i], bv[i]], axis=0)) for i, s in enumerate(streams)]
    u = [_dot(xi.astype(BF16), bdr(ri)) for xi, ri in zip(x, rhs)]
    y = [_dot(cat1([blk(r_t, s), a_rb[i], a_rk[i]]), jnp.concatenate([bm[i], bdr(u[i]), bv[i]], axis=0))
         for i, s in enumerate(streams)]
    full = [_dot_tn(cat0([blk(b_e, s), blk(k_e, s)]), cat0([u[i], blk(v, s)])) for i, s in enumerate(streams)]
    gm = [sum(_dot(t, ones_ref[...]) for t in _split3(eye * blk(decay_end, s)[0:1])) for s in streams]
    for i, s in enumerate(streams):
        upd = jnp.where(lane_head == 0, full[i][0:HS], 0.0)
        for hh in range(1, GROUP):
            upd = upd + jnp.where(lane_head == hh, full[i][hh * HS:(hh + 1) * HS], 0.0)
        m_scr[s[0], s[1]] = gm[i] * m0[i] + upd

    y_all = jnp.concatenate([jnp.concatenate([y[bi * ngrp + gi] for gi in range(ngrp)], axis=1)
                             for bi in range(nbat)], axis=0)
    out = _group_norm_out(y_all, bonus, g, p).astype(o_ref.dtype)
    for bi in range(nbat):
        o_ref[bi] = out[seq_rows[bi]]

    @pl.when(c == nc - 1)
    def _():
        mout_ref[...] = m_scr[...]


def _state_to_m(s):
    b = s.shape[0]
    return jnp.transpose(s.reshape(b, HEADS // GROUP, GROUP, HS, HS), (0, 1, 4, 2, 3)).reshape(b, HEADS // GROUP, HS, GW)


def _m_to_state(m):
    b = m.shape[0]
    return jnp.transpose(m.reshape(b, HEADS // GROUP, HS, GROUP, HS), (0, 1, 3, 4, 2)).reshape(b, HEADS, HS, HS)


def _rwkv(rw, shift0, s0, wts, batch, n_valid):
    chunk = HS
    rows = rw.shape[0]
    seq = rows // batch
    nc = seq // chunk
    nbat = RWKV_BATCHES_PER_STEP if batch % RWKV_BATCHES_PER_STEP == 0 else 1
    const2 = lambda b, c: (0, 0)
    consts = [wts[n] for n in RW_PARAM_NAMES]
    tri = (jnp.arange(chunk)[:, None] >= jnp.arange(chunk)[None, :]).astype(BF16)
    hd = jnp.arange(GW) // HS
    ones_bd = (hd[:, None] == hd[None, :]).astype(BF16)
    m0 = _state_to_m(s0)
    ngrp = HEADS // GROUP
    kern = functools.partial(_rwkv_kernel, n_valid=n_valid, nbat=nbat)
    o, m = pl.pallas_call(
        kern,
        grid=(batch // nbat, nc),
        in_specs=[pl.BlockSpec((nbat, chunk, RW_COLS), lambda b, c: (b, c, 0)),
                  pl.BlockSpec((1, 1, RW_COLS), lambda b, c: (0, 0, 0)),
                  pl.BlockSpec((1, ngrp, HS, GW), lambda b, c: (0, 0, 0, 0))]
        + [pl.BlockSpec(a.shape, const2) for a in consts]
        + [pl.BlockSpec(tri.shape, const2), pl.BlockSpec(ones_bd.shape, const2)],
        out_specs=[pl.BlockSpec((nbat, chunk, RW_W), lambda b, c: (b, c, 0)),
                   pl.BlockSpec((nbat, ngrp, HS, GW), lambda b, c: (b, 0, 0, 0))],
        out_shape=[jax.ShapeDtypeStruct((batch, seq, RW_W), BF16),
                   jax.ShapeDtypeStruct((batch, ngrp, HS, GW), F32)],
        scratch_shapes=[pltpu.VMEM((nbat, 1, RW_COLS), F32), pltpu.VMEM((nbat, ngrp, HS, GW), F32)],
        compiler_params=_params(("parallel", "arbitrary")),
        name="rwkv",
    )(rw.reshape(batch, seq, RW_COLS), shift0, m0, *consts, tri, ones_bd)
    return o.reshape(rows, RW_W), _m_to_state(m)


def _rwkv_step_kernel(rw_ref, shift_ref, s_ref, *rest):
    prm_refs = rest[:len(RW_PARAM_NAMES)]
    o_ref, sout_ref, t_scr, y_scr, tok_scr = rest[len(RW_PARAM_NAMES):]
    h = pl.program_id(0)

    @pl.when(h == 0)
    def _():
        p = {n: ref[...] for n, ref in zip(RW_PARAM_NAMES, prm_refs)}
        r, k_mod, v, logw, a_sig, g, kk, bonus = _rwkv_token_prep(rw_ref[...], shift_ref[...], p)
        t_scr[0] = r.T
        t_scr[1] = jnp.exp(logw).T
        t_scr[2] = k_mod.T
        t_scr[3] = v.T
        t_scr[4] = (-kk).T
        t_scr[5] = (kk * a_sig).T
        tok_scr[0] = bonus
        tok_scr[1] = g

    rows = pl.ds(pl.multiple_of(h * HS, HS), HS)
    r_h, w_h, k_h = t_scr[0, rows, :], t_scr[1, rows, :], t_scr[2, rows, :]
    a_h, b_h = t_scr[4, rows, :], t_scr[5, rows, :]

    def body(vv, carry):
        sv = s_ref[0, vv]
        vrow = t_scr[3, pl.ds(h * HS + vv, 1), :]
        sa = jnp.sum(sv * a_h, axis=0, keepdims=True)
        sn = sv * w_h + sa * b_h + vrow * k_h
        sout_ref[0, vv] = sn
        y_scr[pl.ds(h * HS + vv, 1), :] = jnp.sum(sn * r_h, axis=0, keepdims=True)
        return carry
    lax.fori_loop(0, HS, body, 0)

    @pl.when(h == HEADS - 1)
    def _():
        p = {n: ref[...] for n, ref in zip(RW_PARAM_NAMES, prm_refs)}
        o_ref[...] = _group_norm_out(y_scr[...].T, tok_scr[0], tok_scr[1], p).astype(o_ref.dtype)


def _rwkv_step(rw, shift, s_t, wts):
    nb = rw.shape[0]
    consts = [wts[n] for n in RW_PARAM_NAMES]
    const2 = lambda h: (0, 0)
    return pl.pallas_call(
        _rwkv_step_kernel,
        grid=(HEADS,),
        in_specs=[pl.BlockSpec((nb, RW_COLS), const2), pl.BlockSpec((nb, RW_COLS), const2),
                  pl.BlockSpec((1, HS, HS, nb), lambda h: (h, 0, 0, 0))]
        + [pl.BlockSpec(a.shape, const2) for a in consts],
        out_specs=[pl.BlockSpec((nb, RW_W), const2), pl.BlockSpec((1, HS, HS, nb), lambda h: (h, 0, 0, 0))],
        out_shape=[jax.ShapeDtypeStruct((nb, RW_W), BF16), jax.ShapeDtypeStruct(s_t.shape, F32)],
        scratch_shapes=[pltpu.VMEM((6, RW_W, nb), F32), pltpu.VMEM((RW_W, nb), F32), pltpu.VMEM((2, nb, RW_W), F32)],
        compiler_params=_params(("arbitrary",)),
        name="rwkv_step",
    )(rw, shift, s_t, *consts)


def _mlp_kernel(x_ref, om_ref, or_ref, gs_ref, wom_ref, wor_ref, wout_ref, gffn_ref, wup_ref, wdn_ref, gfin_ref,
                y_ref):
    a = _dot(om_ref[...], wom_ref[...])
    r = _dot(or_ref[...], wor_ref[...])
    mix = gs_ref[:, 0:D_MODEL] * a + gs_ref[:, D_MODEL:2 * D_MODEL] * r
    x1 = x_ref[...] + _dot(mix.astype(BF16), wout_ref[...])
    h2 = _rms(x1, gffn_ref[...]).astype(BF16)
    u = jnp.maximum(_dot(h2, wup_ref[...]), 0.0)
    x2 = x1 + _dot((u * u).astype(BF16), wdn_ref[...])
    y_ref[...] = _rms(x2, gfin_ref[...])


def _mlp(x, om, orw, gsig, wts, tm):
    rows = x.shape[0]
    row = lambda i: (i, 0)
    const = lambda i: (0, 0)
    full = lambda a: pl.BlockSpec(a.shape, const, pipeline_mode=pl.Buffered(1))
    names = ['w_o_mla', 'w_o_rwkv', 'w_out', 'g_ffn', 'w_up', 'w_down', 'g_final']
    return pl.pallas_call(
        _mlp_kernel,
        grid=(rows // tm,),
        in_specs=[pl.BlockSpec((tm, D_MODEL), row), pl.BlockSpec((tm, HEADS * VDIM), row),
                  pl.BlockSpec((tm, RW_W), row), pl.BlockSpec((tm, 2 * D_MODEL), row)]
        + [full(wts[n]) for n in names],
        out_specs=pl.BlockSpec((tm, D_MODEL), row),
        out_shape=jax.ShapeDtypeStruct((rows, D_MODEL), F32),
        compiler_params=_params(("parallel",)),
        name="mlp",
    )(x, om, orw, gsig, *[wts[n] for n in names])


def _swap_halves(w):
    return jnp.concatenate([w[..., ROPE // 2:], w[..., :ROPE // 2]], axis=-1)


def _prep_weights(g_final, g_mix, w_in, g_q, w_uq, g_kv, w_uk, w_uv, w_o_mla, mu_shift, w0, w2, a0, a2, g2,
                  k_k, k_a, r_k, ln_w, ln_b, w_o_rwkv, w_out, g_ffn, w_up, w_down):
    row = lambda a: a.reshape(1, -1).astype(F32)
    q_in = w_in[:, 0:Q_LORA]
    kv_in = w_in[:, Q_LORA:Q_LORA + KV_LORA]
    kr = w_in[:, Q_LORA + KV_LORA:Q_LORA + KV_LORA + ROPE]
    o_rw = Q_LORA + KV_LORA + ROPE
    rw = w_in[:, o_rw:o_rw + RW_COLS]
    gate = w_in[:, o_rw + RW_COLS:]
    zr = jnp.zeros((D_MODEL, ROPE), F32)
    kr_a = jnp.concatenate([kr, zr, kr, zr], axis=1)
    krs = _swap_halves(kr)
    kr_b = jnp.concatenate([krs, zr, krs, zr], axis=1)
    w_in_p = jnp.concatenate([gate, rw, q_in, kv_in, kr_a, kr_b], axis=1).astype(BF16)

    uq = w_uq.reshape(Q_LORA, HEADS, NOPE + ROPE)
    zq = jnp.zeros((Q_LORA, HEADS, HP - NOPE - ROPE), F32)
    uq_pad = jnp.concatenate([uq, zq], axis=-1).reshape(Q_LORA, HEADS * HP)
    uq_sw = jnp.concatenate([jnp.zeros((Q_LORA, HEADS, NOPE), F32), _swap_halves(uq[..., NOPE:]), zq], axis=-1)
    w_uq_p = jnp.concatenate([uq_pad, uq_sw.reshape(Q_LORA, HEADS * HP)], axis=1).astype(BF16)

    uk = w_uk.reshape(KV_LORA, HEADS, NOPE)
    uk_pad = jnp.concatenate([uk, jnp.zeros((KV_LORA, HEADS, HP - NOPE), F32)], axis=-1).reshape(KV_LORA, HEADS * HP)
    w_ukv = uk_pad.astype(BF16)

    head = jnp.arange(RW_W) // HS
    seg = (head[:, None] == head[None, :]).astype(BF16)
    return dict(
        g_mix=row(g_mix), w_in=w_in_p, g_q=row(g_q), w_uq=w_uq_p, g_kv=row(g_kv), w_ukv=w_ukv,
        w_uk=w_uk.astype(BF16), w_uv=w_uv.astype(BF16), w_uvt=w_uv.T.astype(BF16), w_o_mla=w_o_mla.astype(BF16),
        mu_shift=row(mu_shift), w0=row(w0), w2=w2.astype(BF16), a0=row(a0), a2=a2.astype(BF16),
        g2=g2.astype(BF16), k_k=row(k_k), k_a=row(k_a), r_k=row(r_k), ln_w=row(ln_w), ln_b=row(ln_b), seg=seg,
        w_o_rwkv=w_o_rwkv.astype(BF16), w_out=w_out.astype(BF16), g_ffn=row(g_ffn),
        w_up=w_up.astype(BF16), w_down=w_down.astype(BF16), g_final=row(g_final))


def _rope_tables(pos):
    inv = ROPE_BASE ** (-jnp.arange(0, ROPE, 2, dtype=F32) / ROPE)
    ang = pos.astype(F32)[:, None] * inv[None, :]
    cos, sin = jnp.cos(ang), jnp.sin(ang)
    n = pos.shape[0]
    cc = jnp.concatenate([cos, cos], axis=1)
    ss = jnp.concatenate([-sin, sin], axis=1)
    z32 = jnp.zeros((n, ROPE), F32)
    ctq = jnp.concatenate([jnp.ones((n, NOPE), F32), cc, z32], axis=1)
    stq = jnp.concatenate([jnp.zeros((n, NOPE), F32), ss, z32], axis=1)
    ctk = jnp.concatenate([cc, z32, cc, z32], axis=1)
    stk = jnp.concatenate([ss, z32, ss, z32], axis=1)
    return ctq, stq, ctk, stk


def _row_tile(rows, cap):
    tm = min(rows, cap)
    assert rows % tm == 0
    return tm


def kernel(x_prompt, x_sample, cache_kv, page_table, state_wkv, state_shift, meta_tokens, g_final, g_mix, w_in, g_q, w_uq, g_kv, w_uk, w_uv, w_o_mla, mu_shift, w0, w2, a0, a2, g2, k_k, k_a, r_k, ln_w, ln_b, w_o_rwkv, w_out, g_ffn, w_up, w_down):
    depth = g_mix.shape[0]
    assert depth == 1
    bp, seq, _ = x_prompt.shape
    bd, s_s, _ = x_sample.shape
    assert s_s == 1
    n_pages = page_table.shape[1]
    past_len = n_pages * PAGE
    wts = _prep_weights(g_final, g_mix[0], w_in[0], g_q[0], w_uq[0], g_kv[0], w_uk[0], w_uv[0], w_o_mla[0],
                        mu_shift[0], w0[0], w2[0], a0[0], a2[0], g2[0], k_k[0], k_a[0], r_k[0], ln_w[0], ln_b[0],
                        w_o_rwkv[0], w_out[0], g_ffn[0], w_up[0], w_down[0])

    tab_m = _rope_tables(jnp.arange(N_META))
    _, rw_m, _, kvrow_m, kpad_m, vt_m = _proj(meta_tokens.astype(F32), tab_m, wts, N_META, BF16)
    rw_m_pad = jnp.concatenate([rw_m, jnp.zeros((HS - N_META, RW_COLS), F32)], axis=0)
    zero_shift = jnp.zeros((1, 1, RW_COLS), F32)
    zero_state = jnp.zeros((1, HEADS, HS, HS), F32)
    _, s_meta = _rwkv(rw_m_pad, zero_shift, zero_state, wts, 1, N_META)
    shift_meta = rw_m[N_META - 1:N_META].reshape(1, 1, RW_COLS)

    rows_p = bp * seq
    tm = _row_tile(seq, 256)
    xp = x_prompt.reshape(rows_p, D_MODEL)
    ctq, stq, ctk, stk = _rope_tables(N_META + jnp.arange(seq))
    tab_p = (ctq * Q_PRESCALE, stq * Q_PRESCALE, ctk, stk)
    gsig, rw, q, kvrow, kpad, vt = _proj(xp, tab_p, wts, tm, BF16)
    o_mla = _attn(q, kpad, vt, kpad_m, vt_m, bp, seq, tm)
    o_rwkv, s_p = _rwkv(rw, shift_meta, s_meta, wts, bp, None)
    y_prompt = _mlp(xp, o_mla, o_rwkv, gsig, wts, tm).reshape(bp, seq, D_MODEL)
    kv_prompt = jnp.concatenate([jnp.broadcast_to(kvrow_m[None], (bp, N_META, KV_W)),
                                 kvrow.reshape(bp, seq, KV_W)], axis=1)[None]
    shift_prompt = rw.reshape(bp, seq, RW_COLS)[:, -1][None]

    xs = x_sample.reshape(bd, D_MODEL)
    tab_s = _rope_tables(jnp.full((bd,), past_len, jnp.int32))
    gsig_s, rw_s, q_s, kvrow_s, _, _ = _proj(xs, tab_s, wts, bd, F32)
    qf = jnp.transpose(_qlat(q_s, wts['w_uk']), (1, 0, 2))
    cache = jnp.swapaxes(cache_kv.reshape(cache_kv.shape[1], PAGE, KV_W), 1, 2)
    o_mla_s = _sattn(page_table, qf, kvrow_s.reshape(bd, 1, KV_W), wts['w_uv'], cache).reshape(bd, HEADS * VDIM)
    s_t = jnp.transpose(state_wkv[0], (1, 2, 3, 0))
    o_rwkv_s, s_t_new = _rwkv_step(rw_s, state_shift[0], s_t, wts)
    s_s_new = jnp.transpose(s_t_new, (3, 0, 1, 2))
    y_sample = _mlp(xs, o_mla_s, o_rwkv_s, gsig_s, wts, bd).reshape(bd, 1, D_MODEL)

    return (y_prompt, y_sample,
            kv_prompt.astype(cache_kv.dtype), s_p[None].astype(state_wkv.dtype),
            shift_prompt.astype(state_shift.dtype),
            kvrow_s.reshape(1, bd, 1, KV_W).astype(cache_kv.dtype), s_s_new[None].astype(state_wkv.dtype),
            rw_s[None].astype(state_shift.dtype))
```

```python
import functools

import jax
import jax.numpy as jnp
from jax import lax
from jax.experimental import pallas as pl
from jax.experimental.pallas import tpu as pltpu

F32 = jnp.float32
BF16 = jnp.bfloat16

D_MODEL = 1024
N_META = 16
PAGE = 128
HEADS = 8
NOPE = 64
ROPE = 32
VDIM = 64
Q_LORA = 384
KV_LORA = 256
KV_W = KV_LORA + ROPE
ROPE_BASE = 10000.0
SM_SCALE = (NOPE + ROPE) ** -0.5
HS = 64
RW_W = HEADS * HS
W_LORA = 64
A_LORA = 64
G_LORA = 128
RW_COLS = 3 * RW_W + W_LORA + A_LORA + G_LORA
GN_EPS = 64e-5
D_FF = 4 * D_MODEL
NORM_EPS = 1e-6
NEG_INF = -1e30
DECAY_SCALE = 0.6065306597126334
Q_PRESCALE = SM_SCALE * 1.4426950408889634

LANE = 128
HP = 128
VMEM_LIMIT = 56 * 1024 * 1024
KV_TILE = 256
ROW_TILE = 512

C_GATE = 0
C_RW = 2 * D_MODEL
C_QIN = C_RW + RW_COLS
C_KVIN = C_QIN + Q_LORA
C_KR = C_KVIN + KV_LORA
IN_PERM = C_KR + 2 * LANE


def _dot(a, b):
    return jnp.dot(a, b, preferred_element_type=F32)


def _dot_nt(a, b):
    return lax.dot_general(a, b, (((1,), (1,)), ((), ())), preferred_element_type=F32)


def _dot_tn(a, b):
    return lax.dot_general(a, b, (((0,), (0,)), ((), ())), preferred_element_type=F32)


def _split2(x):
    hi = x.astype(BF16)
    lo = (x - hi.astype(F32)).astype(BF16)
    return hi, lo


def _split3(x):
    x1 = x.astype(BF16)
    r1 = x - x1.astype(F32)
    x2 = r1.astype(BF16)
    x3 = (r1 - x2.astype(F32)).astype(BF16)
    return x1, x2, x3


def _rms(x, g):
    return x * lax.rsqrt(jnp.mean(x * x, axis=-1, keepdims=True) + NORM_EPS) * g


def _params(sem):
    return pltpu.CompilerParams(dimension_semantics=sem, vmem_limit_bytes=VMEM_LIMIT)


def _proj_kernel(x_ref, gmix_ref, win_ref, gq_ref, wuq_ref, gkv_ref, wukv_ref, wuvt_ref,
                 ctq_ref, stq_ref, ctk_ref, stk_ref,
                 gsig_ref, rw_ref, q_ref, kvrow_ref, kpad_ref, vt_ref):
    h = _rms(x_ref[...], gmix_ref[...]).astype(BF16)
    step = 512
    for c in range(0, 2 * D_MODEL, step):
        gsig_ref[:, c:c + step] = jax.nn.sigmoid(_dot(h, win_ref[:, C_GATE + c:C_GATE + c + step])).astype(BF16)
    for c in range(0, RW_COLS, 256):
        rw_ref[:, c:c + 256] = _dot(h, win_ref[:, C_RW + c:C_RW + c + 256])

    qn = _rms(_dot(h, win_ref[:, C_QIN:C_QIN + Q_LORA]), gq_ref[...]).astype(BF16)
    ctq = ctq_ref[...]
    stq = stq_ref[...]
    qa = _dot(qn, wuq_ref[:, 0:HEADS * HP])
    qb = _dot(qn, wuq_ref[:, HEADS * HP:2 * HEADS * HP])
    for hh in range(HEADS):
        hs = slice(hh * HP, (hh + 1) * HP)
        q_ref[:, hs] = (qa[:, hs] * ctq + qb[:, hs] * stq).astype(q_ref.dtype)

    ckv = _rms(_dot(h, win_ref[:, C_KVIN:C_KVIN + KV_LORA]), gkv_ref[...])
    kvrow_ref[:, 0:KV_LORA] = ckv
    zkr = _dot(h, win_ref[:, C_KR:C_KR + 2 * LANE])
    kr = zkr[:, 0:LANE] * ctk_ref[...] + zkr[:, LANE:2 * LANE] * stk_ref[...]
    kvrow_ref[:, KV_LORA:KV_W] = kr[:, 0:ROPE]
    lane = lax.broadcasted_iota(jnp.int32, kr.shape, 1)
    kr_head = jnp.where((lane >= NOPE) & (lane < NOPE + ROPE), kr, 0.0)
    cb = ckv.astype(BF16)
    kn = _dot(cb, wukv_ref[...])
    for hh in range(HEADS):
        hs = slice(hh * HP, (hh + 1) * HP)
        kpad_ref[:, hs] = (kn[:, hs] + kr_head).astype(BF16)
    tw = vt_ref.shape[2]
    for sub in range(vt_ref.shape[0]):
        vt_ref[sub] = _dot_nt(wuvt_ref[...], cb[sub * tw:(sub + 1) * tw]).astype(BF16)


def _proj(x, tables, wts, tm, q_dtype):
    rows = x.shape[0]
    nt = tables[0].shape[0] // tm
    tw = min(tm, KV_TILE)
    row = lambda i: (i, 0)
    const = lambda i: (0, 0)
    tab = lambda i: (i % nt, 0)
    full = lambda a: pl.BlockSpec(a.shape, const, pipeline_mode=pl.Buffered(1))
    in_specs = [pl.BlockSpec((tm, D_MODEL), row), full(wts['g_mix']), full(wts['w_in']), full(wts['g_q']),
                full(wts['w_uq']), full(wts['g_kv']), full(wts['w_ukv']), full(wts['w_uvt'])]
    in_specs += [pl.BlockSpec((tm, LANE), tab)] * 4
    widths = [(2 * D_MODEL, BF16), (RW_COLS, F32), (HEADS * HP, q_dtype), (KV_W, F32), (HEADS * HP, BF16)]
    nt_rows = rows // tm
    return pl.pallas_call(
        _proj_kernel,
        grid=(nt_rows,),
        in_specs=in_specs,
        out_specs=[pl.BlockSpec((tm, w), row) for w, _ in widths]
        + [pl.BlockSpec((tm // tw, HEADS * VDIM, tw), lambda i: (i, 0, 0))],
        out_shape=[jax.ShapeDtypeStruct((rows, w), dt) for w, dt in widths]
        + [jax.ShapeDtypeStruct((rows // tw, HEADS * VDIM, tw), BF16)],
        compiler_params=_params(("parallel",)),
        name="proj",
    )(x, wts['g_mix'], wts['w_in'], wts['g_q'], wts['w_uq'], wts['g_kv'], wts['w_ukv'], wts['w_uvt'], *tables)


def _attn_kernel(q_ref, k_ref, vt_ref, km_ref, vmt_ref, eye_ref, o_ref, *, tq):
    i = pl.program_id(1)
    qs = [q_ref[:, hh * HP:(hh + 1) * HP] for hh in range(HEADS)]

    def update(carry, k_tiles, vt_tiles, mask):
        ss = [_dot_nt(k_tiles[hh], qs[hh]) for hh in range(HEADS)]
        ps, stats = [], []
        for hh in range(HEADS):
            m, l, _ = carry[hh]
            s = ss[hh]
            if mask is not None:
                s = jnp.where(mask, s, NEG_INF)
            m_new = jnp.maximum(m, jnp.max(s, axis=0, keepdims=True))
            alpha = jnp.exp2(m - m_new)
            p = jnp.exp2(s - m_new)
            stats.append((m_new, l * alpha + jnp.sum(p, axis=0, keepdims=True), alpha))
            ps.append(p.astype(BF16))
        pv = [_dot(vt_tiles[hh], ps[hh]) for hh in range(HEADS)]
        return tuple((stats[hh][0], stats[hh][1], carry[hh][2] * stats[hh][2] + pv[hh]) for hh in range(HEADS))

    init = (jnp.full((1, tq), NEG_INF, F32), jnp.zeros((1, tq), F32), jnp.zeros((VDIM, tq), F32))
    carry = update((init,) * HEADS, [km_ref[:, hh * HP:(hh + 1) * HP] for hh in range(HEADS)],
                   [vmt_ref[0, hh * VDIM:(hh + 1) * VDIM, :] for hh in range(HEADS)], None)

    def tile(j, carry, mask):
        rows = pl.ds(pl.multiple_of(j * tq, tq), tq)
        return update(carry, [k_ref[rows, hh * HP:(hh + 1) * HP] for hh in range(HEADS)],
                      [vt_ref[j, hh * VDIM:(hh + 1) * VDIM, :] for hh in range(HEADS)], mask)

    carry = lax.fori_loop(0, i, lambda j, c: tile(j, c, None), carry)
    key = lax.broadcasted_iota(jnp.int32, (tq, tq), 0)
    qry = lax.broadcasted_iota(jnp.int32, (tq, tq), 1)
    carry = tile(i, carry, key <= qry)

    outs = []
    for hh in range(HEADS):
        m, l, acc = carry[hh]
        outs.append(_dot_tn((acc / l).astype(BF16), eye_ref[...]))
    o_ref[...] = jnp.concatenate(outs, axis=-1).astype(o_ref.dtype)


def _attn(q, kpad, vt, km, vmt, batch, seq, tq):
    nq = seq // tq
    eye = jnp.eye(VDIM, dtype=BF16)
    kern = functools.partial(_attn_kernel, tq=tq)
    return pl.pallas_call(
        kern,
        grid=(batch, nq),
        in_specs=[
            pl.BlockSpec((tq, HEADS * HP), lambda b, i: (b * nq + i, 0)),
            pl.BlockSpec((seq, HEADS * HP), lambda b, i: (b, 0)),
            pl.BlockSpec((nq, HEADS * VDIM, tq), lambda b, i: (b, 0, 0)),
            pl.BlockSpec(km.shape, lambda b, i: (0, 0)),
            pl.BlockSpec(vmt.shape, lambda b, i: (0, 0, 0)),
            pl.BlockSpec(eye.shape, lambda b, i: (0, 0)),
        ],
        out_specs=pl.BlockSpec((tq, HEADS * VDIM), lambda b, i: (b * nq + i, 0)),
        out_shape=jax.ShapeDtypeStruct((batch * seq, HEADS * VDIM), BF16),
        compiler_params=_params(("parallel", "arbitrary")),
        name="attn",
    )(q, kpad, vt, km, vmt, eye)


def _qlat_kernel(q_ref, wuk_ref, o_ref):
    for hh in range(HEADS):
        qn = q_ref[:, hh * HP:hh * HP + NOPE].astype(BF16)
        o_ref[hh, :, 0:KV_LORA] = _dot_nt(qn, wuk_ref[:, hh * NOPE:(hh + 1) * NOPE]) * SM_SCALE
        o_ref[hh, :, KV_LORA:KV_W] = q_ref[:, hh * HP + NOPE:hh * HP + NOPE + ROPE] * SM_SCALE


def _qlat(q, wuk):
    rows = q.shape[0]
    return pl.pallas_call(
        _qlat_kernel,
        out_shape=jax.ShapeDtypeStruct((HEADS, rows, KV_W), F32),
        name="qlat",
    )(q, wuk)


SATTN_SLOTS = 3


def _sattn_kernel(pt_ref, qf_ref, kvn_ref, wuv_ref, cache_ref, o_ref, buf, sem, *, n_pages, chunk):
    b = pl.program_id(0)
    nb = pl.num_programs(0)

    def page_copy(bb, slot, j):
        return pltpu.make_async_copy(cache_ref.at[pt_ref[bb, j]], buf.at[slot, :, pl.ds(j * PAGE, PAGE)], sem.at[slot])

    def fetch(bb, slot):
        def body(jj, c):
            page_copy(bb, slot, 2 * jj).start(priority=0)
            page_copy(bb, slot, 2 * jj + 1).start(priority=1)
            return c
        lax.fori_loop(0, n_pages // 2, body, 0)

    @pl.when(b == 0)
    def _():
        for ahead in range(SATTN_SLOTS - 1):
            @pl.when(ahead < nb)
            def _():
                fetch(ahead, ahead)

    @pl.when(b + SATTN_SLOTS - 1 < nb)
    def _():
        fetch(b + SATTN_SLOTS - 1, (b + SATTN_SLOTS - 1) % SATTN_SLOTS)

    slot = b % SATTN_SLOTS

    def wait_body(j, c):
        page_copy(b, slot, j).wait()
        return c
    lax.fori_loop(0, n_pages, wait_body, 0)

    qf = qf_ref[0].astype(BF16)
    kvn = kvn_ref[0].astype(BF16).astype(F32)
    s_own = jnp.sum(qf.astype(F32) * kvn, axis=-1, keepdims=True)
    kts = [buf[slot, :, c0:c0 + chunk].astype(BF16) for c0 in range(0, n_pages * PAGE, chunk)]
    ss = [_dot(qf, kt) for kt in kts]
    m = s_own
    for sc in ss:
        m = jnp.maximum(m, jnp.max(sc, axis=-1, keepdims=True))
    ps = [jnp.exp(sc - m) for sc in ss]
    p_own = jnp.exp(s_own - m)
    l = p_own
    for pc in ps:
        l = l + jnp.sum(pc, axis=-1, keepdims=True)
    acc = p_own.astype(BF16).astype(F32) * kvn[:, 0:KV_LORA]
    for pc, kt in zip(ps, kts):
        acc = acc + _dot_nt(pc.astype(BF16), kt[0:KV_LORA, :])
    o_lat = (acc / l).astype(BF16)
    full = _dot(o_lat, wuv_ref[...])
    r = lax.broadcasted_iota(jnp.int32, full.shape, 0)
    cidx = lax.broadcasted_iota(jnp.int32, full.shape, 1)
    o_ref[0] = jnp.sum(jnp.where(cidx // VDIM == r, full, 0.0), axis=0, keepdims=True).astype(o_ref.dtype)


def _sattn(page_table, qf, kvn, wuv, cache):
    nb, n_pages = page_table.shape
    assert n_pages % 2 == 0
    chunk = min(2048, n_pages * PAGE)
    kern = functools.partial(_sattn_kernel, n_pages=n_pages, chunk=chunk)
    grid_spec = pltpu.PrefetchScalarGridSpec(
        num_scalar_prefetch=1,
        grid=(nb,),
        in_specs=[
            pl.BlockSpec((1, HEADS, KV_W), lambda b, pt: (b, 0, 0)),
            pl.BlockSpec((1, 1, KV_W), lambda b, pt: (b, 0, 0)),
            pl.BlockSpec(wuv.shape, lambda b, pt: (0, 0)),
            pl.BlockSpec(memory_space=pl.ANY),
        ],
        out_specs=pl.BlockSpec((1, 1, HEADS * VDIM), lambda b, pt: (b, 0, 0)),
        scratch_shapes=[pltpu.VMEM((SATTN_SLOTS, KV_W, n_pages * PAGE), F32),
                        pltpu.SemaphoreType.DMA((SATTN_SLOTS,))],
    )
    return pl.pallas_call(
        kern,
        grid_spec=grid_spec,
        out_shape=jax.ShapeDtypeStruct((nb, 1, HEADS * VDIM), BF16),
        compiler_params=_params(("arbitrary",)),
        name="sattn",
    )(page_table, qf, kvn, wuv, cache)


GROUP = 4
RWKV_BATCHES_PER_STEP = 4
GW = GROUP * HS
RW_PARAM_NAMES = ['mu_shift', 'w0', 'w2', 'a0', 'a2', 'g2', 'k_k', 'k_a', 'r_k', 'ln_w', 'ln_b', 'seg']


def _seg_sum(x, seg):
    return _dot(x.astype(BF16), seg)


def _rwkv_token_prep(cur, prev, p):
    z = cur + (prev - cur) * p['mu_shift']
    r = z[:, 0:RW_W]
    k = z[:, RW_W:2 * RW_W]
    v = z[:, 2 * RW_W:3 * RW_W]
    w_in = z[:, 3 * RW_W:3 * RW_W + W_LORA]
    a_in = z[:, 3 * RW_W + W_LORA:3 * RW_W + W_LORA + A_LORA]
    g_in = z[:, 3 * RW_W + W_LORA + A_LORA:RW_COLS]
    w_log = p['w0'] + _dot(jnp.tanh(w_in).astype(BF16), p['w2'])
    logw = -DECAY_SCALE * jax.nn.sigmoid(w_log)
    a_sig = jax.nn.sigmoid(p['a0'] + _dot(a_in.astype(BF16), p['a2']))
    g = _dot(jax.nn.sigmoid(g_in).astype(BF16), p['g2'])
    kk = k * p['k_k']
    kk = kk * lax.rsqrt(jnp.maximum(_seg_sum(kk * kk, p['seg']), 1e-24))
    k_mod = k * (1.0 + (a_sig - 1.0) * p['k_a'])
    bonus = _seg_sum(r * k_mod * p['r_k'], p['seg']) * v
    return r, k_mod, v, logw, a_sig, g, kk, bonus


def _group_norm_out(y, bonus, g, p):
    mean = _seg_sum(y, p['seg']) * (1.0 / HS)
    d = y - mean
    var = _seg_sum(d * d, p['seg']) * (1.0 / HS)
    yn = d * lax.rsqrt(var + GN_EPS) * p['ln_w'] + p['ln_b']
    return (yn + bonus) * g


def _rwkv_kernel(rw_ref, shift_ref, m0_ref, *rest, n_valid, nbat):
    prm_refs = rest[:len(RW_PARAM_NAMES)]
    tri_ref, ones_ref, o_ref, mout_ref, prev_scr, m_scr = rest[len(RW_PARAM_NAMES):]
    p = {n: ref[...] for n, ref in zip(RW_PARAM_NAMES, prm_refs)}
    chunk = HS
    ngrp = HEADS // GROUP
    c = pl.program_id(1)
    nc = pl.num_programs(1)

    @pl.when(c == 0)
    def _():
        for bi in range(nbat):
            prev_scr[bi] = shift_ref[0]
            m_scr[bi] = m0_ref[0]

    curs, prevs = [], []
    for bi in range(nbat):
        cur_b = rw_ref[bi]
        rowi = lax.broadcasted_iota(jnp.int32, cur_b.shape, 0)
        prevs.append(jnp.where(rowi == 0, prev_scr[bi], pltpu.roll(cur_b, 1, 0)))
        prev_scr[bi] = cur_b[chunk - 1:chunk, :]
        curs.append(cur_b)
    r, k_mod, v, logw, a_sig, g, kk, bonus = _rwkv_token_prep(
        jnp.concatenate(curs, axis=0), jnp.concatenate(prevs, axis=0), p)

    if n_valid is not None:
        step = lax.broadcasted_iota(jnp.int32, (nbat * chunk, RW_W), 0) % chunk
        live = (c * chunk + step) < n_valid
        logw = jnp.where(live, logw, 0.0)
        kk = jnp.where(live, kk, 0.0)
        k_mod = jnp.where(live, k_mod, 0.0)
        v = jnp.where(live, v, 0.0)

    lw3 = _split3(logw)
    seq_rows = [slice(bi * chunk, (bi + 1) * chunk) for bi in range(nbat)]
    gcum = jnp.concatenate([sum(_dot(tri_ref[...], t[rs]) for t in lw3) for rs in seq_rows], axis=0)
    g_end = jnp.concatenate([jnp.broadcast_to(gcum[rs][chunk - 1:chunk, :], (chunk, RW_W)) for rs in seq_rows], axis=0)
    r_t = r * jnp.exp(gcum)
    a_t = -kk * jnp.exp(gcum - logw)
    eng = jnp.exp(-gcum)
    b_t = kk * a_sig * eng
    k_t = k_mod * eng
    tail = jnp.exp(g_end - gcum)
    b_e = kk * a_sig * tail
    k_e = k_mod * tail
    decay_end = jnp.exp(g_end)

    ti = lax.broadcasted_iota(jnp.int32, (chunk, GW), 0)
    lane = lax.broadcasted_iota(jnp.int32, (chunk, GW), 1)
    si = lane % HS
    lane_head = lane // HS
    strict = si < ti
    incl = si <= ti
    eye = (si == ti).astype(F32)

    def bdr(zf):
        zb = zf.astype(BF16)
        return jnp.concatenate([jnp.where(lane_head == hh, zb, jnp.zeros_like(zb)) for hh in range(GROUP)], axis=0)

    streams = [(bi, gi) for bi in range(nbat) for gi in range(ngrp)]
    blk = lambda t, s: t[s[0] * chunk:(s[0] + 1) * chunk, s[1] * GW:(s[1] + 1) * GW]
    cat0 = lambda xs: jnp.concatenate(xs, axis=0).astype(BF16)
    cat1 = lambda xs: jnp.concatenate(xs, axis=1).astype(BF16)

    ar = [cat0([blk(a_t, s), blk(r_t, s)]) for s in streams]
    ab = [_dot_nt(ar[i], bdr(blk(b_t, s))) for i, s in enumerate(streams)]
    ak = [_dot_nt(ar[i], bdr(blk(k_t, s))) for i, s in enumerate(streams)]
    n = [jnp.where(strict, t[0:chunk], 0.0) for t in ab]
    a_rb = [jnp.where(incl, t[chunk:2 * chunk], 0.0) for t in ab]
    a_ak = [jnp.where(strict, t[0:chunk], 0.0) for t in ak]
    a_rk = [jnp.where(incl, t[chunk:2 * chunk], 0.0) for t in ak]

    x = [eye + t for t in n]
    pw = [_dot(t.astype(BF16), bdr(t)) for t in n]
    for _ in range(4):
        z = [_dot(cat0([xi, pi]), bdr(pi)) for xi, pi in zip(x, pw)]
        x = [xi + zi[0:chunk] for xi, zi in zip(x, z)]
        pw = [zi[chunk:2 * chunk] for zi in z]
    z = [_dot(xi.astype(BF16), bdr(pi)) for xi, pi in zip(x, pw)]
    x = [xi + zi for xi, zi in zip(x, z)]

    m0 = [m_scr[s[0], s[1]] for s in streams]
    bm = [bdr(t) for t in m0]
    bv = [bdr(blk(v, s)) for s in streams]
    rhs = [_dot(cat1([blk(a_t, s), a_ak[i]]), jnp.concatenate([bm[i], bv[i]], axis=0)) for i, s in enumerate(streams)]
    u = [_dot(xi.astype(BF16), bdr(ri)) for xi, ri in zip(x, rhs)]
    y = [_dot(cat1([blk(r_t, s), a_rb[i], a_rk[i]]), jnp.concatenate([bm[i], bdr(u[i]), bv[i]], axis=0))
         for i, s in enumerate(streams)]
    full = [_dot_tn(cat0([blk(b_e, s), blk(k_e, s)]), cat0([u[i], blk(v, s)])) for i, s in enumerate(streams)]
    gm = [sum(_dot(t, ones_ref[...]) for t in _split3(eye * blk(decay_end, s)[0:1])) for s in streams]
    for i, s in enumerate(streams):
        upd = jnp.where(lane_head == 0, full[i][0:HS], 0.0)
        for hh in range(1, GROUP):
            upd = upd + jnp.where(lane_head == hh, full[i][hh * HS:(hh + 1) * HS], 0.0)
        m_scr[s[0], s[1]] = gm[i] * m0[i] + upd

    y_all = jnp.concatenate([jnp.concatenate([y[bi * ngrp + gi] for gi in range(ngrp)], axis=1)
                             for bi in range(nbat)], axis=0)
    out = _group_norm_out(y_all, bonus, g, p).astype(o_ref.dtype)
    for bi in range(nbat):
        o_ref[bi] = out[seq_rows[bi]]

    @pl.when(c == nc - 1)
    def _():
        mout_ref[...] = m_scr[...]


def _state_to_m(s):
    b = s.shape[0]
    return jnp.transpose(s.reshape(b, HEADS // GROUP, GROUP, HS, HS), (0, 1, 4, 2, 3)).reshape(b, HEADS // GROUP, HS, GW)


def _m_to_state(m):
    b = m.shape[0]
    return jnp.transpose(m.reshape(b, HEADS // GROUP, HS, GROUP, HS), (0, 1, 3, 4, 2)).reshape(b, HEADS, HS, HS)


def _rwkv(rw, shift0, s0, wts, batch, n_valid):
    chunk = HS
    rows = rw.shape[0]
    seq = rows // batch
    nc = seq // chunk
    nbat = RWKV_BATCHES_PER_STEP if batch % RWKV_BATCHES_PER_STEP == 0 else 1
    const2 = lambda b, c: (0, 0)
    consts = [wts[n] for n in RW_PARAM_NAMES]
    tri = (jnp.arange(chunk)[:, None] >= jnp.arange(chunk)[None, :]).astype(BF16)
    hd = jnp.arange(GW) // HS
    ones_bd = (hd[:, None] == hd[None, :]).astype(BF16)
    m0 = _state_to_m(s0)
    ngrp = HEADS // GROUP
    kern = functools.partial(_rwkv_kernel, n_valid=n_valid, nbat=nbat)
    o, m = pl.pallas_call(
        kern,
        grid=(batch // nbat, nc),
        in_specs=[pl.BlockSpec((nbat, chunk, RW_COLS), lambda b, c: (b, c, 0)),
                  pl.BlockSpec((1, 1, RW_COLS), lambda b, c: (0, 0, 0)),
                  pl.BlockSpec((1, ngrp, HS, GW), lambda b, c: (0, 0, 0, 0))]
        + [pl.BlockSpec(a.shape, const2) for a in consts]
        + [pl.BlockSpec(tri.shape, const2), pl.BlockSpec(ones_bd.shape, const2)],
        out_specs=[pl.BlockSpec((nbat, chunk, RW_W), lambda b, c: (b, c, 0)),
                   pl.BlockSpec((nbat, ngrp, HS, GW), lambda b, c: (b, 0, 0, 0))],
        out_shape=[jax.ShapeDtypeStruct((batch, seq, RW_W), BF16),
                   jax.ShapeDtypeStruct((batch, ngrp, HS, GW), F32)],
        scratch_shapes=[pltpu.VMEM((nbat, 1, RW_COLS), F32), pltpu.VMEM((nbat, ngrp, HS, GW), F32)],
        compiler_params=_params(("parallel", "arbitrary")),
        name="rwkv",
    )(rw.reshape(batch, seq, RW_COLS), shift0, m0, *consts, tri, ones_bd)
    return o.reshape(rows, RW_W), _m_to_state(m)


def _rwkv_step_kernel(rw_ref, shift_ref, s_ref, *rest):
    prm_refs = rest[:len(RW_PARAM_NAMES)]
    o_ref, sout_ref, t_scr, y_scr, tok_scr = rest[len(RW_PARAM_NAMES):]
    h = pl.program_id(0)

    @pl.when(h == 0)
    def _():
        p = {n: ref[...] for n, ref in zip(RW_PARAM_NAMES, prm_refs)}
        r, k_mod, v, logw, a_sig, g, kk, bonus = _rwkv_token_prep(rw_ref[...], shift_ref[...], p)
        t_scr[0] = r.T
        t_scr[1] = jnp.exp(logw).T
        t_scr[2] = k_mod.T
        t_scr[3] = v.T
        t_scr[4] = (-kk).T
        t_scr[5] = (kk * a_sig).T
        tok_scr[0] = bonus
        tok_scr[1] = g

    rows = pl.ds(pl.multiple_of(h * HS, HS), HS)
    r_h, w_h, k_h = t_scr[0, rows, :], t_scr[1, rows, :], t_scr[2, rows, :]
    a_h, b_h = t_scr[4, rows, :], t_scr[5, rows, :]

    def body(vv, carry):
        sv = s_ref[0, vv]
        vrow = t_scr[3, pl.ds(h * HS + vv, 1), :]
        sa = jnp.sum(sv * a_h, axis=0, keepdims=True)
        sn = sv * w_h + sa * b_h + vrow * k_h
        sout_ref[0, vv] = sn
        y_scr[pl.ds(h * HS + vv, 1), :] = jnp.sum(sn * r_h, axis=0, keepdims=True)
        return carry
    lax.fori_loop(0, HS, body, 0)

    @pl.when(h == HEADS - 1)
    def _():
        p = {n: ref[...] for n, ref in zip(RW_PARAM_NAMES, prm_refs)}
        o_ref[...] = _group_norm_out(y_scr[...].T, tok_scr[0], tok_scr[1], p).astype(o_ref.dtype)


def _rwkv_step(rw, shift, s_t, wts):
    nb = rw.shape[0]
    consts = [wts[n] for n in RW_PARAM_NAMES]
    const2 = lambda h: (0, 0)
    return pl.pallas_call(
        _rwkv_step_kernel,
        grid=(HEADS,),
        in_specs=[pl.BlockSpec((nb, RW_COLS), const2), pl.BlockSpec((nb, RW_COLS), const2),
                  pl.BlockSpec((1, HS, HS, nb), lambda h: (h, 0, 0, 0))]
        + [pl.BlockSpec(a.shape, const2) for a in consts],
        out_specs=[pl.BlockSpec((nb, RW_W), const2), pl.BlockSpec((1, HS, HS, nb), lambda h: (h, 0, 0, 0))],
        out_shape=[jax.ShapeDtypeStruct((nb, RW_W), BF16), jax.ShapeDtypeStruct(s_t.shape, F32)],
        scratch_shapes=[pltpu.VMEM((6, RW_W, nb), F32), pltpu.VMEM((RW_W, nb), F32), pltpu.VMEM((2, nb, RW_W), F32)],
        compiler_params=_params(("arbitrary",)),
        name="rwkv_step",
    )(rw, shift, s_t, *consts)


def _mlp_kernel(x_ref, om_ref, or_ref, gs_ref, wom_ref, wor_ref, wout_ref, gffn_ref, wup_ref, wdn_ref, gfin_ref,
                y_ref):
    a = _dot(om_ref[...], wom_ref[...])
    r = _dot(or_ref[...], wor_ref[...])
    mix = gs_ref[:, 0:D_MODEL] * a + gs_ref[:, D_MODEL:2 * D_MODEL] * r
    x1 = x_ref[...] + _dot(mix.astype(BF16), wout_ref[...])
    h2 = _rms(x1, gffn_ref[...]).astype(BF16)
    u = jnp.maximum(_dot(h2, wup_ref[...]), 0.0)
    x2 = x1 + _dot((u * u).astype(BF16), wdn_ref[...])
    y_ref[...] = _rms(x2, gfin_ref[...])


def _mlp(x, om, orw, gsig, wts, tm):
    rows = x.shape[0]
    row = lambda i: (i, 0)
    const = lambda i: (0, 0)
    full = lambda a: pl.BlockSpec(a.shape, const, pipeline_mode=pl.Buffered(1))
    names = ['w_o_mla', 'w_o_rwkv', 'w_out', 'g_ffn', 'w_up', 'w_down', 'g_final']
    return pl.pallas_call(
        _mlp_kernel,
        grid=(rows // tm,),
        in_specs=[pl.BlockSpec((tm, D_MODEL), row), pl.BlockSpec((tm, HEADS * VDIM), row),
                  pl.BlockSpec((tm, RW_W), row), pl.BlockSpec((tm, 2 * D_MODEL), row)]
        + [full(wts[n]) for n in names],
        out_specs=pl.BlockSpec((tm, D_MODEL), row),
        out_shape=jax.ShapeDtypeStruct((rows, D_MODEL), F32),
        compiler_params=_params(("parallel",)),
        name="mlp",
    )(x, om, orw, gsig, *[wts[n] for n in names])


def _swap_halves(w):
    return jnp.concatenate([w[..., ROPE // 2:], w[..., :ROPE // 2]], axis=-1)


def _prep_weights(g_final, g_mix, w_in, g_q, w_uq, g_kv, w_uk, w_uv, w_o_mla, mu_shift, w0, w2, a0, a2, g2,
                  k_k, k_a, r_k, ln_w, ln_b, w_o_rwkv, w_out, g_ffn, w_up, w_down):
    row = lambda a: a.reshape(1, -1).astype(F32)
    q_in = w_in[:, 0:Q_LORA]
    kv_in = w_in[:, Q_LORA:Q_LORA + KV_LORA]
    kr = w_in[:, Q_LORA + KV_LORA:Q_LORA + KV_LORA + ROPE]
    o_rw = Q_LORA + KV_LORA + ROPE
    rw = w_in[:, o_rw:o_rw + RW_COLS]
    gate = w_in[:, o_rw + RW_COLS:]
    zr = jnp.zeros((D_MODEL, ROPE), F32)
    kr_a = jnp.concatenate([kr, zr, kr, zr], axis=1)
    krs = _swap_halves(kr)
    kr_b = jnp.concatenate([krs, zr, krs, zr], axis=1)
    w_in_p = jnp.concatenate([gate, rw, q_in, kv_in, kr_a, kr_b], axis=1).astype(BF16)

    uq = w_uq.reshape(Q_LORA, HEADS, NOPE + ROPE)
    zq = jnp.zeros((Q_LORA, HEADS, HP - NOPE - ROPE), F32)
    uq_pad = jnp.concatenate([uq, zq], axis=-1).reshape(Q_LORA, HEADS * HP)
    uq_sw = jnp.concatenate([jnp.zeros((Q_LORA, HEADS, NOPE), F32), _swap_halves(uq[..., NOPE:]), zq], axis=-1)
    w_uq_p = jnp.concatenate([uq_pad, uq_sw.reshape(Q_LORA, HEADS * HP)], axis=1).astype(BF16)

    uk = w_uk.reshape(KV_LORA, HEADS, NOPE)
    uk_pad = jnp.concatenate([uk, jnp.zeros((KV_LORA, HEADS, HP - NOPE), F32)], axis=-1).reshape(KV_LORA, HEADS * HP)
    w_ukv = uk_pad.astype(BF16)

    head = jnp.arange(RW_W) // HS
    seg = (head[:, None] == head[None, :]).astype(BF16)
    return dict(
        g_mix=row(g_mix), w_in=w_in_p, g_q=row(g_q), w_uq=w_uq_p, g_kv=row(g_kv), w_ukv=w_ukv,
        w_uk=w_uk.astype(BF16), w_uv=w_uv.astype(BF16), w_uvt=w_uv.T.astype(BF16), w_o_mla=w_o_mla.astype(BF16),
        mu_shift=row(mu_shift), w0=row(w0), w2=w2.astype(BF16), a0=row(a0), a2=a2.astype(BF16),
        g2=g2.astype(BF16), k_k=row(k_k), k_a=row(k_a), r_k=row(r_k), ln_w=row(ln_w), ln_b=row(ln_b), seg=seg,
        w_o_rwkv=w_o_rwkv.astype(BF16), w_out=w_out.astype(BF16), g_ffn=row(g_ffn),
        w_up=w_up.astype(BF16), w_down=w_down.astype(BF16), g_final=row(g_final))


def _rope_tables(pos):
    inv = ROPE_BASE ** (-jnp.arange(0, ROPE, 2, dtype=F32) / ROPE)
    ang = pos.astype(F32)[:, None] * inv[None, :]
    cos, sin = jnp.cos(ang), jnp.sin(ang)
    n = pos.shape[0]
    cc = jnp.concatenate([cos, cos], axis=1)
    ss = jnp.concatenate([-sin, sin], axis=1)
    z32 = jnp.zeros((n, ROPE), F32)
    ctq = jnp.concatenate([jnp.ones((n, NOPE), F32), cc, z32], axis=1)
    stq = jnp.concatenate([jnp.zeros((n, NOPE), F32), ss, z32], axis=1)
    ctk = jnp.concatenate([cc, z32, cc, z32], axis=1)
    stk = jnp.concatenate([ss, z32, ss, z32], axis=1)
    return ctq, stq, ctk, stk


def _row_tile(rows, cap):
    tm = min(rows, cap)
    assert rows % tm == 0
    return tm


def kernel(x_prompt, x_sample, cache_kv, page_table, state_wkv, state_shift, meta_tokens, g_final, g_mix, w_in, g_q, w_uq, g_kv, w_uk, w_uv, w_o_mla, mu_shift, w0, w2, a0, a2, g2, k_k, k_a, r_k, ln_w, ln_b, w_o_rwkv, w_out, g_ffn, w_up, w_down):
    depth = g_mix.shape[0]
    assert depth == 1
    bp, seq, _ = x_prompt.shape
    bd, s_s, _ = x_sample.shape
    assert s_s == 1
    n_pages = page_table.shape[1]
    past_len = n_pages * PAGE
    wts = _prep_weights(g_final, g_mix[0], w_in[0], g_q[0], w_uq[0], g_kv[0], w_uk[0], w_uv[0], w_o_mla[0],
                        mu_shift[0], w0[0], w2[0], a0[0], a2[0], g2[0], k_k[0], k_a[0], r_k[0], ln_w[0], ln_b[0],
                        w_o_rwkv[0], w_out[0], g_ffn[0], w_up[0], w_down[0])

    tab_m = _rope_tables(jnp.arange(N_META))
    _, rw_m, _, kvrow_m, kpad_m, vt_m = _proj(meta_tokens.astype(F32), tab_m, wts, N_META, BF16)
    rw_m_pad = jnp.concatenate([rw_m, jnp.zeros((HS - N_META, RW_COLS), F32)], axis=0)
    zero_shift = jnp.zeros((1, 1, RW_COLS), F32)
    zero_state = jnp.zeros((1, HEADS, HS, HS), F32)
    _, s_meta = _rwkv(rw_m_pad, zero_shift, zero_state, wts, 1, N_META)
    shift_meta = rw_m[N_META - 1:N_META].reshape(1, 1, RW_COLS)

    rows_p = bp * seq
    tm = _row_tile(seq, ROW_TILE)
    tq = _row_tile(seq, KV_TILE)
    xp = x_prompt.reshape(rows_p, D_MODEL)
    ctq, stq, ctk, stk = _rope_tables(N_META + jnp.arange(seq))
    tab_p = (ctq * Q_PRESCALE, stq * Q_PRESCALE, ctk, stk)
    gsig, rw, q, kvrow, kpad, vt = _proj(xp, tab_p, wts, tm, BF16)
    o_mla = _attn(q, kpad, vt, kpad_m, vt_m, bp, seq, tq)
    o_rwkv, s_p = _rwkv(rw, shift_meta, s_meta, wts, bp, None)
    y_prompt = _mlp(xp, o_mla, o_rwkv, gsig, wts, tm).reshape(bp, seq, D_MODEL)
    kv_prompt = jnp.concatenate([jnp.broadcast_to(kvrow_m[None], (bp, N_META, KV_W)),
                                 kvrow.reshape(bp, seq, KV_W)], axis=1)[None]
    shift_prompt = rw.reshape(bp, seq, RW_COLS)[:, -1][None]

    xs = x_sample.reshape(bd, D_MODEL)
    tab_s = _rope_tables(jnp.full((bd,), past_len, jnp.int32))
    gsig_s, rw_s, q_s, kvrow_s, _, _ = _proj(xs, tab_s, wts, bd, F32)
    qf = jnp.transpose(_qlat(q_s, wts['w_uk']), (1, 0, 2))
    cache = jnp.swapaxes(cache_kv.reshape(cache_kv.shape[1], PAGE, KV_W), 1, 2)
    o_mla_s = _sattn(page_table, qf, kvrow_s.reshape(bd, 1, KV_W), wts['w_uv'], cache).reshape(bd, HEADS * VDIM)
    s_t = jnp.transpose(state_wkv[0], (1, 2, 3, 0))
    o_rwkv_s, s_t_new = _rwkv_step(rw_s, state_shift[0], s_t, wts)
    s_s_new = jnp.transpose(s_t_new, (3, 0, 1, 2))
    y_sample = _mlp(xs, o_mla_s, o_rwkv_s, gsig_s, wts, bd).reshape(bd, 1, D_MODEL)

    return (y_prompt, y_sample,
            kv_prompt.astype(cache_kv.dtype), s_p[None].astype(state_wkv.dtype),
            shift_prompt.astype(state_shift.dtype),
            kvrow_s.reshape(1, bd, 1, KV_W).astype(cache_kv.dtype), s_s_new[None].astype(state_wkv.dtype),
            rw_s[None].astype(state_shift.dtype))
```

```python
import functools

import jax
import jax.numpy as jnp
from jax import lax
from jax.experimental import pallas as pl
from jax.experimental.pallas import tpu as pltpu

F32 = jnp.float32
BF16 = jnp.bfloat16

D_MODEL = 1024
N_META = 16
PAGE = 128
HEADS = 8
NOPE = 64
ROPE = 32
VDIM = 64
Q_LORA = 384
KV_LORA = 256
KV_W = KV_LORA + ROPE
ROPE_BASE = 10000.0
SM_SCALE = (NOPE + ROPE) ** -0.5
HS = 64
RW_W = HEADS * HS
W_LORA = 64
A_LORA = 64
G_LORA = 128
RW_COLS = 3 * RW_W + W_LORA + A_LORA + G_LORA
GN_EPS = 64e-5
D_FF = 4 * D_MODEL
NORM_EPS = 1e-6
NEG_INF = -1e30
DECAY_SCALE = 0.6065306597126334
Q_PRESCALE = SM_SCALE * 1.4426950408889634

LANE = 128
HP = 128
VMEM_LIMIT = 56 * 1024 * 1024
KV_TILE = 256
ROW_TILE = 512

C_GATE = 0
C_RW = 2 * D_MODEL
C_QIN = C_RW + RW_COLS
C_KVIN = C_QIN + Q_LORA
C_KR = C_KVIN + KV_LORA
IN_PERM = C_KR + 2 * LANE


def _dot(a, b):
    return jnp.dot(a, b, preferred_element_type=F32)


def _dot_nt(a, b):
    return lax.dot_general(a, b, (((1,), (1,)), ((), ())), preferred_element_type=F32)


def _dot_tn(a, b):
    return lax.dot_general(a, b, (((0,), (0,)), ((), ())), preferred_element_type=F32)


def _split2(x):
    hi = x.astype(BF16)
    lo = (x - hi.astype(F32)).astype(BF16)
    return hi, lo


def _split3(x):
    x1 = x.astype(BF16)
    r1 = x - x1.astype(F32)
    x2 = r1.astype(BF16)
    x3 = (r1 - x2.astype(F32)).astype(BF16)
    return x1, x2, x3


def _rms(x, g):
    return x * lax.rsqrt(jnp.mean(x * x, axis=-1, keepdims=True) + NORM_EPS) * g


def _params(sem):
    return pltpu.CompilerParams(dimension_semantics=sem, vmem_limit_bytes=VMEM_LIMIT)


def _proj_kernel(x_ref, gmix_ref, win_ref, gq_ref, wuq_ref, gkv_ref, wukv_ref, wuvt_ref,
                 ctq_ref, stq_ref, ctk_ref, stk_ref,
                 gsig_ref, rw_ref, q_ref, kvrow_ref, kpad_ref, vt_ref):
    h = _rms(x_ref[...], gmix_ref[...]).astype(BF16)
    step = 512
    for c in range(0, 2 * D_MODEL, step):
        gsig_ref[:, c:c + step] = jax.nn.sigmoid(_dot(h, win_ref[:, C_GATE + c:C_GATE + c + step])).astype(BF16)
    for c in range(0, RW_COLS, 256):
        rw_ref[:, c:c + 256] = _dot(h, win_ref[:, C_RW + c:C_RW + c + 256])

    qn = _rms(_dot(h, win_ref[:, C_QIN:C_QIN + Q_LORA]), gq_ref[...]).astype(BF16)
    ctq = ctq_ref[...]
    stq = stq_ref[...]
    qa = _dot(qn, wuq_ref[:, 0:HEADS * HP])
    qb = _dot(qn, wuq_ref[:, HEADS * HP:2 * HEADS * HP])
    for hh in range(HEADS):
        hs = slice(hh * HP, (hh + 1) * HP)
        q_ref[:, hs] = (qa[:, hs] * ctq + qb[:, hs] * stq).astype(q_ref.dtype)

    ckv = _rms(_dot(h, win_ref[:, C_KVIN:C_KVIN + KV_LORA]), gkv_ref[...])
    kvrow_ref[:, 0:KV_LORA] = ckv
    zkr = _dot(h, win_ref[:, C_KR:C_KR + 2 * LANE])
    kr = zkr[:, 0:LANE] * ctk_ref[...] + zkr[:, LANE:2 * LANE] * stk_ref[...]
    kvrow_ref[:, KV_LORA:KV_W] = kr[:, 0:ROPE]
    lane = lax.broadcasted_iota(jnp.int32, kr.shape, 1)
    kr_head = jnp.where((lane >= NOPE) & (lane < NOPE + ROPE), kr, 0.0)
    cb = ckv.astype(BF16)
    kn = _dot(cb, wukv_ref[...])
    for hh in range(HEADS):
        hs = slice(hh * HP, (hh + 1) * HP)
        kpad_ref[:, hs] = (kn[:, hs] + kr_head).astype(BF16)
    tw = vt_ref.shape[2]
    for sub in range(vt_ref.shape[0]):
        vt_ref[sub] = _dot_nt(wuvt_ref[...], cb[sub * tw:(sub + 1) * tw]).astype(BF16)


def _proj(x, tables, wts, tm, q_dtype):
    rows = x.shape[0]
    nt = tables[0].shape[0] // tm
    tw = min(tm, KV_TILE)
    row = lambda i: (i, 0)
    const = lambda i: (0, 0)
    tab = lambda i: (i % nt, 0)
    full = lambda a: pl.BlockSpec(a.shape, const, pipeline_mode=pl.Buffered(1))
    in_specs = [pl.BlockSpec((tm, D_MODEL), row), full(wts['g_mix']), full(wts['w_in']), full(wts['g_q']),
                full(wts['w_uq']), full(wts['g_kv']), full(wts['w_ukv']), full(wts['w_uvt'])]
    in_specs += [pl.BlockSpec((tm, LANE), tab)] * 4
    widths = [(2 * D_MODEL, BF16), (RW_COLS, F32), (HEADS * HP, q_dtype), (KV_W, F32), (HEADS * HP, BF16)]
    nt_rows = rows // tm
    return pl.pallas_call(
        _proj_kernel,
        grid=(nt_rows,),
        in_specs=in_specs,
        out_specs=[pl.BlockSpec((tm, w), row) for w, _ in widths]
        + [pl.BlockSpec((tm // tw, HEADS * VDIM, tw), lambda i: (i, 0, 0))],
        out_shape=[jax.ShapeDtypeStruct((rows, w), dt) for w, dt in widths]
        + [jax.ShapeDtypeStruct((rows // tw, HEADS * VDIM, tw), BF16)],
        compiler_params=_params(("parallel",)),
        name="proj",
    )(x, wts['g_mix'], wts['w_in'], wts['g_q'], wts['w_uq'], wts['g_kv'], wts['w_ukv'], wts['w_uvt'], *tables)


def _attn_kernel(q_ref, k_ref, vt_ref, km_ref, vmt_ref, eye_ref, o_ref, *, tq):
    i = pl.program_id(1)
    qs = [q_ref[:, hh * HP:(hh + 1) * HP] for hh in range(HEADS)]

    def update(carry, k_tiles, vt_tiles, mask):
        ss = [_dot_nt(k_tiles[hh], qs[hh]) for hh in range(HEADS)]
        ps, stats = [], []
        for hh in range(HEADS):
            m, l, _ = carry[hh]
            s = ss[hh]
            if mask is not None:
                s = jnp.where(mask, s, NEG_INF)
            m_new = jnp.maximum(m, jnp.max(s, axis=0, keepdims=True))
            alpha = jnp.exp2(m - m_new)
            p = jnp.exp2(s - m_new)
            stats.append((m_new, l * alpha + jnp.sum(p, axis=0, keepdims=True), alpha))
            ps.append(p.astype(BF16))
        pv = [_dot(vt_tiles[hh], ps[hh]) for hh in range(HEADS)]
        return tuple((stats[hh][0], stats[hh][1], carry[hh][2] * stats[hh][2] + pv[hh]) for hh in range(HEADS))

    init = (jnp.full((1, tq), NEG_INF, F32), jnp.zeros((1, tq), F32), jnp.zeros((VDIM, tq), F32))
    carry = update((init,) * HEADS, [km_ref[:, hh * HP:(hh + 1) * HP] for hh in range(HEADS)],
                   [vmt_ref[0, hh * VDIM:(hh + 1) * VDIM, :] for hh in range(HEADS)], None)

    def tile(j, carry, mask):
        rows = pl.ds(pl.multiple_of(j * tq, tq), tq)
        return update(carry, [k_ref[rows, hh * HP:(hh + 1) * HP] for hh in range(HEADS)],
                      [vt_ref[j, hh * VDIM:(hh + 1) * VDIM, :] for hh in range(HEADS)], mask)

    carry = lax.fori_loop(0, i, lambda j, c: tile(j, c, None), carry)
    key = lax.broadcasted_iota(jnp.int32, (tq, tq), 0)
    qry = lax.broadcasted_iota(jnp.int32, (tq, tq), 1)
    carry = tile(i, carry, key <= qry)

    outs = []
    for hh in range(HEADS):
        m, l, acc = carry[hh]
        outs.append(_dot_tn((acc / l).astype(BF16), eye_ref[...]))
    o_ref[...] = jnp.concatenate(outs, axis=-1).astype(o_ref.dtype)


def _attn(q, kpad, vt, km, vmt, batch, seq, tq):
    nq = seq // tq
    eye = jnp.eye(VDIM, dtype=BF16)
    kern = functools.partial(_attn_kernel, tq=tq)
    return pl.pallas_call(
        kern,
        grid=(batch, nq),
        in_specs=[
            pl.BlockSpec((tq, HEADS * HP), lambda b, i: (b * nq + i, 0)),
            pl.BlockSpec((seq, HEADS * HP), lambda b, i: (b, 0)),
            pl.BlockSpec((nq, HEADS * VDIM, tq), lambda b, i: (b, 0, 0)),
            pl.BlockSpec(km.shape, lambda b, i: (0, 0)),
            pl.BlockSpec(vmt.shape, lambda b, i: (0, 0, 0)),
            pl.BlockSpec(eye.shape, lambda b, i: (0, 0)),
        ],
        out_specs=pl.BlockSpec((tq, HEADS * VDIM), lambda b, i: (b * nq + i, 0)),
        out_shape=jax.ShapeDtypeStruct((batch * seq, HEADS * VDIM), BF16),
        compiler_params=_params(("parallel", "arbitrary")),
        name="attn",
    )(q, kpad, vt, km, vmt, eye)


def _qlat_kernel(q_ref, wuk_ref, o_ref):
    for hh in range(HEADS):
        qn = q_ref[:, hh * HP:hh * HP + NOPE].astype(BF16)
        o_ref[hh, :, 0:KV_LORA] = _dot_nt(qn, wuk_ref[:, hh * NOPE:(hh + 1) * NOPE]) * SM_SCALE
        o_ref[hh, :, KV_LORA:KV_W] = q_ref[:, hh * HP + NOPE:hh * HP + NOPE + ROPE] * SM_SCALE


def _qlat(q, wuk):
    rows = q.shape[0]
    return pl.pallas_call(
        _qlat_kernel,
        out_shape=jax.ShapeDtypeStruct((HEADS, rows, KV_W), F32),
        name="qlat",
    )(q, wuk)


SATTN_SLOTS = 3
SATTN_CHUNK = 1024


def _sattn_kernel(pt_ref, qf_ref, kvn_ref, wuv_ref, cache_ref, o_ref, buf, sem, *, n_pages, chunk):
    b = pl.program_id(0)
    nb = pl.num_programs(0)
    ahead = SATTN_SLOTS - 1

    def page_copy(page, slot, j):
        return pltpu.make_async_copy(cache_ref.at[page], buf.at[slot, :, pl.ds(j * PAGE, PAGE)], sem.at[slot])

    def start_pages(bb, slot, lo, hi):
        for j in range(lo, hi):
            page_copy(pt_ref[bb, j], slot, j).start(priority=j % 2)

    def wait_slot(slot):
        for j in range(n_pages):
            page_copy(0, slot, j).wait()

    @pl.when(b == 0)
    def _():
        for first in range(ahead):
            start_pages(jnp.minimum(first, nb - 1), first, 0, n_pages)

    slot = b % SATTN_SLOTS
    wait_slot(slot)
    nxt = jnp.minimum(b + ahead, nb - 1)
    nslot = (b + ahead) % SATTN_SLOTS
    n_chunks = (n_pages * PAGE) // chunk
    per_chunk = n_pages // n_chunks

    qf = qf_ref[0].astype(BF16)
    kvn = kvn_ref[0].astype(BF16).astype(F32)
    s_own = jnp.sum(qf.astype(F32) * kvn, axis=-1, keepdims=True)
    kts, ss = [], []
    for ci in range(n_chunks):
        kt = buf[slot, :, ci * chunk:(ci + 1) * chunk].astype(BF16)
        kts.append(kt)
        ss.append(_dot(qf, kt))
        start_pages(nxt, nslot, ci * per_chunk, (ci + 1) * per_chunk)
    m = s_own
    for sc in ss:
        m = jnp.maximum(m, jnp.max(sc, axis=-1, keepdims=True))
    ps = [jnp.exp(sc - m) for sc in ss]
    p_own = jnp.exp(s_own - m)
    l = p_own
    for pc in ps:
        l = l + jnp.sum(pc, axis=-1, keepdims=True)
    acc = p_own.astype(BF16).astype(F32) * kvn[:, 0:KV_LORA]
    for pc, kt in zip(ps, kts):
        acc = acc + _dot_nt(pc.astype(BF16), kt[0:KV_LORA, :])

    @pl.when(b == nb - 1)
    def _():
        for extra in range(1, SATTN_SLOTS):
            wait_slot((b + extra) % SATTN_SLOTS)

    o_lat = (acc / l).astype(BF16)
    full = _dot(o_lat, wuv_ref[...])
    r = lax.broadcasted_iota(jnp.int32, full.shape, 0)
    cidx = lax.broadcasted_iota(jnp.int32, full.shape, 1)
    o_ref[0] = jnp.sum(jnp.where(cidx // VDIM == r, full, 0.0), axis=0, keepdims=True).astype(o_ref.dtype)


def _sattn(page_table, qf, kvn, wuv, cache):
    nb, n_pages = page_table.shape
    assert n_pages % 2 == 0
    chunk = min(SATTN_CHUNK, n_pages * PAGE)
    assert (n_pages * PAGE) % chunk == 0
    kern = functools.partial(_sattn_kernel, n_pages=n_pages, chunk=chunk)
    grid_spec = pltpu.PrefetchScalarGridSpec(
        num_scalar_prefetch=1,
        grid=(nb,),
        in_specs=[
            pl.BlockSpec((1, HEADS, KV_W), lambda b, pt: (b, 0, 0)),
            pl.BlockSpec((1, 1, KV_W), lambda b, pt: (b, 0, 0)),
            pl.BlockSpec(wuv.shape, lambda b, pt: (0, 0)),
            pl.BlockSpec(memory_space=pl.ANY),
        ],
        out_specs=pl.BlockSpec((1, 1, HEADS * VDIM), lambda b, pt: (b, 0, 0)),
        scratch_shapes=[pltpu.VMEM((SATTN_SLOTS, KV_W, n_pages * PAGE), F32),
                        pltpu.SemaphoreType.DMA((SATTN_SLOTS,))],
    )
    return pl.pallas_call(
        kern,
        grid_spec=grid_spec,
        out_shape=jax.ShapeDtypeStruct((nb, 1, HEADS * VDIM), BF16),
        compiler_params=_params(("arbitrary",)),
        name="sattn",
    )(page_table, qf, kvn, wuv, cache)


GROUP = 4
RWKV_BATCHES_PER_STEP = 8
GW = GROUP * HS
RW_PARAM_NAMES = ['mu_shift', 'w0', 'w2', 'a0', 'a2', 'g2', 'k_k', 'k_a', 'r_k', 'ln_w', 'ln_b', 'seg']


def _seg_sum(x, seg):
    return _dot(x.astype(BF16), seg)


def _rwkv_token_prep(cur, prev, p):
    z = cur + (prev - cur) * p['mu_shift']
    r = z[:, 0:RW_W]
    k = z[:, RW_W:2 * RW_W]
    v = z[:, 2 * RW_W:3 * RW_W]
    w_in = z[:, 3 * RW_W:3 * RW_W + W_LORA]
    a_in = z[:, 3 * RW_W + W_LORA:3 * RW_W + W_LORA + A_LORA]
    g_in = z[:, 3 * RW_W + W_LORA + A_LORA:RW_COLS]
    w_log = p['w0'] + _dot(jnp.tanh(w_in).astype(BF16), p['w2'])
    logw = -DECAY_SCALE * jax.nn.sigmoid(w_log)
    a_sig = jax.nn.sigmoid(p['a0'] + _dot(a_in.astype(BF16), p['a2']))
    g = _dot(jax.nn.sigmoid(g_in).astype(BF16), p['g2'])
    kk = k * p['k_k']
    kk = kk * lax.rsqrt(jnp.maximum(_seg_sum(kk * kk, p['seg']), 1e-24))
    k_mod = k * (1.0 + (a_sig - 1.0) * p['k_a'])
    bonus = _seg_sum(r * k_mod * p['r_k'], p['seg']) * v
    return r, k_mod, v, logw, a_sig, g, kk, bonus


def _group_norm_out(y, bonus, g, p):
    mean = _seg_sum(y, p['seg']) * (1.0 / HS)
    d = y - mean
    var = _seg_sum(d * d, p['seg']) * (1.0 / HS)
    yn = d * lax.rsqrt(var + GN_EPS) * p['ln_w'] + p['ln_b']
    return (yn + bonus) * g


def _rwkv_kernel(rw_ref, shift_ref, m0_ref, *rest, n_valid, nbat):
    prm_refs = rest[:len(RW_PARAM_NAMES)]
    tri_ref, ones_ref, o_ref, mout_ref, prev_scr, m_scr = rest[len(RW_PARAM_NAMES):]
    p = {n: ref[...] for n, ref in zip(RW_PARAM_NAMES, prm_refs)}
    chunk = HS
    ngrp = HEADS // GROUP
    c = pl.program_id(1)
    nc = pl.num_programs(1)

    @pl.when(c == 0)
    def _():
        for bi in range(nbat):
            prev_scr[bi] = shift_ref[0]
            m_scr[bi] = m0_ref[0]

    curs, prevs = [], []
    for bi in range(nbat):
        cur_b = rw_ref[bi]
        rowi = lax.broadcasted_iota(jnp.int32, cur_b.shape, 0)
        prevs.append(jnp.where(rowi == 0, prev_scr[bi], pltpu.roll(cur_b, 1, 0)))
        prev_scr[bi] = cur_b[chunk - 1:chunk, :]
        curs.append(cur_b)
    r, k_mod, v, logw, a_sig, g, kk, bonus = _rwkv_token_prep(
        jnp.concatenate(curs, axis=0), jnp.concatenate(prevs, axis=0), p)

    if n_valid is not None:
        step = lax.broadcasted_iota(jnp.int32, (nbat * chunk, RW_W), 0) % chunk
        live = (c * chunk + step) < n_valid
        logw = jnp.where(live, logw, 0.0)
        kk = jnp.where(live, kk, 0.0)
        k_mod = jnp.where(live, k_mod, 0.0)
        v = jnp.where(live, v, 0.0)

    lw3 = _split3(logw)
    seq_rows = [slice(bi * chunk, (bi + 1) * chunk) for bi in range(nbat)]
    gcum = jnp.concatenate([sum(_dot(tri_ref[...], t[rs]) for t in lw3) for rs in seq_rows], axis=0)
    g_end = jnp.concatenate([jnp.broadcast_to(gcum[rs][chunk - 1:chunk, :], (chunk, RW_W)) for rs in seq_rows], axis=0)
    r_t = r * jnp.exp(gcum)
    a_t = -kk * jnp.exp(gcum - logw)
    eng = jnp.exp(-gcum)
    b_t = kk * a_sig * eng
    k_t = k_mod * eng
    tail = jnp.exp(g_end - gcum)
    b_e = kk * a_sig * tail
    k_e = k_mod * tail
    decay_end = jnp.exp(g_end)

    ti = lax.broadcasted_iota(jnp.int32, (chunk, GW), 0)
    lane = lax.broadcasted_iota(jnp.int32, (chunk, GW), 1)
    si = lane % HS
    lane_head = lane // HS
    strict = si < ti
    incl = si <= ti
    eye = (si == ti).astype(F32)

    def bdr(zf):
        zb = zf.astype(BF16)
        return jnp.concatenate([jnp.where(lane_head == hh, zb, jnp.zeros_like(zb)) for hh in range(GROUP)], axis=0)

    streams = [(bi, gi) for bi in range(nbat) for gi in range(ngrp)]
    blk = lambda t, s: t[s[0] * chunk:(s[0] + 1) * chunk, s[1] * GW:(s[1] + 1) * GW]
    cat0 = lambda xs: jnp.concatenate(xs, axis=0).astype(BF16)
    cat1 = lambda xs: jnp.concatenate(xs, axis=1).astype(BF16)

    ar = [cat0([blk(a_t, s), blk(r_t, s)]) for s in streams]
    ab = [_dot_nt(ar[i], bdr(blk(b_t, s))) for i, s in enumerate(streams)]
    ak = [_dot_nt(ar[i], bdr(blk(k_t, s))) for i, s in enumerate(streams)]
    n = [jnp.where(strict, t[0:chunk], 0.0) for t in ab]
    a_rb = [jnp.where(incl, t[chunk:2 * chunk], 0.0) for t in ab]
    a_ak = [jnp.where(strict, t[0:chunk], 0.0) for t in ak]
    a_rk = [jnp.where(incl, t[chunk:2 * chunk], 0.0) for t in ak]

    x = [eye + t for t in n]
    pw = [_dot(t.astype(BF16), bdr(t)) for t in n]
    for _ in range(4):
        z = [_dot(cat0([xi, pi]), bdr(pi)) for xi, pi in zip(x, pw)]
        x = [xi + zi[0:chunk] for xi, zi in zip(x, z)]
        pw = [zi[chunk:2 * chunk] for zi in z]
    z = [_dot(xi.astype(BF16), bdr(pi)) for xi, pi in zip(x, pw)]
    x = [xi + zi for xi, zi in zip(x, z)]

    m0 = [m_scr[s[0], s[1]] for s in streams]
    bm = [bdr(t) for t in m0]
    bv = [bdr(blk(v, s)) for s in streams]
    rhs = [_dot(cat1([blk(a_t, s), a_ak[i]]), jnp.concatenate([bm[i], bv[i]], axis=0)) for i, s in enumerate(streams)]
    u = [_dot(xi.astype(BF16), bdr(ri)) for xi, ri in zip(x, rhs)]
    y = [_dot(cat1([blk(r_t, s), a_rb[i], a_rk[i]]), jnp.concatenate([bm[i], bdr(u[i]), bv[i]], axis=0))
         for i, s in enumerate(streams)]
    full = [_dot_tn(cat0([blk(b_e, s), blk(k_e, s)]), cat0([u[i], blk(v, s)])) for i, s in enumerate(streams)]
    gm = [sum(_dot(t, ones_ref[...]) for t in _split3(eye * blk(decay_end, s)[0:1])) for s in streams]
    for i, s in enumerate(streams):
        upd = jnp.where(lane_head == 0, full[i][0:HS], 0.0)
        for hh in range(1, GROUP):
            upd = upd + jnp.where(lane_head == hh, full[i][hh * HS:(hh + 1) * HS], 0.0)
        m_scr[s[0], s[1]] = gm[i] * m0[i] + upd

    y_all = jnp.concatenate([jnp.concatenate([y[bi * ngrp + gi] for gi in range(ngrp)], axis=1)
                             for bi in range(nbat)], axis=0)
    out = _group_norm_out(y_all, bonus, g, p).astype(o_ref.dtype)
    for bi in range(nbat):
        o_ref[bi] = out[seq_rows[bi]]

    @pl.when(c == nc - 1)
    def _():
        mout_ref[...] = m_scr[...]


def _state_to_m(s):
    b = s.shape[0]
    return jnp.transpose(s.reshape(b, HEADS // GROUP, GROUP, HS, HS), (0, 1, 4, 2, 3)).reshape(b, HEADS // GROUP, HS, GW)


def _m_to_state(m):
    b = m.shape[0]
    return jnp.transpose(m.reshape(b, HEADS // GROUP, HS, GROUP, HS), (0, 1, 3, 4, 2)).reshape(b, HEADS, HS, HS)


def _rwkv(rw, shift0, s0, wts, batch, n_valid):
    chunk = HS
    rows = rw.shape[0]
    seq = rows // batch
    nc = seq // chunk
    nbat = RWKV_BATCHES_PER_STEP if batch % RWKV_BATCHES_PER_STEP == 0 else 1
    const2 = lambda b, c: (0, 0)
    consts = [wts[n] for n in RW_PARAM_NAMES]
    tri = (jnp.arange(chunk)[:, None] >= jnp.arange(chunk)[None, :]).astype(BF16)
    hd = jnp.arange(GW) // HS
    ones_bd = (hd[:, None] == hd[None, :]).astype(BF16)
    m0 = _state_to_m(s0)
    ngrp = HEADS // GROUP
    kern = functools.partial(_rwkv_kernel, n_valid=n_valid, nbat=nbat)
    o, m = pl.pallas_call(
        kern,
        grid=(batch // nbat, nc),
        in_specs=[pl.BlockSpec((nbat, chunk, RW_COLS), lambda b, c: (b, c, 0)),
                  pl.BlockSpec((1, 1, RW_COLS), lambda b, c: (0, 0, 0)),
                  pl.BlockSpec((1, ngrp, HS, GW), lambda b, c: (0, 0, 0, 0))]
        + [pl.BlockSpec(a.shape, const2) for a in consts]
        + [pl.BlockSpec(tri.shape, const2), pl.BlockSpec(ones_bd.shape, const2)],
        out_specs=[pl.BlockSpec((nbat, chunk, RW_W), lambda b, c: (b, c, 0)),
                   pl.BlockSpec((nbat, ngrp, HS, GW), lambda b, c: (b, 0, 0, 0))],
        out_shape=[jax.ShapeDtypeStruct((batch, seq, RW_W), BF16),
                   jax.ShapeDtypeStruct((batch, ngrp, HS, GW), F32)],
        scratch_shapes=[pltpu.VMEM((nbat, 1, RW_COLS), F32), pltpu.VMEM((nbat, ngrp, HS, GW), F32)],
        compiler_params=_params(("parallel", "arbitrary")),
        name="rwkv",
    )(rw.reshape(batch, seq, RW_COLS), shift0, m0, *consts, tri, ones_bd)
    return o.reshape(rows, RW_W), _m_to_state(m)


def _rwkv_step_kernel(rw_ref, shift_ref, s_ref, *rest):
    prm_refs = rest[:len(RW_PARAM_NAMES)]
    o_ref, sout_ref, t_scr, y_scr, tok_scr = rest[len(RW_PARAM_NAMES):]
    h = pl.program_id(0)

    @pl.when(h == 0)
    def _():
        p = {n: ref[...] for n, ref in zip(RW_PARAM_NAMES, prm_refs)}
        r, k_mod, v, logw, a_sig, g, kk, bonus = _rwkv_token_prep(rw_ref[...], shift_ref[...], p)
        t_scr[0] = r.T
        t_scr[1] = jnp.exp(logw).T
        t_scr[2] = k_mod.T
        t_scr[3] = v.T
        t_scr[4] = (-kk).T
        t_scr[5] = (kk * a_sig).T
        tok_scr[0] = bonus
        tok_scr[1] = g

    rows = pl.ds(pl.multiple_of(h * HS, HS), HS)
    r_h, w_h, k_h = t_scr[0, rows, :], t_scr[1, rows, :], t_scr[2, rows, :]
    a_h, b_h = t_scr[4, rows, :], t_scr[5, rows, :]

    def body(vv, carry):
        sv = s_ref[0, vv]
        vrow = t_scr[3, pl.ds(h * HS + vv, 1), :]
        sa = jnp.sum(sv * a_h, axis=0, keepdims=True)
        sn = sv * w_h + sa * b_h + vrow * k_h
        sout_ref[0, vv] = sn
        y_scr[pl.ds(h * HS + vv, 1), :] = jnp.sum(sn * r_h, axis=0, keepdims=True)
        return carry
    lax.fori_loop(0, HS, body, 0)

    @pl.when(h == HEADS - 1)
    def _():
        p = {n: ref[...] for n, ref in zip(RW_PARAM_NAMES, prm_refs)}
        o_ref[...] = _group_norm_out(y_scr[...].T, tok_scr[0], tok_scr[1], p).astype(o_ref.dtype)


def _rwkv_step(rw, shift, s_t, wts):
    nb = rw.shape[0]
    consts = [wts[n] for n in RW_PARAM_NAMES]
    const2 = lambda h: (0, 0)
    return pl.pallas_call(
        _rwkv_step_kernel,
        grid=(HEADS,),
        in_specs=[pl.BlockSpec((nb, RW_COLS), const2), pl.BlockSpec((nb, RW_COLS), const2),
                  pl.BlockSpec((1, HS, HS, nb), lambda h: (h, 0, 0, 0))]
        + [pl.BlockSpec(a.shape, const2) for a in consts],
        out_specs=[pl.BlockSpec((nb, RW_W), const2), pl.BlockSpec((1, HS, HS, nb), lambda h: (h, 0, 0, 0))],
        out_shape=[jax.ShapeDtypeStruct((nb, RW_W), BF16), jax.ShapeDtypeStruct(s_t.shape, F32)],
        scratch_shapes=[pltpu.VMEM((6, RW_W, nb), F32), pltpu.VMEM((RW_W, nb), F32), pltpu.VMEM((2, nb, RW_W), F32)],
        compiler_params=_params(("arbitrary",)),
        name="rwkv_step",
    )(rw, shift, s_t, *consts)


def _mlp_kernel(x_ref, om_ref, or_ref, gs_ref, wom_ref, wor_ref, wout_ref, gffn_ref, wup_ref, wdn_ref, gfin_ref,
                y_ref):
    a = _dot(om_ref[...], wom_ref[...])
    r = _dot(or_ref[...], wor_ref[...])
    mix = gs_ref[:, 0:D_MODEL] * a + gs_ref[:, D_MODEL:2 * D_MODEL] * r
    x1 = x_ref[...] + _dot(mix.astype(BF16), wout_ref[...])
    h2 = _rms(x1, gffn_ref[...]).astype(BF16)
    u = jnp.maximum(_dot(h2, wup_ref[...]), 0.0)
    x2 = x1 + _dot((u * u).astype(BF16), wdn_ref[...])
    y_ref[...] = _rms(x2, gfin_ref[...])


def _mlp(x, om, orw, gsig, wts, tm):
    rows = x.shape[0]
    row = lambda i: (i, 0)
    const = lambda i: (0, 0)
    full = lambda a: pl.BlockSpec(a.shape, const, pipeline_mode=pl.Buffered(1))
    names = ['w_o_mla', 'w_o_rwkv', 'w_out', 'g_ffn', 'w_up', 'w_down', 'g_final']
    return pl.pallas_call(
        _mlp_kernel,
        grid=(rows // tm,),
        in_specs=[pl.BlockSpec((tm, D_MODEL), row), pl.BlockSpec((tm, HEADS * VDIM), row),
                  pl.BlockSpec((tm, RW_W), row), pl.BlockSpec((tm, 2 * D_MODEL), row)]
        + [full(wts[n]) for n in names],
        out_specs=pl.BlockSpec((tm, D_MODEL), row),
        out_shape=jax.ShapeDtypeStruct((rows, D_MODEL), F32),
        compiler_params=_params(("parallel",)),
        name="mlp",
    )(x, om, orw, gsig, *[wts[n] for n in names])


def _swap_halves(w):
    return jnp.concatenate([w[..., ROPE // 2:], w[..., :ROPE // 2]], axis=-1)


def _prep_weights(g_final, g_mix, w_in, g_q, w_uq, g_kv, w_uk, w_uv, w_o_mla, mu_shift, w0, w2, a0, a2, g2,
                  k_k, k_a, r_k, ln_w, ln_b, w_o_rwkv, w_out, g_ffn, w_up, w_down):
    row = lambda a: a.reshape(1, -1).astype(F32)
    q_in = w_in[:, 0:Q_LORA]
    kv_in = w_in[:, Q_LORA:Q_LORA + KV_LORA]
    kr = w_in[:, Q_LORA + KV_LORA:Q_LORA + KV_LORA + ROPE]
    o_rw = Q_LORA + KV_LORA + ROPE
    rw = w_in[:, o_rw:o_rw + RW_COLS]
    gate = w_in[:, o_rw + RW_COLS:]
    zr = jnp.zeros((D_MODEL, ROPE), F32)
    kr_a = jnp.concatenate([kr, zr, kr, zr], axis=1)
    krs = _swap_halves(kr)
    kr_b = jnp.concatenate([krs, zr, krs, zr], axis=1)
    w_in_p = jnp.concatenate([gate, rw, q_in, kv_in, kr_a, kr_b], axis=1).astype(BF16)

    uq = w_uq.reshape(Q_LORA, HEADS, NOPE + ROPE)
    zq = jnp.zeros((Q_LORA, HEADS, HP - NOPE - ROPE), F32)
    uq_pad = jnp.concatenate([uq, zq], axis=-1).reshape(Q_LORA, HEADS * HP)
    uq_sw = jnp.concatenate([jnp.zeros((Q_LORA, HEADS, NOPE), F32), _swap_halves(uq[..., NOPE:]), zq], axis=-1)
    w_uq_p = jnp.concatenate([uq_pad, uq_sw.reshape(Q_LORA, HEADS * HP)], axis=1).astype(BF16)

    uk = w_uk.reshape(KV_LORA, HEADS, NOPE)
    uk_pad = jnp.concatenate([uk, jnp.zeros((KV_LORA, HEADS, HP - NOPE), F32)], axis=-1).reshape(KV_LORA, HEADS * HP)
    w_ukv = uk_pad.astype(BF16)

    head = jnp.arange(RW_W) // HS
    seg = (head[:, None] == head[None, :]).astype(BF16)
    return dict(
        g_mix=row(g_mix), w_in=w_in_p, g_q=row(g_q), w_uq=w_uq_p, g_kv=row(g_kv), w_ukv=w_ukv,
        w_uk=w_uk.astype(BF16), w_uv=w_uv.astype(BF16), w_uvt=w_uv.T.astype(BF16), w_o_mla=w_o_mla.astype(BF16),
        mu_shift=row(mu_shift), w0=row(w0), w2=w2.astype(BF16), a0=row(a0), a2=a2.astype(BF16),
        g2=g2.astype(BF16), k_k=row(k_k), k_a=row(k_a), r_k=row(r_k), ln_w=row(ln_w), ln_b=row(ln_b), seg=seg,
        w_o_rwkv=w_o_rwkv.astype(BF16), w_out=w_out.astype(BF16), g_ffn=row(g_ffn),
        w_up=w_up.astype(BF16), w_down=w_down.astype(BF16), g_final=row(g_final))


def _rope_tables(pos):
    inv = ROPE_BASE ** (-jnp.arange(0, ROPE, 2, dtype=F32) / ROPE)
    ang = pos.astype(F32)[:, None] * inv[None, :]
    cos, sin = jnp.cos(ang), jnp.sin(ang)
    n = pos.shape[0]
    cc = jnp.concatenate([cos, cos], axis=1)
    ss = jnp.concatenate([-sin, sin], axis=1)
    z32 = jnp.zeros((n, ROPE), F32)
    ctq = jnp.concatenate([jnp.ones((n, NOPE), F32), cc, z32], axis=1)
    stq = jnp.concatenate([jnp.zeros((n, NOPE), F32), ss, z32], axis=1)
    ctk = jnp.concatenate([cc, z32, cc, z32], axis=1)
    stk = jnp.concatenate([ss, z32, ss, z32], axis=1)
    return ctq, stq, ctk, stk


def _row_tile(rows, cap):
    tm = min(rows, cap)
    assert rows % tm == 0
    return tm


def kernel(x_prompt, x_sample, cache_kv, page_table, state_wkv, state_shift, meta_tokens, g_final, g_mix, w_in, g_q, w_uq, g_kv, w_uk, w_uv, w_o_mla, mu_shift, w0, w2, a0, a2, g2, k_k, k_a, r_k, ln_w, ln_b, w_o_rwkv, w_out, g_ffn, w_up, w_down):
    depth = g_mix.shape[0]
    assert depth == 1
    bp, seq, _ = x_prompt.shape
    bd, s_s, _ = x_sample.shape
    assert s_s == 1
    n_pages = page_table.shape[1]
    past_len = n_pages * PAGE
    wts = _prep_weights(g_final, g_mix[0], w_in[0], g_q[0], w_uq[0], g_kv[0], w_uk[0], w_uv[0], w_o_mla[0],
                        mu_shift[0], w0[0], w2[0], a0[0], a2[0], g2[0], k_k[0], k_a[0], r_k[0], ln_w[0], ln_b[0],
                        w_o_rwkv[0], w_out[0], g_ffn[0], w_up[0], w_down[0])

    tab_m = _rope_tables(jnp.arange(N_META))
    _, rw_m, _, kvrow_m, kpad_m, vt_m = _proj(meta_tokens.astype(F32), tab_m, wts, N_META, BF16)
    rw_m_pad = jnp.concatenate([rw_m, jnp.zeros((HS - N_META, RW_COLS), F32)], axis=0)
    zero_shift = jnp.zeros((1, 1, RW_COLS), F32)
    zero_state = jnp.zeros((1, HEADS, HS, HS), F32)
    _, s_meta = _rwkv(rw_m_pad, zero_shift, zero_state, wts, 1, N_META)
    shift_meta = rw_m[N_META - 1:N_META].reshape(1, 1, RW_COLS)

    rows_p = bp * seq
    tm = _row_tile(seq, ROW_TILE)
    tq = _row_tile(seq, KV_TILE)
    xp = x_prompt.reshape(rows_p, D_MODEL)
    ctq, stq, ctk, stk = _rope_tables(N_META + jnp.arange(seq))
    tab_p = (ctq * Q_PRESCALE, stq * Q_PRESCALE, ctk, stk)
    gsig, rw, q, kvrow, kpad, vt = _proj(xp, tab_p, wts, tm, BF16)
    o_mla = _attn(q, kpad, vt, kpad_m, vt_m, bp, seq, tq)
    o_rwkv, s_p = _rwkv(rw, shift_meta, s_meta, wts, bp, None)
    y_prompt = _mlp(xp, o_mla, o_rwkv, gsig, wts, tm).reshape(bp, seq, D_MODEL)
    kv_prompt = jnp.concatenate([jnp.broadcast_to(kvrow_m[None], (bp, N_META, KV_W)),
                                 kvrow.reshape(bp, seq, KV_W)], axis=1)[None]
    shift_prompt = rw.reshape(bp, seq, RW_COLS)[:, -1][None]

    xs = x_sample.reshape(bd, D_MODEL)
    tab_s = _rope_tables(jnp.full((bd,), past_len, jnp.int32))
    gsig_s, rw_s, q_s, kvrow_s, _, _ = _proj(xs, tab_s, wts, bd, F32)
    qf = jnp.transpose(_qlat(q_s, wts['w_uk']), (1, 0, 2))
    cache = jnp.swapaxes(cache_kv.reshape(cache_kv.shape[1], PAGE, KV_W), 1, 2)
    o_mla_s = _sattn(page_table, qf, kvrow_s.reshape(bd, 1, KV_W), wts['w_uv'], cache).reshape(bd, HEADS * VDIM)
    s_t = jnp.transpose(state_wkv[0], (1, 2, 3, 0))
    o_rwkv_s, s_t_new = _rwkv_step(rw_s, state_shift[0], s_t, wts)
    s_s_new = jnp.transpose(s_t_new, (3, 0, 1, 2))
    y_sample = _mlp(xs, o_mla_s, o_rwkv_s, gsig_s, wts, bd).reshape(bd, 1, D_MODEL)

    return (y_prompt, y_sample,
            kv_prompt.astype(cache_kv.dtype), s_p[None].astype(state_wkv.dtype),
            shift_prompt.astype(state_shift.dtype),
            kvrow_s.reshape(1, bd, 1, KV_W).astype(cache_kv.dtype), s_s_new[None].astype(state_wkv.dtype),
            rw_s[None].astype(state_shift.dtype))
```

```python
import functools

import jax
import jax.numpy as jnp
from jax import lax
from jax.experimental import pallas as pl
from jax.experimental.pallas import tpu as pltpu

F32 = jnp.float32
BF16 = jnp.bfloat16

D_MODEL = 1024
N_META = 16
PAGE = 128
HEADS = 8
NOPE = 64
ROPE = 32
VDIM = 64
Q_LORA = 384
KV_LORA = 256
KV_W = KV_LORA + ROPE
ROPE_BASE = 10000.0
SM_SCALE = (NOPE + ROPE) ** -0.5
HS = 64
RW_W = HEADS * HS
W_LORA = 64
A_LORA = 64
G_LORA = 128
RW_COLS = 3 * RW_W + W_LORA + A_LORA + G_LORA
GN_EPS = 64e-5
D_FF = 4 * D_MODEL
NORM_EPS = 1e-6
NEG_INF = -1e30
DECAY_SCALE = 0.6065306597126334
Q_PRESCALE = SM_SCALE * 1.4426950408889634

LANE = 128
HP = 128
VMEM_LIMIT = 56 * 1024 * 1024
KV_TILE = 256
ROW_TILE = 512

C_GATE = 0
C_RW = 2 * D_MODEL
C_QIN = C_RW + RW_COLS
C_KVIN = C_QIN + Q_LORA
C_KR = C_KVIN + KV_LORA
IN_PERM = C_KR + LANE


def _dot(a, b):
    return jnp.dot(a, b, preferred_element_type=F32)


def _dot_nt(a, b):
    return lax.dot_general(a, b, (((1,), (1,)), ((), ())), preferred_element_type=F32)


def _dot_tn(a, b):
    return lax.dot_general(a, b, (((0,), (0,)), ((), ())), preferred_element_type=F32)


def _split2(x):
    hi = x.astype(BF16)
    lo = (x - hi.astype(F32)).astype(BF16)
    return hi, lo


def _split3(x):
    x1 = x.astype(BF16)
    r1 = x - x1.astype(F32)
    x2 = r1.astype(BF16)
    x3 = (r1 - x2.astype(F32)).astype(BF16)
    return x1, x2, x3


def _rms(x, g):
    return x * lax.rsqrt(jnp.mean(x * x, axis=-1, keepdims=True) + NORM_EPS) * g


def _params(sem):
    return pltpu.CompilerParams(dimension_semantics=sem, vmem_limit_bytes=VMEM_LIMIT)


def _proj_kernel(x_ref, gmix_ref, win_ref, gq_ref, wuq_ref, gkv_ref, wukv_ref, wuvt_ref,
                 ctq_ref, stq_ref, ctk_ref, stk_ref,
                 gsig_ref, rw_ref, q_ref, kvrow_ref, kpad_ref, vt_ref):
    h = _rms(x_ref[...], gmix_ref[...]).astype(BF16)
    step = 512
    for c in range(0, 2 * D_MODEL, step):
        gsig_ref[:, c:c + step] = jax.nn.sigmoid(_dot(h, win_ref[:, C_GATE + c:C_GATE + c + step])).astype(BF16)
    for c in range(0, RW_COLS, 256):
        rw_ref[:, c:c + 256] = _dot(h, win_ref[:, C_RW + c:C_RW + c + 256])

    lane = lax.broadcasted_iota(jnp.int32, (x_ref.shape[0], LANE), 1)
    first_half = (lane % ROPE) < ROPE // 2

    def rope(t, cos_t, sin_t):
        swapped = jnp.where(first_half, pltpu.roll(t, LANE - ROPE // 2, 1), pltpu.roll(t, ROPE // 2, 1))
        return t * cos_t + swapped * sin_t

    qn = _rms(_dot(h, win_ref[:, C_QIN:C_QIN + Q_LORA]), gq_ref[...]).astype(BF16)
    ctq = ctq_ref[...]
    stq = stq_ref[...]
    qa = _dot(qn, wuq_ref[...])
    for hh in range(HEADS):
        hs = slice(hh * HP, (hh + 1) * HP)
        q_ref[:, hs] = rope(qa[:, hs], ctq, stq).astype(q_ref.dtype)

    ckv = _rms(_dot(h, win_ref[:, C_KVIN:C_KVIN + KV_LORA]), gkv_ref[...])
    kvrow_ref[:, 0:KV_LORA] = ckv
    kr = rope(_dot(h, win_ref[:, C_KR:C_KR + LANE]), ctk_ref[...], stk_ref[...])
    kvrow_ref[:, KV_LORA:KV_W] = kr[:, 0:ROPE]
    lane = lax.broadcasted_iota(jnp.int32, kr.shape, 1)
    kr_head = jnp.where((lane >= NOPE) & (lane < NOPE + ROPE), kr, 0.0)
    cb = ckv.astype(BF16)
    kn = _dot(cb, wukv_ref[...])
    for hh in range(HEADS):
        hs = slice(hh * HP, (hh + 1) * HP)
        kpad_ref[:, hs] = (kn[:, hs] + kr_head).astype(BF16)
    tw = vt_ref.shape[2]
    for sub in range(vt_ref.shape[0]):
        vt_ref[sub] = _dot_nt(wuvt_ref[...], cb[sub * tw:(sub + 1) * tw]).astype(BF16)


def _proj(x, tables, wts, tm, q_dtype):
    rows = x.shape[0]
    nt = tables[0].shape[0] // tm
    tw = min(tm, KV_TILE)
    row = lambda i: (i, 0)
    const = lambda i: (0, 0)
    tab = lambda i: (i % nt, 0)
    full = lambda a: pl.BlockSpec(a.shape, const, pipeline_mode=pl.Buffered(1))
    in_specs = [pl.BlockSpec((tm, D_MODEL), row), full(wts['g_mix']), full(wts['w_in']), full(wts['g_q']),
                full(wts['w_uq']), full(wts['g_kv']), full(wts['w_ukv']), full(wts['w_uvt'])]
    in_specs += [pl.BlockSpec((tm, LANE), tab)] * 4
    widths = [(2 * D_MODEL, BF16), (RW_COLS, F32), (HEADS * HP, q_dtype), (KV_W, F32), (HEADS * HP, BF16)]
    nt_rows = rows // tm
    return pl.pallas_call(
        _proj_kernel,
        grid=(nt_rows,),
        in_specs=in_specs,
        out_specs=[pl.BlockSpec((tm, w), row) for w, _ in widths]
        + [pl.BlockSpec((tm // tw, HEADS * VDIM, tw), lambda i: (i, 0, 0))],
        out_shape=[jax.ShapeDtypeStruct((rows, w), dt) for w, dt in widths]
        + [jax.ShapeDtypeStruct((rows // tw, HEADS * VDIM, tw), BF16)],
        compiler_params=_params(("parallel",)),
        name="proj",
    )(x, wts['g_mix'], wts['w_in'], wts['g_q'], wts['w_uq'], wts['g_kv'], wts['w_ukv'], wts['w_uvt'], *tables)


def _attn_kernel(q_ref, k_ref, vt_ref, km_ref, vmt_ref, eye_ref, o_ref, *, tq):
    i = pl.program_id(1)
    qs = [q_ref[:, hh * HP:(hh + 1) * HP] for hh in range(HEADS)]

    def update(carry, k_tiles, vt_tiles, mask):
        ss = [_dot_nt(k_tiles[hh], qs[hh]) for hh in range(HEADS)]
        ps, stats = [], []
        for hh in range(HEADS):
            m, l, _ = carry[hh]
            s = ss[hh]
            if mask is not None:
                s = jnp.where(mask, s, NEG_INF)
            m_new = jnp.maximum(m, jnp.max(s, axis=0, keepdims=True))
            alpha = jnp.exp2(m - m_new)
            p = jnp.exp2(s - m_new)
            stats.append((m_new, l * alpha + jnp.sum(p, axis=0, keepdims=True), alpha))
            ps.append(p.astype(BF16))
        pv = [_dot(vt_tiles[hh], ps[hh]) for hh in range(HEADS)]
        return tuple((stats[hh][0], stats[hh][1], carry[hh][2] * stats[hh][2] + pv[hh]) for hh in range(HEADS))

    init = (jnp.full((1, tq), NEG_INF, F32), jnp.zeros((1, tq), F32), jnp.zeros((VDIM, tq), F32))
    carry = update((init,) * HEADS, [km_ref[:, hh * HP:(hh + 1) * HP] for hh in range(HEADS)],
                   [vmt_ref[0, hh * VDIM:(hh + 1) * VDIM, :] for hh in range(HEADS)], None)

    def tile(j, carry, mask):
        rows = pl.ds(pl.multiple_of(j * tq, tq), tq)
        return update(carry, [k_ref[rows, hh * HP:(hh + 1) * HP] for hh in range(HEADS)],
                      [vt_ref[j, hh * VDIM:(hh + 1) * VDIM, :] for hh in range(HEADS)], mask)

    carry = lax.fori_loop(0, i, lambda j, c: tile(j, c, None), carry)
    key = lax.broadcasted_iota(jnp.int32, (tq, tq), 0)
    qry = lax.broadcasted_iota(jnp.int32, (tq, tq), 1)
    carry = tile(i, carry, key <= qry)

    outs = []
    for hh in range(HEADS):
        m, l, acc = carry[hh]
        outs.append(_dot_tn((acc / l).astype(BF16), eye_ref[...]))
    o_ref[...] = jnp.concatenate(outs, axis=-1).astype(o_ref.dtype)


def _attn(q, kpad, vt, km, vmt, batch, seq, tq):
    nq = seq // tq
    eye = jnp.eye(VDIM, dtype=BF16)
    kern = functools.partial(_attn_kernel, tq=tq)
    return pl.pallas_call(
        kern,
        grid=(batch, nq),
        in_specs=[
            pl.BlockSpec((tq, HEADS * HP), lambda b, i: (b * nq + i, 0)),
            pl.BlockSpec((seq, HEADS * HP), lambda b, i: (b, 0)),
            pl.BlockSpec((nq, HEADS * VDIM, tq), lambda b, i: (b, 0, 0)),
            pl.BlockSpec(km.shape, lambda b, i: (0, 0)),
            pl.BlockSpec(vmt.shape, lambda b, i: (0, 0, 0)),
            pl.BlockSpec(eye.shape, lambda b, i: (0, 0)),
        ],
        out_specs=pl.BlockSpec((tq, HEADS * VDIM), lambda b, i: (b * nq + i, 0)),
        out_shape=jax.ShapeDtypeStruct((batch * seq, HEADS * VDIM), BF16),
        compiler_params=_params(("parallel", "arbitrary")),
        name="attn",
    )(q, kpad, vt, km, vmt, eye)


def _qlat_kernel(q_ref, wuk_ref, o_ref):
    for hh in range(HEADS):
        qn = q_ref[:, hh * HP:hh * HP + NOPE].astype(BF16)
        o_ref[hh, :, 0:KV_LORA] = _dot_nt(qn, wuk_ref[:, hh * NOPE:(hh + 1) * NOPE]) * SM_SCALE
        o_ref[hh, :, KV_LORA:KV_W] = q_ref[:, hh * HP + NOPE:hh * HP + NOPE + ROPE] * SM_SCALE


def _qlat(q, wuk):
    rows = q.shape[0]
    return pl.pallas_call(
        _qlat_kernel,
        out_shape=jax.ShapeDtypeStruct((HEADS, rows, KV_W), F32),
        name="qlat",
    )(q, wuk)


SATTN_SLOTS = 3
SATTN_CHUNK = 1024


def _sattn_kernel(pt_ref, qf_ref, kvn_ref, wuv_ref, cache_ref, o_ref, buf, sem, *, n_pages, chunk):
    b = pl.program_id(0)
    nb = pl.num_programs(0)
    ahead = SATTN_SLOTS - 1

    def page_copy(page, slot, j):
        return pltpu.make_async_copy(cache_ref.at[page], buf.at[slot, :, pl.ds(j * PAGE, PAGE)], sem.at[slot])

    def start_pages(bb, slot, lo, hi):
        for j in range(lo, hi):
            page_copy(pt_ref[bb, j], slot, j).start(priority=j % 2)

    def wait_slot(slot):
        for j in range(n_pages):
            page_copy(0, slot, j).wait()

    @pl.when(b == 0)
    def _():
        for first in range(ahead):
            start_pages(jnp.minimum(first, nb - 1), first, 0, n_pages)

    slot = b % SATTN_SLOTS
    wait_slot(slot)
    nxt = jnp.minimum(b + ahead, nb - 1)
    nslot = (b + ahead) % SATTN_SLOTS
    n_chunks = (n_pages * PAGE) // chunk
    per_chunk = n_pages // n_chunks

    qf = qf_ref[0].astype(BF16)
    kvn = kvn_ref[0].astype(BF16).astype(F32)
    s_own = jnp.sum(qf.astype(F32) * kvn, axis=-1, keepdims=True)
    kts, ss = [], []
    for ci in range(n_chunks):
        kt = buf[slot, :, ci * chunk:(ci + 1) * chunk].astype(BF16)
        kts.append(kt)
        ss.append(_dot(qf, kt))
        start_pages(nxt, nslot, ci * per_chunk, (ci + 1) * per_chunk)
    m = s_own
    for sc in ss:
        m = jnp.maximum(m, jnp.max(sc, axis=-1, keepdims=True))
    ps = [jnp.exp(sc - m) for sc in ss]
    p_own = jnp.exp(s_own - m)
    l = p_own
    for pc in ps:
        l = l + jnp.sum(pc, axis=-1, keepdims=True)
    acc = p_own.astype(BF16).astype(F32) * kvn[:, 0:KV_LORA]
    for pc, kt in zip(ps, kts):
        acc = acc + _dot_nt(pc.astype(BF16), kt[0:KV_LORA, :])

    @pl.when(b == nb - 1)
    def _():
        for extra in range(1, SATTN_SLOTS):
            wait_slot((b + extra) % SATTN_SLOTS)

    o_lat = (acc / l).astype(BF16)
    full = _dot(o_lat, wuv_ref[...])
    r = lax.broadcasted_iota(jnp.int32, full.shape, 0)
    cidx = lax.broadcasted_iota(jnp.int32, full.shape, 1)
    o_ref[0] = jnp.sum(jnp.where(cidx // VDIM == r, full, 0.0), axis=0, keepdims=True).astype(o_ref.dtype)


def _sattn(page_table, qf, kvn, wuv, cache):
    nb, n_pages = page_table.shape
    assert n_pages % 2 == 0
    chunk = min(SATTN_CHUNK, n_pages * PAGE)
    assert (n_pages * PAGE) % chunk == 0
    kern = functools.partial(_sattn_kernel, n_pages=n_pages, chunk=chunk)
    grid_spec = pltpu.PrefetchScalarGridSpec(
        num_scalar_prefetch=1,
        grid=(nb,),
        in_specs=[
            pl.BlockSpec((1, HEADS, KV_W), lambda b, pt: (b, 0, 0)),
            pl.BlockSpec((1, 1, KV_W), lambda b, pt: (b, 0, 0)),
            pl.BlockSpec(wuv.shape, lambda b, pt: (0, 0)),
            pl.BlockSpec(memory_space=pl.ANY),
        ],
        out_specs=pl.BlockSpec((1, 1, HEADS * VDIM), lambda b, pt: (b, 0, 0)),
        scratch_shapes=[pltpu.VMEM((SATTN_SLOTS, KV_W, n_pages * PAGE), F32),
                        pltpu.SemaphoreType.DMA((SATTN_SLOTS,))],
    )
    return pl.pallas_call(
        kern,
        grid_spec=grid_spec,
        out_shape=jax.ShapeDtypeStruct((nb, 1, HEADS * VDIM), BF16),
        compiler_params=_params(("arbitrary",)),
        name="sattn",
    )(page_table, qf, kvn, wuv, cache)


GROUP = 4
RWKV_BATCHES_PER_STEP = 8
GW = GROUP * HS
RW_PARAM_NAMES = ['mu_shift', 'w0', 'w2', 'a0', 'a2', 'g2', 'k_k', 'k_a', 'r_k', 'ln_w', 'ln_b', 'seg']


def _seg_sum(x, seg):
    return _dot(x.astype(BF16), seg)


def _rwkv_token_prep(cur, prev, p):
    z = cur + (prev - cur) * p['mu_shift']
    r = z[:, 0:RW_W]
    k = z[:, RW_W:2 * RW_W]
    v = z[:, 2 * RW_W:3 * RW_W]
    w_in = z[:, 3 * RW_W:3 * RW_W + W_LORA]
    a_in = z[:, 3 * RW_W + W_LORA:3 * RW_W + W_LORA + A_LORA]
    g_in = z[:, 3 * RW_W + W_LORA + A_LORA:RW_COLS]
    w_log = p['w0'] + _dot(jnp.tanh(w_in).astype(BF16), p['w2'])
    logw = -DECAY_SCALE * jax.nn.sigmoid(w_log)
    a_sig = jax.nn.sigmoid(p['a0'] + _dot(a_in.astype(BF16), p['a2']))
    g = _dot(jax.nn.sigmoid(g_in).astype(BF16), p['g2'])
    kk = k * p['k_k']
    kk = kk * lax.rsqrt(jnp.maximum(_seg_sum(kk * kk, p['seg']), 1e-24))
    k_mod = k * (1.0 + (a_sig - 1.0) * p['k_a'])
    bonus = _seg_sum(r * k_mod * p['r_k'], p['seg']) * v
    return r, k_mod, v, logw, a_sig, g, kk, bonus


def _group_norm_out(y, bonus, g, p):
    mean = _seg_sum(y, p['seg']) * (1.0 / HS)
    d = y - mean
    var = _seg_sum(d * d, p['seg']) * (1.0 / HS)
    yn = d * lax.rsqrt(var + GN_EPS) * p['ln_w'] + p['ln_b']
    return (yn + bonus) * g


def _rwkv_kernel(rw_ref, shift_ref, m0_ref, *rest, n_valid, nbat):
    prm_refs = rest[:len(RW_PARAM_NAMES)]
    tri_ref, ones_ref, o_ref, mout_ref, prev_scr, m_scr = rest[len(RW_PARAM_NAMES):]
    p = {n: ref[...] for n, ref in zip(RW_PARAM_NAMES, prm_refs)}
    chunk = HS
    ngrp = HEADS // GROUP
    c = pl.program_id(1)
    nc = pl.num_programs(1)

    @pl.when(c == 0)
    def _():
        for bi in range(nbat):
            prev_scr[bi] = shift_ref[0]
            m_scr[bi] = m0_ref[0]

    curs, prevs = [], []
    for bi in range(nbat):
        cur_b = rw_ref[bi]
        rowi = lax.broadcasted_iota(jnp.int32, cur_b.shape, 0)
        prevs.append(jnp.where(rowi == 0, prev_scr[bi], pltpu.roll(cur_b, 1, 0)))
        prev_scr[bi] = cur_b[chunk - 1:chunk, :]
        curs.append(cur_b)
    r, k_mod, v, logw, a_sig, g, kk, bonus = _rwkv_token_prep(
        jnp.concatenate(curs, axis=0), jnp.concatenate(prevs, axis=0), p)

    if n_valid is not None:
        step = lax.broadcasted_iota(jnp.int32, (nbat * chunk, RW_W), 0) % chunk
        live = (c * chunk + step) < n_valid
        logw = jnp.where(live, logw, 0.0)
        kk = jnp.where(live, kk, 0.0)
        k_mod = jnp.where(live, k_mod, 0.0)
        v = jnp.where(live, v, 0.0)

    lw3 = _split3(logw)
    seq_rows = [slice(bi * chunk, (bi + 1) * chunk) for bi in range(nbat)]
    gcum = jnp.concatenate([sum(_dot(tri_ref[...], t[rs]) for t in lw3) for rs in seq_rows], axis=0)
    g_end = jnp.concatenate([jnp.broadcast_to(gcum[rs][chunk - 1:chunk, :], (chunk, RW_W)) for rs in seq_rows], axis=0)
    r_t = r * jnp.exp(gcum)
    a_t = -kk * jnp.exp(gcum - logw)
    eng = jnp.exp(-gcum)
    b_t = kk * a_sig * eng
    k_t = k_mod * eng
    tail = jnp.exp(g_end - gcum)
    b_e = kk * a_sig * tail
    k_e = k_mod * tail
    decay_end = jnp.exp(g_end)

    ti = lax.broadcasted_iota(jnp.int32, (chunk, GW), 0)
    lane = lax.broadcasted_iota(jnp.int32, (chunk, GW), 1)
    si = lane % HS
    lane_head = lane // HS
    strict = si < ti
    incl = si <= ti
    eye = (si == ti).astype(F32)

    def bdr(zf):
        zb = zf.astype(BF16)
        return jnp.concatenate([jnp.where(lane_head == hh, zb, jnp.zeros_like(zb)) for hh in range(GROUP)], axis=0)

    streams = [(bi, gi) for bi in range(nbat) for gi in range(ngrp)]
    blk = lambda t, s: t[s[0] * chunk:(s[0] + 1) * chunk, s[1] * GW:(s[1] + 1) * GW]
    cat0 = lambda xs: jnp.concatenate(xs, axis=0).astype(BF16)
    cat1 = lambda xs: jnp.concatenate(xs, axis=1).astype(BF16)

    ar = [cat0([blk(a_t, s), blk(r_t, s)]) for s in streams]
    ab = [_dot_nt(ar[i], bdr(blk(b_t, s))) for i, s in enumerate(streams)]
    ak = [_dot_nt(ar[i], bdr(blk(k_t, s))) for i, s in enumerate(streams)]
    n = [jnp.where(strict, t[0:chunk], 0.0) for t in ab]
    a_rb = [jnp.where(incl, t[chunk:2 * chunk], 0.0) for t in ab]
    a_ak = [jnp.where(strict, t[0:chunk], 0.0) for t in ak]
    a_rk = [jnp.where(incl, t[chunk:2 * chunk], 0.0) for t in ak]

    x = [eye + t for t in n]
    pw = [_dot(t.astype(BF16), bdr(t)) for t in n]
    for _ in range(4):
        z = [_dot(cat0([xi, pi]), bdr(pi)) for xi, pi in zip(x, pw)]
        x = [xi + zi[0:chunk] for xi, zi in zip(x, z)]
        pw = [zi[chunk:2 * chunk] for zi in z]
    z = [_dot(xi.astype(BF16), bdr(pi)) for xi, pi in zip(x, pw)]
    x = [xi + zi for xi, zi in zip(x, z)]

    m0 = [m_scr[s[0], s[1]] for s in streams]
    bm = [bdr(t) for t in m0]
    bv = [bdr(blk(v, s)) for s in streams]
    rhs = [_dot(cat1([blk(a_t, s), a_ak[i]]), jnp.concatenate([bm[i], bv[i]], axis=0)) for i, s in enumerate(streams)]
    u = [_dot(xi.astype(BF16), bdr(ri)) for xi, ri in zip(x, rhs)]
    y = [_dot(cat1([blk(r_t, s), a_rb[i], a_rk[i]]), jnp.concatenate([bm[i], bdr(u[i]), bv[i]], axis=0))
         for i, s in enumerate(streams)]
    full = [_dot_tn(cat0([blk(b_e, s), blk(k_e, s)]), cat0([u[i], blk(v, s)])) for i, s in enumerate(streams)]
    dg = jnp.concatenate([eye * blk(decay_end, s)[0:1] for s in streams], axis=0)
    gm_all = sum(_dot(t, ones_ref[...]) for t in _split3(dg))
    gm = [gm_all[i * chunk:(i + 1) * chunk] for i in range(len(streams))]
    for i, s in enumerate(streams):
        upd = jnp.where(lane_head == 0, full[i][0:HS], 0.0)
        for hh in range(1, GROUP):
            upd = upd + jnp.where(lane_head == hh, full[i][hh * HS:(hh + 1) * HS], 0.0)
        m_scr[s[0], s[1]] = gm[i] * m0[i] + upd

    y_all = jnp.concatenate([jnp.concatenate([y[bi * ngrp + gi] for gi in range(ngrp)], axis=1)
                             for bi in range(nbat)], axis=0)
    out = _group_norm_out(y_all, bonus, g, p).astype(o_ref.dtype)
    for bi in range(nbat):
        o_ref[bi] = out[seq_rows[bi]]

    @pl.when(c == nc - 1)
    def _():
        mout_ref[...] = m_scr[...]


def _state_to_m(s):
    b = s.shape[0]
    return jnp.transpose(s.reshape(b, HEADS // GROUP, GROUP, HS, HS), (0, 1, 4, 2, 3)).reshape(b, HEADS // GROUP, HS, GW)


def _m_to_state(m):
    b = m.shape[0]
    return jnp.transpose(m.reshape(b, HEADS // GROUP, HS, GROUP, HS), (0, 1, 3, 4, 2)).reshape(b, HEADS, HS, HS)


def _rwkv(rw, shift0, s0, wts, batch, n_valid):
    chunk = HS
    rows = rw.shape[0]
    seq = rows // batch
    nc = seq // chunk
    nbat = RWKV_BATCHES_PER_STEP if batch % RWKV_BATCHES_PER_STEP == 0 else 1
    const2 = lambda b, c: (0, 0)
    consts = [wts[n] for n in RW_PARAM_NAMES]
    tri = (jnp.arange(chunk)[:, None] >= jnp.arange(chunk)[None, :]).astype(BF16)
    hd = jnp.arange(GW) // HS
    ones_bd = (hd[:, None] == hd[None, :]).astype(BF16)
    m0 = _state_to_m(s0)
    ngrp = HEADS // GROUP
    kern = functools.partial(_rwkv_kernel, n_valid=n_valid, nbat=nbat)
    o, m = pl.pallas_call(
        kern,
        grid=(batch // nbat, nc),
        in_specs=[pl.BlockSpec((nbat, chunk, RW_COLS), lambda b, c: (b, c, 0)),
                  pl.BlockSpec((1, 1, RW_COLS), lambda b, c: (0, 0, 0)),
                  pl.BlockSpec((1, ngrp, HS, GW), lambda b, c: (0, 0, 0, 0))]
        + [pl.BlockSpec(a.shape, const2) for a in consts]
        + [pl.BlockSpec(tri.shape, const2), pl.BlockSpec(ones_bd.shape, const2)],
        out_specs=[pl.BlockSpec((nbat, chunk, RW_W), lambda b, c: (b, c, 0)),
                   pl.BlockSpec((nbat, ngrp, HS, GW), lambda b, c: (b, 0, 0, 0))],
        out_shape=[jax.ShapeDtypeStruct((batch, seq, RW_W), BF16),
                   jax.ShapeDtypeStruct((batch, ngrp, HS, GW), F32)],
        scratch_shapes=[pltpu.VMEM((nbat, 1, RW_COLS), F32), pltpu.VMEM((nbat, ngrp, HS, GW), F32)],
        compiler_params=_params(("parallel", "arbitrary")),
        name="rwkv",
    )(rw.reshape(batch, seq, RW_COLS), shift0, m0, *consts, tri, ones_bd)
    return o.reshape(rows, RW_W), _m_to_state(m)


def _rwkv_step_kernel(rw_ref, shift_ref, s_ref, *rest):
    prm_refs = rest[:len(RW_PARAM_NAMES)]
    o_ref, sout_ref, t_scr, y_scr, tok_scr = rest[len(RW_PARAM_NAMES):]
    h = pl.program_id(0)

    @pl.when(h == 0)
    def _():
        p = {n: ref[...] for n, ref in zip(RW_PARAM_NAMES, prm_refs)}
        r, k_mod, v, logw, a_sig, g, kk, bonus = _rwkv_token_prep(rw_ref[...], shift_ref[...], p)
        t_scr[0] = r.T
        t_scr[1] = jnp.exp(logw).T
        t_scr[2] = k_mod.T
        t_scr[3] = v.T
        t_scr[4] = (-kk).T
        t_scr[5] = (kk * a_sig).T
        tok_scr[0] = bonus
        tok_scr[1] = g

    rows = pl.ds(pl.multiple_of(h * HS, HS), HS)
    r_h, w_h, k_h = t_scr[0, rows, :], t_scr[1, rows, :], t_scr[2, rows, :]
    a_h, b_h = t_scr[4, rows, :], t_scr[5, rows, :]

    def body(vv, carry):
        sv = s_ref[0, vv]
        vrow = t_scr[3, pl.ds(h * HS + vv, 1), :]
        sa = jnp.sum(sv * a_h, axis=0, keepdims=True)
        sn = sv * w_h + sa * b_h + vrow * k_h
        sout_ref[0, vv] = sn
        y_scr[pl.ds(h * HS + vv, 1), :] = jnp.sum(sn * r_h, axis=0, keepdims=True)
        return carry
    lax.fori_loop(0, HS, body, 0)

    @pl.when(h == HEADS - 1)
    def _():
        p = {n: ref[...] for n, ref in zip(RW_PARAM_NAMES, prm_refs)}
        o_ref[...] = _group_norm_out(y_scr[...].T, tok_scr[0], tok_scr[1], p).astype(o_ref.dtype)


def _rwkv_step(rw, shift, s_t, wts):
    nb = rw.shape[0]
    consts = [wts[n] for n in RW_PARAM_NAMES]
    const2 = lambda h: (0, 0)
    return pl.pallas_call(
        _rwkv_step_kernel,
        grid=(HEADS,),
        in_specs=[pl.BlockSpec((nb, RW_COLS), const2), pl.BlockSpec((nb, RW_COLS), const2),
                  pl.BlockSpec((1, HS, HS, nb), lambda h: (h, 0, 0, 0))]
        + [pl.BlockSpec(a.shape, const2) for a in consts],
        out_specs=[pl.BlockSpec((nb, RW_W), const2), pl.BlockSpec((1, HS, HS, nb), lambda h: (h, 0, 0, 0))],
        out_shape=[jax.ShapeDtypeStruct((nb, RW_W), BF16), jax.ShapeDtypeStruct(s_t.shape, F32)],
        scratch_shapes=[pltpu.VMEM((6, RW_W, nb), F32), pltpu.VMEM((RW_W, nb), F32), pltpu.VMEM((2, nb, RW_W), F32)],
        compiler_params=_params(("arbitrary",)),
        name="rwkv_step",
    )(rw, shift, s_t, *consts)


def _mlp_kernel(x_ref, om_ref, or_ref, gs_ref, wom_ref, wor_ref, wout_ref, gffn_ref, wup_ref, wdn_ref, gfin_ref,
                y_ref):
    a = _dot(om_ref[...], wom_ref[...])
    r = _dot(or_ref[...], wor_ref[...])
    mix = gs_ref[:, 0:D_MODEL] * a + gs_ref[:, D_MODEL:2 * D_MODEL] * r
    x1 = x_ref[...] + _dot(mix.astype(BF16), wout_ref[...])
    h2 = _rms(x1, gffn_ref[...]).astype(BF16)
    u = jnp.maximum(_dot(h2, wup_ref[...]), 0.0)
    x2 = x1 + _dot((u * u).astype(BF16), wdn_ref[...])
    y_ref[...] = _rms(x2, gfin_ref[...])


def _mlp(x, om, orw, gsig, wts, tm):
    rows = x.shape[0]
    row = lambda i: (i, 0)
    const = lambda i: (0, 0)
    full = lambda a: pl.BlockSpec(a.shape, const, pipeline_mode=pl.Buffered(1))
    names = ['w_o_mla', 'w_o_rwkv', 'w_out', 'g_ffn', 'w_up', 'w_down', 'g_final']
    return pl.pallas_call(
        _mlp_kernel,
        grid=(rows // tm,),
        in_specs=[pl.BlockSpec((tm, D_MODEL), row), pl.BlockSpec((tm, HEADS * VDIM), row),
                  pl.BlockSpec((tm, RW_W), row), pl.BlockSpec((tm, 2 * D_MODEL), row)]
        + [full(wts[n]) for n in names],
        out_specs=pl.BlockSpec((tm, D_MODEL), row),
        out_shape=jax.ShapeDtypeStruct((rows, D_MODEL), F32),
        compiler_params=_params(("parallel",)),
        name="mlp",
    )(x, om, orw, gsig, *[wts[n] for n in names])


def _prep_weights(g_final, g_mix, w_in, g_q, w_uq, g_kv, w_uk, w_uv, w_o_mla, mu_shift, w0, w2, a0, a2, g2,
                  k_k, k_a, r_k, ln_w, ln_b, w_o_rwkv, w_out, g_ffn, w_up, w_down):
    row = lambda a: a.reshape(1, -1).astype(F32)
    q_in = w_in[:, 0:Q_LORA]
    kv_in = w_in[:, Q_LORA:Q_LORA + KV_LORA]
    kr = w_in[:, Q_LORA + KV_LORA:Q_LORA + KV_LORA + ROPE]
    o_rw = Q_LORA + KV_LORA + ROPE
    rw = w_in[:, o_rw:o_rw + RW_COLS]
    gate = w_in[:, o_rw + RW_COLS:]
    zr = jnp.zeros((D_MODEL, ROPE), F32)
    kr_a = jnp.concatenate([kr, zr, kr, zr], axis=1)
    w_in_p = jnp.concatenate([gate, rw, q_in, kv_in, kr_a], axis=1).astype(BF16)

    uq = w_uq.reshape(Q_LORA, HEADS, NOPE + ROPE)
    zq = jnp.zeros((Q_LORA, HEADS, HP - NOPE - ROPE), F32)
    w_uq_p = jnp.concatenate([uq, zq], axis=-1).reshape(Q_LORA, HEADS * HP).astype(BF16)

    uk = w_uk.reshape(KV_LORA, HEADS, NOPE)
    uk_pad = jnp.concatenate([uk, jnp.zeros((KV_LORA, HEADS, HP - NOPE), F32)], axis=-1).reshape(KV_LORA, HEADS * HP)
    w_ukv = uk_pad.astype(BF16)

    head = jnp.arange(RW_W) // HS
    seg = (head[:, None] == head[None, :]).astype(BF16)
    return dict(
        g_mix=row(g_mix), w_in=w_in_p, g_q=row(g_q), w_uq=w_uq_p, g_kv=row(g_kv), w_ukv=w_ukv,
        w_uk=w_uk.astype(BF16), w_uv=w_uv.astype(BF16), w_uvt=w_uv.T.astype(BF16), w_o_mla=w_o_mla.astype(BF16),
        mu_shift=row(mu_shift), w0=row(w0), w2=w2.astype(BF16), a0=row(a0), a2=a2.astype(BF16),
        g2=g2.astype(BF16), k_k=row(k_k), k_a=row(k_a), r_k=row(r_k), ln_w=row(ln_w), ln_b=row(ln_b), seg=seg,
        w_o_rwkv=w_o_rwkv.astype(BF16), w_out=w_out.astype(BF16), g_ffn=row(g_ffn),
        w_up=w_up.astype(BF16), w_down=w_down.astype(BF16), g_final=row(g_final))


def _rope_tables(pos):
    inv = ROPE_BASE ** (-jnp.arange(0, ROPE, 2, dtype=F32) / ROPE)
    ang = pos.astype(F32)[:, None] * inv[None, :]
    cos, sin = jnp.cos(ang), jnp.sin(ang)
    n = pos.shape[0]
    cc = jnp.concatenate([cos, cos], axis=1)
    ss = jnp.concatenate([-sin, sin], axis=1)
    z32 = jnp.zeros((n, ROPE), F32)
    ctq = jnp.concatenate([jnp.ones((n, NOPE), F32), cc, z32], axis=1)
    stq = jnp.concatenate([jnp.zeros((n, NOPE), F32), ss, z32], axis=1)
    ctk = jnp.concatenate([cc, z32, cc, z32], axis=1)
    stk = jnp.concatenate([ss, z32, ss, z32], axis=1)
    return ctq, stq, ctk, stk


def _row_tile(rows, cap):
    tm = min(rows, cap)
    assert rows % tm == 0
    return tm


def kernel(x_prompt, x_sample, cache_kv, page_table, state_wkv, state_shift, meta_tokens, g_final, g_mix, w_in, g_q, w_uq, g_kv, w_uk, w_uv, w_o_mla, mu_shift, w0, w2, a0, a2, g2, k_k, k_a, r_k, ln_w, ln_b, w_o_rwkv, w_out, g_ffn, w_up, w_down):
    depth = g_mix.shape[0]
    assert depth == 1
    bp, seq, _ = x_prompt.shape
    bd, s_s, _ = x_sample.shape
    assert s_s == 1
    n_pages = page_table.shape[1]
    past_len = n_pages * PAGE
    wts = _prep_weights(g_final, g_mix[0], w_in[0], g_q[0], w_uq[0], g_kv[0], w_uk[0], w_uv[0], w_o_mla[0],
                        mu_shift[0], w0[0], w2[0], a0[0], a2[0], g2[0], k_k[0], k_a[0], r_k[0], ln_w[0], ln_b[0],
                        w_o_rwkv[0], w_out[0], g_ffn[0], w_up[0], w_down[0])

    tab_m = _rope_tables(jnp.arange(N_META))
    _, rw_m, _, kvrow_m, kpad_m, vt_m = _proj(meta_tokens.astype(F32), tab_m, wts, N_META, BF16)
    rw_m_pad = jnp.concatenate([rw_m, jnp.zeros((HS - N_META, RW_COLS), F32)], axis=0)
    zero_shift = jnp.zeros((1, 1, RW_COLS), F32)
    zero_state = jnp.zeros((1, HEADS, HS, HS), F32)
    _, s_meta = _rwkv(rw_m_pad, zero_shift, zero_state, wts, 1, N_META)
    shift_meta = rw_m[N_META - 1:N_META].reshape(1, 1, RW_COLS)

    rows_p = bp * seq
    tm = _row_tile(seq, ROW_TILE)
    tq = _row_tile(seq, KV_TILE)
    xp = x_prompt.reshape(rows_p, D_MODEL)
    ctq, stq, ctk, stk = _rope_tables(N_META + jnp.arange(seq))
    tab_p = (ctq * Q_PRESCALE, stq * Q_PRESCALE, ctk, stk)
    gsig, rw, q, kvrow, kpad, vt = _proj(xp, tab_p, wts, tm, BF16)
    o_mla = _attn(q, kpad, vt, kpad_m, vt_m, bp, seq, tq)
    o_rwkv, s_p = _rwkv(rw, shift_meta, s_meta, wts, bp, None)
    y_prompt = _mlp(xp, o_mla, o_rwkv, gsig, wts, tm).reshape(bp, seq, D_MODEL)
    kv_prompt = jnp.concatenate([jnp.broadcast_to(kvrow_m[None], (bp, N_META, KV_W)),
                                 kvrow.reshape(bp, seq, KV_W)], axis=1)[None]
    shift_prompt = rw.reshape(bp, seq, RW_COLS)[:, -1][None]

    xs = x_sample.reshape(bd, D_MODEL)
    tab_s = _rope_tables(jnp.full((bd,), past_len, jnp.int32))
    gsig_s, rw_s, q_s, kvrow_s, _, _ = _proj(xs, tab_s, wts, bd, F32)
    qf = jnp.transpose(_qlat(q_s, wts['w_uk']), (1, 0, 2))
    cache = jnp.swapaxes(cache_kv.reshape(cache_kv.shape[1], PAGE, KV_W), 1, 2)
    o_mla_s = _sattn(page_table, qf, kvrow_s.reshape(bd, 1, KV_W), wts['w_uv'], cache).reshape(bd, HEADS * VDIM)
    s_t = jnp.transpose(state_wkv[0], (1, 2, 3, 0))
    o_rwkv_s, s_t_new = _rwkv_step(rw_s, state_shift[0], s_t, wts)
    s_s_new = jnp.transpose(s_t_new, (3, 0, 1, 2))
    y_sample = _mlp(xs, o_mla_s, o_rwkv_s, gsig_s, wts, bd).reshape(bd, 1, D_MODEL)

    return (y_prompt, y_sample,
            kv_prompt.astype(cache_kv.dtype), s_p[None].astype(state_wkv.dtype),
            shift_prompt.astype(state_shift.dtype),
            kvrow_s.reshape(1, bd, 1, KV_W).astype(cache_kv.dtype), s_s_new[None].astype(state_wkv.dtype),
            rw_s[None].astype(state_shift.dtype))
```

```python
import functools

import jax
import jax.numpy as jnp
from jax import lax
from jax.experimental import pallas as pl
from jax.experimental.pallas import tpu as pltpu

F32 = jnp.float32
BF16 = jnp.bfloat16

D_MODEL = 1024
N_META = 16
PAGE = 128
HEADS = 8
NOPE = 64
ROPE = 32
VDIM = 64
Q_LORA = 384
KV_LORA = 256
KV_W = KV_LORA + ROPE
ROPE_BASE = 10000.0
SM_SCALE = (NOPE + ROPE) ** -0.5
HS = 64
RW_W = HEADS * HS
W_LORA = 64
A_LORA = 64
G_LORA = 128
RW_COLS = 3 * RW_W + W_LORA + A_LORA + G_LORA
GN_EPS = 64e-5
D_FF = 4 * D_MODEL
NORM_EPS = 1e-6
NEG_INF = -1e30
DECAY_SCALE = 0.6065306597126334
Q_PRESCALE = SM_SCALE * 1.4426950408889634

LANE = 128
HP = 128
VMEM_LIMIT = 56 * 1024 * 1024
KV_TILE = 256
ROW_TILE = 512

C_GATE = 0
C_RW = 2 * D_MODEL
C_QIN = C_RW + RW_COLS
C_KVIN = C_QIN + Q_LORA
C_KR = C_KVIN + KV_LORA
IN_PERM = C_KR + LANE


def _dot(a, b):
    return jnp.dot(a, b, preferred_element_type=F32)


def _dot_nt(a, b):
    return lax.dot_general(a, b, (((1,), (1,)), ((), ())), preferred_element_type=F32)


def _dot_tn(a, b):
    return lax.dot_general(a, b, (((0,), (0,)), ((), ())), preferred_element_type=F32)


def _split2(x):
    hi = x.astype(BF16)
    lo = (x - hi.astype(F32)).astype(BF16)
    return hi, lo


def _split3(x):
    x1 = x.astype(BF16)
    r1 = x - x1.astype(F32)
    x2 = r1.astype(BF16)
    x3 = (r1 - x2.astype(F32)).astype(BF16)
    return x1, x2, x3


def _rms(x, g):
    return x * lax.rsqrt(jnp.mean(x * x, axis=-1, keepdims=True) + NORM_EPS) * g


def _params(sem):
    return pltpu.CompilerParams(dimension_semantics=sem, vmem_limit_bytes=VMEM_LIMIT)


def _proj_kernel(x_ref, gmix_ref, win_ref, gq_ref, wuq_ref, gkv_ref, wukv_ref, wuvt_ref,
                 ctq_ref, stq_ref, ctk_ref, stk_ref,
                 gsig_ref, rw_ref, q_ref, kvrow_ref, kpad_ref, vt_ref):
    h = _rms(x_ref[...], gmix_ref[...]).astype(BF16)
    lane = lax.broadcasted_iota(jnp.int32, (x_ref.shape[0], LANE), 1)
    first_half = (lane % ROPE) < ROPE // 2

    def rope(t, cos_t, sin_t):
        swapped = jnp.where(first_half, pltpu.roll(t, LANE - ROPE // 2, 1), pltpu.roll(t, ROPE // 2, 1))
        return t * cos_t + swapped * sin_t

    zq = _dot(h, win_ref[:, C_QIN:C_QIN + Q_LORA])
    zkv = _dot(h, win_ref[:, C_KVIN:C_KVIN + KV_LORA])
    zkr = _dot(h, win_ref[:, C_KR:C_KR + LANE])
    step = 512
    for c in range(0, 2 * D_MODEL, step):
        gsig_ref[:, c:c + step] = jax.nn.sigmoid(_dot(h, win_ref[:, C_GATE + c:C_GATE + c + step])).astype(BF16)

    qn = _rms(zq, gq_ref[...]).astype(BF16)
    ckv = _rms(zkv, gkv_ref[...])
    cb = ckv.astype(BF16)
    qa = _dot(qn, wuq_ref[...])
    kn = _dot(cb, wukv_ref[...])
    tw = vt_ref.shape[2]
    for sub in range(vt_ref.shape[0]):
        vt_ref[sub] = _dot_nt(wuvt_ref[...], cb[sub * tw:(sub + 1) * tw]).astype(BF16)

    for c in range(0, RW_COLS, 256):
        rw_ref[:, c:c + 256] = _dot(h, win_ref[:, C_RW + c:C_RW + c + 256])

    ctq = ctq_ref[...]
    stq = stq_ref[...]
    for hh in range(HEADS):
        hs = slice(hh * HP, (hh + 1) * HP)
        q_ref[:, hs] = rope(qa[:, hs], ctq, stq).astype(q_ref.dtype)
    kvrow_ref[:, 0:KV_LORA] = ckv
    kr = rope(zkr, ctk_ref[...], stk_ref[...])
    kvrow_ref[:, KV_LORA:KV_W] = kr[:, 0:ROPE]
    kr_head = jnp.where((lane >= NOPE) & (lane < NOPE + ROPE), kr, 0.0)
    for hh in range(HEADS):
        hs = slice(hh * HP, (hh + 1) * HP)
        kpad_ref[:, hs] = (kn[:, hs] + kr_head).astype(BF16)


def _proj(x, tables, wts, tm, q_dtype):
    rows = x.shape[0]
    nt = tables[0].shape[0] // tm
    tw = min(tm, KV_TILE)
    row = lambda i: (i, 0)
    const = lambda i: (0, 0)
    tab = lambda i: (i % nt, 0)
    full = lambda a: pl.BlockSpec(a.shape, const, pipeline_mode=pl.Buffered(1))
    in_specs = [pl.BlockSpec((tm, D_MODEL), row), full(wts['g_mix']), full(wts['w_in']), full(wts['g_q']),
                full(wts['w_uq']), full(wts['g_kv']), full(wts['w_ukv']), full(wts['w_uvt'])]
    in_specs += [pl.BlockSpec((tm, LANE), tab)] * 4
    widths = [(2 * D_MODEL, BF16), (RW_COLS, F32), (HEADS * HP, q_dtype), (KV_W, F32), (HEADS * HP, BF16)]
    nt_rows = rows // tm
    return pl.pallas_call(
        _proj_kernel,
        grid=(nt_rows,),
        in_specs=in_specs,
        out_specs=[pl.BlockSpec((tm, w), row) for w, _ in widths]
        + [pl.BlockSpec((tm // tw, HEADS * VDIM, tw), lambda i: (i, 0, 0))],
        out_shape=[jax.ShapeDtypeStruct((rows, w), dt) for w, dt in widths]
        + [jax.ShapeDtypeStruct((rows // tw, HEADS * VDIM, tw), BF16)],
        compiler_params=_params(("parallel",)),
        name="proj",
    )(x, wts['g_mix'], wts['w_in'], wts['g_q'], wts['w_uq'], wts['g_kv'], wts['w_ukv'], wts['w_uvt'], *tables)


def _attn_kernel(q_ref, k_ref, vt_ref, km_ref, vmt_ref, eye_ref, o_ref, *, tq):
    i = pl.program_id(1)
    qs = [q_ref[:, hh * HP:(hh + 1) * HP] for hh in range(HEADS)]

    def update(carry, groups):
        ss = [[_dot_nt(k_tiles[hh], qs[hh]) for hh in range(HEADS)] for k_tiles, _, _ in groups]
        ps, stats = [], []
        for hh in range(HEADS):
            m, l, _ = carry[hh]
            sm = [s_g[hh] if g[2] is None else jnp.where(g[2], s_g[hh], NEG_INF) for s_g, g in zip(ss, groups)]
            m_new = m
            for s in sm:
                m_new = jnp.maximum(m_new, jnp.max(s, axis=0, keepdims=True))
            alpha = jnp.exp2(m - m_new)
            p = [jnp.exp2(s - m_new) for s in sm]
            l = l * alpha
            for pg in p:
                l = l + jnp.sum(pg, axis=0, keepdims=True)
            stats.append((m_new, l, alpha))
            ps.append([pg.astype(BF16) for pg in p])
        pv = [[_dot(g[1][hh], ps[hh][gi]) for hh in range(HEADS)] for gi, g in enumerate(groups)]
        out = []
        for hh in range(HEADS):
            acc = carry[hh][2] * stats[hh][2]
            for gi in range(len(groups)):
                acc = acc + pv[gi][hh]
            out.append((stats[hh][0], stats[hh][1], acc))
        return tuple(out)

    def kv_tile(j, mask):
        rows = pl.ds(pl.multiple_of(j * tq, tq), tq)
        return ([k_ref[rows, hh * HP:(hh + 1) * HP] for hh in range(HEADS)],
                [vt_ref[j, hh * VDIM:(hh + 1) * VDIM, :] for hh in range(HEADS)], mask)

    init = (jnp.full((1, tq), NEG_INF, F32), jnp.zeros((1, tq), F32), jnp.zeros((VDIM, tq), F32))
    carry = lax.fori_loop(0, i, lambda j, c: update(c, [kv_tile(j, None)]), (init,) * HEADS)
    key = lax.broadcasted_iota(jnp.int32, (tq, tq), 0)
    qry = lax.broadcasted_iota(jnp.int32, (tq, tq), 1)
    meta = ([km_ref[:, hh * HP:(hh + 1) * HP] for hh in range(HEADS)],
            [vmt_ref[0, hh * VDIM:(hh + 1) * VDIM, :] for hh in range(HEADS)], None)
    carry = update(carry, [meta, kv_tile(i, key <= qry)])

    outs = []
    for hh in range(HEADS):
        m, l, acc = carry[hh]
        outs.append(_dot_tn((acc / l).astype(BF16), eye_ref[...]))
    o_ref[...] = jnp.concatenate(outs, axis=-1).astype(o_ref.dtype)


def _attn(q, kpad, vt, km, vmt, batch, seq, tq):
    nq = seq // tq
    eye = jnp.eye(VDIM, dtype=BF16)
    kern = functools.partial(_attn_kernel, tq=tq)
    return pl.pallas_call(
        kern,
        grid=(batch, nq),
        in_specs=[
            pl.BlockSpec((tq, HEADS * HP), lambda b, i: (b * nq + i, 0)),
            pl.BlockSpec((seq, HEADS * HP), lambda b, i: (b, 0)),
            pl.BlockSpec((nq, HEADS * VDIM, tq), lambda b, i: (b, 0, 0)),
            pl.BlockSpec(km.shape, lambda b, i: (0, 0)),
            pl.BlockSpec(vmt.shape, lambda b, i: (0, 0, 0)),
            pl.BlockSpec(eye.shape, lambda b, i: (0, 0)),
        ],
        out_specs=pl.BlockSpec((tq, HEADS * VDIM), lambda b, i: (b * nq + i, 0)),
        out_shape=jax.ShapeDtypeStruct((batch * seq, HEADS * VDIM), BF16),
        compiler_params=_params(("parallel", "arbitrary")),
        name="attn",
    )(q, kpad, vt, km, vmt, eye)


def _qlat_kernel(q_ref, wuk_ref, o_ref):
    for hh in range(HEADS):
        qn = q_ref[:, hh * HP:hh * HP + NOPE].astype(BF16)
        o_ref[hh, :, 0:KV_LORA] = _dot_nt(qn, wuk_ref[:, hh * NOPE:(hh + 1) * NOPE]) * SM_SCALE
        o_ref[hh, :, KV_LORA:KV_W] = q_ref[:, hh * HP + NOPE:hh * HP + NOPE + ROPE] * SM_SCALE


def _qlat(q, wuk):
    rows = q.shape[0]
    return pl.pallas_call(
        _qlat_kernel,
        out_shape=jax.ShapeDtypeStruct((HEADS, rows, KV_W), F32),
        name="qlat",
    )(q, wuk)


SATTN_SLOTS = 3
SATTN_CHUNK = 1024


def _sattn_kernel(pt_ref, qf_ref, kvn_ref, wuv_ref, cache_ref, o_ref, buf, sem, *, n_pages, chunk):
    b = pl.program_id(0)
    nb = pl.num_programs(0)
    ahead = SATTN_SLOTS - 1

    def page_copy(page, slot, j):
        return pltpu.make_async_copy(cache_ref.at[page], buf.at[slot, :, pl.ds(j * PAGE, PAGE)], sem.at[slot])

    def start_pages(bb, slot, lo, hi):
        for j in range(lo, hi):
            page_copy(pt_ref[bb, j], slot, j).start(priority=j % 2)

    def wait_slot(slot):
        for j in range(n_pages):
            page_copy(0, slot, j).wait()

    @pl.when(b == 0)
    def _():
        for first in range(ahead):
            start_pages(jnp.minimum(first, nb - 1), first, 0, n_pages)

    slot = b % SATTN_SLOTS
    wait_slot(slot)
    nxt = jnp.minimum(b + ahead, nb - 1)
    nslot = (b + ahead) % SATTN_SLOTS
    n_chunks = (n_pages * PAGE) // chunk
    per_chunk = n_pages // n_chunks

    qf = qf_ref[0].astype(BF16)
    kvn = kvn_ref[0].astype(BF16).astype(F32)
    s_own = jnp.sum(qf.astype(F32) * kvn, axis=-1, keepdims=True)
    kts, ss = [], []
    for ci in range(n_chunks):
        kt = buf[slot, :, ci * chunk:(ci + 1) * chunk].astype(BF16)
        kts.append(kt)
        ss.append(_dot(qf, kt))
        start_pages(nxt, nslot, ci * per_chunk, (ci + 1) * per_chunk)
    m = s_own
    for sc in ss:
        m = jnp.maximum(m, jnp.max(sc, axis=-1, keepdims=True))
    ps = [jnp.exp(sc - m) for sc in ss]
    p_own = jnp.exp(s_own - m)
    l = p_own
    for pc in ps:
        l = l + jnp.sum(pc, axis=-1, keepdims=True)
    acc = p_own.astype(BF16).astype(F32) * kvn[:, 0:KV_LORA]
    for pc, kt in zip(ps, kts):
        acc = acc + _dot_nt(pc.astype(BF16), kt[0:KV_LORA, :])

    @pl.when(b == nb - 1)
    def _():
        for extra in range(1, SATTN_SLOTS):
            wait_slot((b + extra) % SATTN_SLOTS)

    o_lat = (acc / l).astype(BF16)
    full = _dot(o_lat, wuv_ref[...])
    r = lax.broadcasted_iota(jnp.int32, full.shape, 0)
    cidx = lax.broadcasted_iota(jnp.int32, full.shape, 1)
    o_ref[0] = jnp.sum(jnp.where(cidx // VDIM == r, full, 0.0), axis=0, keepdims=True).astype(o_ref.dtype)


def _sattn(page_table, qf, kvn, wuv, cache):
    nb, n_pages = page_table.shape
    assert n_pages % 2 == 0
    chunk = min(SATTN_CHUNK, n_pages * PAGE)
    assert (n_pages * PAGE) % chunk == 0
    kern = functools.partial(_sattn_kernel, n_pages=n_pages, chunk=chunk)
    grid_spec = pltpu.PrefetchScalarGridSpec(
        num_scalar_prefetch=1,
        grid=(nb,),
        in_specs=[
            pl.BlockSpec((1, HEADS, KV_W), lambda b, pt: (b, 0, 0)),
            pl.BlockSpec((1, 1, KV_W), lambda b, pt: (b, 0, 0)),
            pl.BlockSpec(wuv.shape, lambda b, pt: (0, 0)),
            pl.BlockSpec(memory_space=pl.ANY),
        ],
        out_specs=pl.BlockSpec((1, 1, HEADS * VDIM), lambda b, pt: (b, 0, 0)),
        scratch_shapes=[pltpu.VMEM((SATTN_SLOTS, KV_W, n_pages * PAGE), F32),
                        pltpu.SemaphoreType.DMA((SATTN_SLOTS,))],
    )
    return pl.pallas_call(
        kern,
        grid_spec=grid_spec,
        out_shape=jax.ShapeDtypeStruct((nb, 1, HEADS * VDIM), BF16),
        compiler_params=_params(("arbitrary",)),
        name="sattn",
    )(page_table, qf, kvn, wuv, cache)


GROUP = 4
RWKV_BATCHES_PER_STEP = 8
GW = GROUP * HS
RW_PARAM_NAMES = ['mu_shift', 'w0', 'w2', 'a0', 'a2', 'g2', 'k_k', 'k_a', 'r_k', 'ln_w', 'ln_b', 'seg']


def _seg_sum(x, seg):
    return _dot(x.astype(BF16), seg)


def _rwkv_token_prep(cur, prev, p):
    z = cur + (prev - cur) * p['mu_shift']
    r = z[:, 0:RW_W]
    k = z[:, RW_W:2 * RW_W]
    v = z[:, 2 * RW_W:3 * RW_W]
    w_in = z[:, 3 * RW_W:3 * RW_W + W_LORA]
    a_in = z[:, 3 * RW_W + W_LORA:3 * RW_W + W_LORA + A_LORA]
    g_in = z[:, 3 * RW_W + W_LORA + A_LORA:RW_COLS]
    w_log = p['w0'] + _dot(jnp.tanh(w_in).astype(BF16), p['w2'])
    logw = -DECAY_SCALE * jax.nn.sigmoid(w_log)
    a_sig = jax.nn.sigmoid(p['a0'] + _dot(a_in.astype(BF16), p['a2']))
    g = _dot(jax.nn.sigmoid(g_in).astype(BF16), p['g2'])
    kk = k * p['k_k']
    kk = kk * lax.rsqrt(jnp.maximum(_seg_sum(kk * kk, p['seg']), 1e-24))
    k_mod = k * (1.0 + (a_sig - 1.0) * p['k_a'])
    bonus = _seg_sum(r * k_mod * p['r_k'], p['seg']) * v
    return r, k_mod, v, logw, a_sig, g, kk, bonus


def _group_norm_out(y, bonus, g, p):
    mean = _seg_sum(y, p['seg']) * (1.0 / HS)
    d = y - mean
    var = _seg_sum(d * d, p['seg']) * (1.0 / HS)
    yn = d * lax.rsqrt(var + GN_EPS) * p['ln_w'] + p['ln_b']
    return (yn + bonus) * g


def _rwkv_kernel(rw_ref, shift_ref, m0_ref, *rest, n_valid, nbat):
    prm_refs = rest[:len(RW_PARAM_NAMES)]
    tri_ref, ones_ref, o_ref, mout_ref, prev_scr, m_scr = rest[len(RW_PARAM_NAMES):]
    p = {n: ref[...] for n, ref in zip(RW_PARAM_NAMES, prm_refs)}
    chunk = HS
    ngrp = HEADS // GROUP
    c = pl.program_id(1)
    nc = pl.num_programs(1)

    @pl.when(c == 0)
    def _():
        for bi in range(nbat):
            prev_scr[bi] = shift_ref[0]
            m_scr[bi] = m0_ref[0]

    curs, prevs = [], []
    for bi in range(nbat):
        cur_b = rw_ref[bi]
        rowi = lax.broadcasted_iota(jnp.int32, cur_b.shape, 0)
        prevs.append(jnp.where(rowi == 0, prev_scr[bi], pltpu.roll(cur_b, 1, 0)))
        prev_scr[bi] = cur_b[chunk - 1:chunk, :]
        curs.append(cur_b)
    r, k_mod, v, logw, a_sig, g, kk, bonus = _rwkv_token_prep(
        jnp.concatenate(curs, axis=0), jnp.concatenate(prevs, axis=0), p)

    if n_valid is not None:
        step = lax.broadcasted_iota(jnp.int32, (nbat * chunk, RW_W), 0) % chunk
        live = (c * chunk + step) < n_valid
        logw = jnp.where(live, logw, 0.0)
        kk = jnp.where(live, kk, 0.0)
        k_mod = jnp.where(live, k_mod, 0.0)
        v = jnp.where(live, v, 0.0)

    lw3 = _split3(logw)
    seq_rows = [slice(bi * chunk, (bi + 1) * chunk) for bi in range(nbat)]
    gcum = jnp.concatenate([sum(_dot(tri_ref[...], t[rs]) for t in lw3) for rs in seq_rows], axis=0)
    g_end = jnp.concatenate([jnp.broadcast_to(gcum[rs][chunk - 1:chunk, :], (chunk, RW_W)) for rs in seq_rows], axis=0)
    r_t = r * jnp.exp(gcum)
    a_t = -kk * jnp.exp(gcum - logw)
    eng = jnp.exp(-gcum)
    b_t = kk * a_sig * eng
    k_t = k_mod * eng
    tail = jnp.exp(g_end - gcum)
    b_e = kk * a_sig * tail
    k_e = k_mod * tail
    decay_end = jnp.exp(g_end)

    ti = lax.broadcasted_iota(jnp.int32, (chunk, GW), 0)
    lane = lax.broadcasted_iota(jnp.int32, (chunk, GW), 1)
    si = lane % HS
    lane_head = lane // HS
    strict = si < ti
    incl = si <= ti
    eye = (si == ti).astype(F32)

    def bdr(zf):
        zb = zf.astype(BF16)
        return jnp.concatenate([jnp.where(lane_head == hh, zb, jnp.zeros_like(zb)) for hh in range(GROUP)], axis=0)

    streams = [(bi, gi) for bi in range(nbat) for gi in range(ngrp)]
    blk = lambda t, s: t[s[0] * chunk:(s[0] + 1) * chunk, s[1] * GW:(s[1] + 1) * GW]
    cat0 = lambda xs: jnp.concatenate(xs, axis=0).astype(BF16)
    cat1 = lambda xs: jnp.concatenate(xs, axis=1).astype(BF16)

    ar = [cat0([blk(a_t, s), blk(r_t, s)]) for s in streams]
    ab = [_dot_nt(ar[i], bdr(blk(b_t, s))) for i, s in enumerate(streams)]
    ak = [_dot_nt(ar[i], bdr(blk(k_t, s))) for i, s in enumerate(streams)]
    n = [jnp.where(strict, t[0:chunk], 0.0) for t in ab]
    a_rb = [jnp.where(incl, t[chunk:2 * chunk], 0.0) for t in ab]
    a_ak = [jnp.where(strict, t[0:chunk], 0.0) for t in ak]
    a_rk = [jnp.where(incl, t[chunk:2 * chunk], 0.0) for t in ak]

    x = [eye + t for t in n]
    pw = [_dot(t.astype(BF16), bdr(t)) for t in n]
    for _ in range(4):
        z = [_dot(cat0([xi, pi]), bdr(pi)) for xi, pi in zip(x, pw)]
        x = [xi + zi[0:chunk] for xi, zi in zip(x, z)]
        pw = [zi[chunk:2 * chunk] for zi in z]
    z = [_dot(xi.astype(BF16), bdr(pi)) for xi, pi in zip(x, pw)]
    x = [xi + zi for xi, zi in zip(x, z)]

    m0 = [m_scr[s[0], s[1]] for s in streams]
    bm = [bdr(t) for t in m0]
    bv = [bdr(blk(v, s)) for s in streams]
    rhs = [_dot(cat1([blk(a_t, s), a_ak[i]]), jnp.concatenate([bm[i], bv[i]], axis=0)) for i, s in enumerate(streams)]
    u = [_dot(xi.astype(BF16), bdr(ri)) for xi, ri in zip(x, rhs)]
    y = [_dot(cat1([blk(r_t, s), a_rb[i], a_rk[i]]), jnp.concatenate([bm[i], bdr(u[i]), bv[i]], axis=0))
         for i, s in enumerate(streams)]
    full = [_dot_tn(cat0([blk(b_e, s), blk(k_e, s)]), cat0([u[i], blk(v, s)])) for i, s in enumerate(streams)]
    dg = jnp.concatenate([eye * blk(decay_end, s)[0:1] for s in streams], axis=0)
    gm_all = sum(_dot(t, ones_ref[...]) for t in _split3(dg))
    gm = [gm_all[i * chunk:(i + 1) * chunk] for i in range(len(streams))]
    for i, s in enumerate(streams):
        upd = jnp.where(lane_head == 0, full[i][0:HS], 0.0)
        for hh in range(1, GROUP):
            upd = upd + jnp.where(lane_head == hh, full[i][hh * HS:(hh + 1) * HS], 0.0)
        m_scr[s[0], s[1]] = gm[i] * m0[i] + upd

    y_all = jnp.concatenate([jnp.concatenate([y[bi * ngrp + gi] for gi in range(ngrp)], axis=1)
                             for bi in range(nbat)], axis=0)
    out = _group_norm_out(y_all, bonus, g, p).astype(o_ref.dtype)
    for bi in range(nbat):
        o_ref[bi] = out[seq_rows[bi]]

    @pl.when(c == nc - 1)
    def _():
        mout_ref[...] = m_scr[...]


def _state_to_m(s):
    b = s.shape[0]
    return jnp.transpose(s.reshape(b, HEADS // GROUP, GROUP, HS, HS), (0, 1, 4, 2, 3)).reshape(b, HEADS // GROUP, HS, GW)


def _m_to_state(m):
    b = m.shape[0]
    return jnp.transpose(m.reshape(b, HEADS // GROUP, HS, GROUP, HS), (0, 1, 3, 4, 2)).reshape(b, HEADS, HS, HS)


def _rwkv(rw, shift0, s0, wts, batch, n_valid):
    chunk = HS
    rows = rw.shape[0]
    seq = rows // batch
    nc = seq // chunk
    nbat = RWKV_BATCHES_PER_STEP if batch % RWKV_BATCHES_PER_STEP == 0 else 1
    const2 = lambda b, c: (0, 0)
    consts = [wts[n] for n in RW_PARAM_NAMES]
    tri = (jnp.arange(chunk)[:, None] >= jnp.arange(chunk)[None, :]).astype(BF16)
    hd = jnp.arange(GW) // HS
    ones_bd = (hd[:, None] == hd[None, :]).astype(BF16)
    m0 = _state_to_m(s0)
    ngrp = HEADS // GROUP
    kern = functools.partial(_rwkv_kernel, n_valid=n_valid, nbat=nbat)
    o, m = pl.pallas_call(
        kern,
        grid=(batch // nbat, nc),
        in_specs=[pl.BlockSpec((nbat, chunk, RW_COLS), lambda b, c: (b, c, 0)),
                  pl.BlockSpec((1, 1, RW_COLS), lambda b, c: (0, 0, 0)),
                  pl.BlockSpec((1, ngrp, HS, GW), lambda b, c: (0, 0, 0, 0))]
        + [pl.BlockSpec(a.shape, const2) for a in consts]
        + [pl.BlockSpec(tri.shape, const2), pl.BlockSpec(ones_bd.shape, const2)],
        out_specs=[pl.BlockSpec((nbat, chunk, RW_W), lambda b, c: (b, c, 0)),
                   pl.BlockSpec((nbat, ngrp, HS, GW), lambda b, c: (b, 0, 0, 0))],
        out_shape=[jax.ShapeDtypeStruct((batch, seq, RW_W), BF16),
                   jax.ShapeDtypeStruct((batch, ngrp, HS, GW), F32)],
        scratch_shapes=[pltpu.VMEM((nbat, 1, RW_COLS), F32), pltpu.VMEM((nbat, ngrp, HS, GW), F32)],
        compiler_params=_params(("parallel", "arbitrary")),
        name="rwkv",
    )(rw.reshape(batch, seq, RW_COLS), shift0, m0, *consts, tri, ones_bd)
    return o.reshape(rows, RW_W), _m_to_state(m)


def _rwkv_step_kernel(rw_ref, shift_ref, s_ref, *rest):
    prm_refs = rest[:len(RW_PARAM_NAMES)]
    o_ref, sout_ref, t_scr, y_scr, tok_scr = rest[len(RW_PARAM_NAMES):]
    h = pl.program_id(0)

    @pl.when(h == 0)
    def _():
        p = {n: ref[...] for n, ref in zip(RW_PARAM_NAMES, prm_refs)}
        r, k_mod, v, logw, a_sig, g, kk, bonus = _rwkv_token_prep(rw_ref[...], shift_ref[...], p)
        t_scr[0] = r.T
        t_scr[1] = jnp.exp(logw).T
        t_scr[2] = k_mod.T
        t_scr[3] = v.T
        t_scr[4] = (-kk).T
        t_scr[5] = (kk * a_sig).T
        tok_scr[0] = bonus
        tok_scr[1] = g

    rows = pl.ds(pl.multiple_of(h * HS, HS), HS)
    r_h, w_h, k_h = t_scr[0, rows, :], t_scr[1, rows, :], t_scr[2, rows, :]
    a_h, b_h = t_scr[4, rows, :], t_scr[5, rows, :]

    def body(vv, carry):
        sv = s_ref[0, vv]
        vrow = t_scr[3, pl.ds(h * HS + vv, 1), :]
        sa = jnp.sum(sv * a_h, axis=0, keepdims=True)
        sn = sv * w_h + sa * b_h + vrow * k_h
        sout_ref[0, vv] = sn
        y_scr[pl.ds(h * HS + vv, 1), :] = jnp.sum(sn * r_h, axis=0, keepdims=True)
        return carry
    lax.fori_loop(0, HS, body, 0)

    @pl.when(h == HEADS - 1)
    def _():
        p = {n: ref[...] for n, ref in zip(RW_PARAM_NAMES, prm_refs)}
        o_ref[...] = _group_norm_out(y_scr[...].T, tok_scr[0], tok_scr[1], p).astype(o_ref.dtype)


def _rwkv_step(rw, shift, s_t, wts):
    nb = rw.shape[0]
    consts = [wts[n] for n in RW_PARAM_NAMES]
    const2 = lambda h: (0, 0)
    return pl.pallas_call(
        _rwkv_step_kernel,
        grid=(HEADS,),
        in_specs=[pl.BlockSpec((nb, RW_COLS), const2), pl.BlockSpec((nb, RW_COLS), const2),
                  pl.BlockSpec((1, HS, HS, nb), lambda h: (h, 0, 0, 0))]
        + [pl.BlockSpec(a.shape, const2) for a in consts],
        out_specs=[pl.BlockSpec((nb, RW_W), const2), pl.BlockSpec((1, HS, HS, nb), lambda h: (h, 0, 0, 0))],
        out_shape=[jax.ShapeDtypeStruct((nb, RW_W), BF16), jax.ShapeDtypeStruct(s_t.shape, F32)],
        scratch_shapes=[pltpu.VMEM((6, RW_W, nb), F32), pltpu.VMEM((RW_W, nb), F32), pltpu.VMEM((2, nb, RW_W), F32)],
        compiler_params=_params(("arbitrary",)),
        name="rwkv_step",
    )(rw, shift, s_t, *consts)


def _mlp_kernel(x_ref, om_ref, or_ref, gs_ref, wom_ref, wor_ref, wout_ref, gffn_ref, wup_ref, wdn_ref, gfin_ref,
                y_ref):
    a = _dot(om_ref[...], wom_ref[...])
    r = _dot(or_ref[...], wor_ref[...])
    mix = gs_ref[:, 0:D_MODEL] * a + gs_ref[:, D_MODEL:2 * D_MODEL] * r
    x1 = x_ref[...] + _dot(mix.astype(BF16), wout_ref[...])
    h2 = _rms(x1, gffn_ref[...]).astype(BF16)
    u = jnp.maximum(_dot(h2, wup_ref[...]), 0.0)
    x2 = x1 + _dot((u * u).astype(BF16), wdn_ref[...])
    y_ref[...] = _rms(x2, gfin_ref[...])


def _mlp(x, om, orw, gsig, wts, tm):
    rows = x.shape[0]
    row = lambda i: (i, 0)
    const = lambda i: (0, 0)
    full = lambda a: pl.BlockSpec(a.shape, const, pipeline_mode=pl.Buffered(1))
    names = ['w_o_mla', 'w_o_rwkv', 'w_out', 'g_ffn', 'w_up', 'w_down', 'g_final']
    return pl.pallas_call(
        _mlp_kernel,
        grid=(rows // tm,),
        in_specs=[pl.BlockSpec((tm, D_MODEL), row), pl.BlockSpec((tm, HEADS * VDIM), row),
                  pl.BlockSpec((tm, RW_W), row), pl.BlockSpec((tm, 2 * D_MODEL), row)]
        + [full(wts[n]) for n in names],
        out_specs=pl.BlockSpec((tm, D_MODEL), row),
        out_shape=jax.ShapeDtypeStruct((rows, D_MODEL), F32),
        compiler_params=_params(("parallel",)),
        name="mlp",
    )(x, om, orw, gsig, *[wts[n] for n in names])


SRC_QIN = 0
SRC_KVIN = Q_LORA
SRC_KR = Q_LORA + KV_LORA
SRC_RW = SRC_KR + ROPE
SRC_GATE = SRC_RW + RW_COLS


def _winprep_kernel(wt_ref, o_ref):
    def put(dst, src, n):
        for c in range(0, n, 256):
            w = min(256, n - c)
            o_ref[:, dst + c:dst + c + w] = wt_ref[src + c:src + c + w, :].T.astype(BF16)

    put(C_GATE, SRC_GATE, 2 * D_MODEL)
    put(C_RW, SRC_RW, RW_COLS)
    put(C_QIN, SRC_QIN, Q_LORA)
    put(C_KVIN, SRC_KVIN, KV_LORA)
    t = wt_ref[SRC_KR:SRC_KR + LANE, :].T
    lane = lax.broadcasted_iota(jnp.int32, t.shape, 1)
    kr0 = jnp.where(lane < ROPE, t, 0.0)
    o_ref[:, C_KR:C_KR + LANE] = (kr0 + pltpu.roll(kr0, NOPE, 1)).astype(BF16)


def _winprep(w_t):
    return pl.pallas_call(
        _winprep_kernel,
        out_shape=jax.ShapeDtypeStruct((D_MODEL, IN_PERM), BF16),
        compiler_params=pltpu.CompilerParams(vmem_limit_bytes=VMEM_LIMIT),
        name="winprep",
    )(w_t)


def _prep_weights(g_final, g_mix, w_in, g_q, w_uq, g_kv, w_uk, w_uv, w_o_mla, mu_shift, w0, w2, a0, a2, g2,
                  k_k, k_a, r_k, ln_w, ln_b, w_o_rwkv, w_out, g_ffn, w_up, w_down):
    row = lambda a: a.reshape(1, -1).astype(F32)
    w_in_p = _winprep(jnp.swapaxes(w_in, 0, 1))

    uq = w_uq.reshape(Q_LORA, HEADS, NOPE + ROPE)
    zq = jnp.zeros((Q_LORA, HEADS, HP - NOPE - ROPE), F32)
    w_uq_p = jnp.concatenate([uq, zq], axis=-1).reshape(Q_LORA, HEADS * HP).astype(BF16)

    uk = w_uk.reshape(KV_LORA, HEADS, NOPE)
    uk_pad = jnp.concatenate([uk, jnp.zeros((KV_LORA, HEADS, HP - NOPE), F32)], axis=-1).reshape(KV_LORA, HEADS * HP)
    w_ukv = uk_pad.astype(BF16)

    head = jnp.arange(RW_W) // HS
    seg = (head[:, None] == head[None, :]).astype(BF16)
    return dict(
        g_mix=row(g_mix), w_in=w_in_p, g_q=row(g_q), w_uq=w_uq_p, g_kv=row(g_kv), w_ukv=w_ukv,
        w_uk=w_uk.astype(BF16), w_uv=w_uv.astype(BF16), w_uvt=w_uv.T.astype(BF16), w_o_mla=w_o_mla.astype(BF16),
        mu_shift=row(mu_shift), w0=row(w0), w2=w2.astype(BF16), a0=row(a0), a2=a2.astype(BF16),
        g2=g2.astype(BF16), k_k=row(k_k), k_a=row(k_a), r_k=row(r_k), ln_w=row(ln_w), ln_b=row(ln_b), seg=seg,
        w_o_rwkv=w_o_rwkv.astype(BF16), w_out=w_out.astype(BF16), g_ffn=row(g_ffn),
        w_up=w_up.astype(BF16), w_down=w_down.astype(BF16), g_final=row(g_final))


def _rope_tables(pos):
    inv = ROPE_BASE ** (-jnp.arange(0, ROPE, 2, dtype=F32) / ROPE)
    ang = pos.astype(F32)[:, None] * inv[None, :]
    cos, sin = jnp.cos(ang), jnp.sin(ang)
    n = pos.shape[0]
    cc = jnp.concatenate([cos, cos], axis=1)
    ss = jnp.concatenate([-sin, sin], axis=1)
    z32 = jnp.zeros((n, ROPE), F32)
    ctq = jnp.concatenate([jnp.ones((n, NOPE), F32), cc, z32], axis=1)
    stq = jnp.concatenate([jnp.zeros((n, NOPE), F32), ss, z32], axis=1)
    ctk = jnp.concatenate([cc, z32, cc, z32], axis=1)
    stk = jnp.concatenate([ss, z32, ss, z32], axis=1)
    return ctq, stq, ctk, stk


def _row_tile(rows, cap):
    tm = min(rows, cap)
    assert rows % tm == 0
    return tm


def kernel(x_prompt, x_sample, cache_kv, page_table, state_wkv, state_shift, meta_tokens, g_final, g_mix, w_in, g_q, w_uq, g_kv, w_uk, w_uv, w_o_mla, mu_shift, w0, w2, a0, a2, g2, k_k, k_a, r_k, ln_w, ln_b, w_o_rwkv, w_out, g_ffn, w_up, w_down):
    depth = g_mix.shape[0]
    assert depth == 1
    bp, seq, _ = x_prompt.shape
    bd, s_s, _ = x_sample.shape
    assert s_s == 1
    n_pages = page_table.shape[1]
    past_len = n_pages * PAGE
    wts = _prep_weights(g_final, g_mix[0], w_in[0], g_q[0], w_uq[0], g_kv[0], w_uk[0], w_uv[0], w_o_mla[0],
                        mu_shift[0], w0[0], w2[0], a0[0], a2[0], g2[0], k_k[0], k_a[0], r_k[0], ln_w[0], ln_b[0],
                        w_o_rwkv[0], w_out[0], g_ffn[0], w_up[0], w_down[0])

    tab_m = _rope_tables(jnp.arange(N_META))
    _, rw_m, _, kvrow_m, kpad_m, vt_m = _proj(meta_tokens.astype(F32), tab_m, wts, N_META, BF16)
    rw_m_pad = jnp.concatenate([rw_m, jnp.zeros((HS - N_META, RW_COLS), F32)], axis=0)
    zero_shift = jnp.zeros((1, 1, RW_COLS), F32)
    zero_state = jnp.zeros((1, HEADS, HS, HS), F32)
    _, s_meta = _rwkv(rw_m_pad, zero_shift, zero_state, wts, 1, N_META)
    shift_meta = rw_m[N_META - 1:N_META].reshape(1, 1, RW_COLS)

    rows_p = bp * seq
    tm = _row_tile(seq, ROW_TILE)
    tq = _row_tile(seq, KV_TILE)
    xp = x_prompt.reshape(rows_p, D_MODEL)
    ctq, stq, ctk, stk = _rope_tables(N_META + jnp.arange(seq))
    tab_p = (ctq * Q_PRESCALE, stq * Q_PRESCALE, ctk, stk)
    gsig, rw, q, kvrow, kpad, vt = _proj(xp, tab_p, wts, tm, BF16)
    o_mla = _attn(q, kpad, vt, kpad_m, vt_m, bp, seq, tq)
    o_rwkv, s_p = _rwkv(rw, shift_meta, s_meta, wts, bp, None)
    y_prompt = _mlp(xp, o_mla, o_rwkv, gsig, wts, tm).reshape(bp, seq, D_MODEL)
    kv_prompt = jnp.concatenate([jnp.broadcast_to(kvrow_m[None], (bp, N_META, KV_W)),
                                 kvrow.reshape(bp, seq, KV_W)], axis=1)[None]
    shift_prompt = rw.reshape(bp, seq, RW_COLS)[:, -1][None]

    xs = x_sample.reshape(bd, D_MODEL)
    tab_s = _rope_tables(jnp.full((bd,), past_len, jnp.int32))
    gsig_s, rw_s, q_s, kvrow_s, _, _ = _proj(xs, tab_s, wts, bd, F32)
    qf = jnp.transpose(_qlat(q_s, wts['w_uk']), (1, 0, 2))
    cache = jnp.swapaxes(cache_kv.reshape(cache_kv.shape[1], PAGE, KV_W), 1, 2)
    o_mla_s = _sattn(page_table, qf, kvrow_s.reshape(bd, 1, KV_W), wts['w_uv'], cache).reshape(bd, HEADS * VDIM)
    s_t = jnp.transpose(state_wkv[0], (1, 2, 3, 0))
    o_rwkv_s, s_t_new = _rwkv_step(rw_s, state_shift[0], s_t, wts)
    s_s_new = jnp.transpose(s_t_new, (3, 0, 1, 2))
    y_sample = _mlp(xs, o_mla_s, o_rwkv_s, gsig_s, wts, bd).reshape(bd, 1, D_MODEL)

    return (y_prompt, y_sample,
            kv_prompt.astype(cache_kv.dtype), s_p[None].astype(state_wkv.dtype),
            shift_prompt.astype(state_shift.dtype),
            kvrow_s.reshape(1, bd, 1, KV_W).astype(cache_kv.dtype), s_s_new[None].astype(state_wkv.dtype),
            rw_s[None].astype(state_shift.dtype))
```

```python
import functools

import jax
import jax.numpy as jnp
from jax import lax
from jax.experimental import pallas as pl
from jax.experimental.pallas import tpu as pltpu

F32 = jnp.float32
BF16 = jnp.bfloat16

D_MODEL = 1024
N_META = 16
PAGE = 128
HEADS = 8
NOPE = 64
ROPE = 32
VDIM = 64
Q_LORA = 384
KV_LORA = 256
KV_W = KV_LORA + ROPE
ROPE_BASE = 10000.0
SM_SCALE = (NOPE + ROPE) ** -0.5
HS = 64
RW_W = HEADS * HS
W_LORA = 64
A_LORA = 64
G_LORA = 128
RW_COLS = 3 * RW_W + W_LORA + A_LORA + G_LORA
GN_EPS = 64e-5
D_FF = 4 * D_MODEL
NORM_EPS = 1e-6
NEG_INF = -1e30
DECAY_SCALE = 0.6065306597126334
Q_PRESCALE = SM_SCALE * 1.4426950408889634

LANE = 128
HP = 128
VMEM_LIMIT = 56 * 1024 * 1024
KV_TILE = 256
ROW_TILE = 512

C_GATE = 0
C_RW = 2 * D_MODEL
C_QIN = C_RW + RW_COLS
C_KVIN = C_QIN + Q_LORA
C_KR = C_KVIN + KV_LORA
IN_PERM = C_KR + LANE


def _dot(a, b):
    return jnp.dot(a, b, preferred_element_type=F32)


def _dot_nt(a, b):
    return lax.dot_general(a, b, (((1,), (1,)), ((), ())), preferred_element_type=F32)


def _dot_tn(a, b):
    return lax.dot_general(a, b, (((0,), (0,)), ((), ())), preferred_element_type=F32)


def _split2(x):
    hi = x.astype(BF16)
    lo = (x - hi.astype(F32)).astype(BF16)
    return hi, lo


def _split3(x):
    x1 = x.astype(BF16)
    r1 = x - x1.astype(F32)
    x2 = r1.astype(BF16)
    x3 = (r1 - x2.astype(F32)).astype(BF16)
    return x1, x2, x3


def _rms(x, g):
    return x * lax.rsqrt(jnp.mean(x * x, axis=-1, keepdims=True) + NORM_EPS) * g


def _params(sem):
    return pltpu.CompilerParams(dimension_semantics=sem, vmem_limit_bytes=VMEM_LIMIT)


def _proj_kernel(x_ref, gmix_ref, win_ref, gq_ref, wuq_ref, gkv_ref, wukv_ref, wuvt_ref,
                 ctq_ref, stq_ref, ctk_ref, stk_ref,
                 gsig_ref, rw_ref, q_ref, kvrow_ref, kpad_ref, vt_ref, *, q_transposed):
    h = _rms(x_ref[...], gmix_ref[...]).astype(BF16)
    tw = vt_ref.shape[2]
    lane = lax.broadcasted_iota(jnp.int32, (x_ref.shape[0], LANE), 1)
    first_half = (lane % ROPE) < ROPE // 2

    def rope(t, cos_t, sin_t):
        swapped = jnp.where(first_half, pltpu.roll(t, LANE - ROPE // 2, 1), pltpu.roll(t, ROPE // 2, 1))
        return t * cos_t + swapped * sin_t

    zq = _dot(h, win_ref[:, C_QIN:C_QIN + Q_LORA])
    zkv = _dot(h, win_ref[:, C_KVIN:C_KVIN + KV_LORA])
    zkr = _dot(h, win_ref[:, C_KR:C_KR + LANE])
    step = 512
    for c in range(0, 2 * D_MODEL, step):
        gsig_ref[:, c:c + step] = jax.nn.sigmoid(_dot(h, win_ref[:, C_GATE + c:C_GATE + c + step])).astype(BF16)

    qn = _rms(zq, gq_ref[...]).astype(BF16)
    ckv = _rms(zkv, gkv_ref[...])
    cb = ckv.astype(BF16)
    qa = _dot(qn, wuq_ref[...])
    kn = _dot(cb, wukv_ref[...])
    for sub in range(vt_ref.shape[0]):
        vt_ref[sub] = _dot_nt(wuvt_ref[...], cb[sub * tw:(sub + 1) * tw]).astype(BF16)

    for c in range(0, RW_COLS, 256):
        rw_ref[:, c:c + 256] = _dot(h, win_ref[:, C_RW + c:C_RW + c + 256])

    ctq = ctq_ref[...]
    stq = stq_ref[...]
    for hh in range(HEADS):
        hs = slice(hh * HP, (hh + 1) * HP)
        qh = rope(qa[:, hs], ctq, stq)
        if q_transposed:
            for sub in range(q_ref.shape[0]):
                q_ref[sub, hs, :] = qh[sub * tw:(sub + 1) * tw].T.astype(q_ref.dtype)
        else:
            q_ref[:, hs] = qh.astype(q_ref.dtype)
    kvrow_ref[:, 0:KV_LORA] = ckv
    kr = rope(zkr, ctk_ref[...], stk_ref[...])
    kvrow_ref[:, KV_LORA:KV_W] = kr[:, 0:ROPE]
    kr_head = jnp.where((lane >= NOPE) & (lane < NOPE + ROPE), kr, 0.0)
    for hh in range(HEADS):
        hs = slice(hh * HP, (hh + 1) * HP)
        kpad_ref[:, hs] = (kn[:, hs] + kr_head).astype(BF16)


def _proj(x, tables, wts, tm, q_dtype, q_transposed):
    rows = x.shape[0]
    nt = tables[0].shape[0] // tm
    tw = min(tm, KV_TILE)
    row = lambda i: (i, 0)
    const = lambda i: (0, 0)
    tab = lambda i: (i % nt, 0)
    full = lambda a: pl.BlockSpec(a.shape, const, pipeline_mode=pl.Buffered(1))
    in_specs = [pl.BlockSpec((tm, D_MODEL), row), full(wts['g_mix']), full(wts['w_in']), full(wts['g_q']),
                full(wts['w_uq']), full(wts['g_kv']), full(wts['w_ukv']), full(wts['w_uvt'])]
    in_specs += [pl.BlockSpec((tm, LANE), tab)] * 4
    widths = [(2 * D_MODEL, BF16), (RW_COLS, F32), (HEADS * HP, q_dtype), (KV_W, F32), (HEADS * HP, BF16)]
    nt_rows = rows // tm
    tiled = lambda w: pl.BlockSpec((tm // tw, w, tw), lambda i: (i, 0, 0))
    out_specs = [pl.BlockSpec((tm, w), row) for w, _ in widths] + [tiled(HEADS * VDIM)]
    out_shape = [jax.ShapeDtypeStruct((rows, w), dt) for w, dt in widths]
    out_shape += [jax.ShapeDtypeStruct((rows // tw, HEADS * VDIM, tw), BF16)]
    if q_transposed:
        out_specs[2] = tiled(HEADS * HP)
        out_shape[2] = jax.ShapeDtypeStruct((rows // tw, HEADS * HP, tw), q_dtype)
    return pl.pallas_call(
        functools.partial(_proj_kernel, q_transposed=q_transposed),
        grid=(nt_rows,),
        in_specs=in_specs,
        out_specs=out_specs,
        out_shape=out_shape,
        compiler_params=_params(("parallel",)),
        name="proj",
    )(x, wts['g_mix'], wts['w_in'], wts['g_q'], wts['w_uq'], wts['g_kv'], wts['w_ukv'], wts['w_uvt'], *tables)


def _attn_kernel(q_ref, k_ref, vt_ref, km_ref, vmt_ref, eye_ref, o_ref, *, tq):
    i = pl.program_id(1)
    qs = [q_ref[0, hh * HP:(hh + 1) * HP, :] for hh in range(HEADS)]

    def update(carry, groups):
        ss = [[_dot(k_tiles[hh], qs[hh]) for hh in range(HEADS)] for k_tiles, _, _ in groups]
        ps, stats = [], []
        for hh in range(HEADS):
            m, l, _ = carry[hh]
            sm = [s_g[hh] if g[2] is None else jnp.where(g[2], s_g[hh], NEG_INF) for s_g, g in zip(ss, groups)]
            m_new = m
            for s in sm:
                m_new = jnp.maximum(m_new, jnp.max(s, axis=0, keepdims=True))
            alpha = jnp.exp2(m - m_new)
            p = [jnp.exp2(s - m_new) for s in sm]
            l = l * alpha
            for pg in p:
                l = l + jnp.sum(pg, axis=0, keepdims=True)
            stats.append((m_new, l, alpha))
            ps.append([pg.astype(BF16) for pg in p])
        pv = [[_dot(g[1][hh], ps[hh][gi]) for hh in range(HEADS)] for gi, g in enumerate(groups)]
        out = []
        for hh in range(HEADS):
            acc = carry[hh][2] * stats[hh][2]
            for gi in range(len(groups)):
                acc = acc + pv[gi][hh]
            out.append((stats[hh][0], stats[hh][1], acc))
        return tuple(out)

    def kv_tile(j, mask):
        rows = pl.ds(pl.multiple_of(j * tq, tq), tq)
        return ([k_ref[rows, hh * HP:(hh + 1) * HP] for hh in range(HEADS)],
                [vt_ref[j, hh * VDIM:(hh + 1) * VDIM, :] for hh in range(HEADS)], mask)

    init = (jnp.full((1, tq), NEG_INF, F32), jnp.zeros((1, tq), F32), jnp.zeros((VDIM, tq), F32))
    carry = lax.fori_loop(0, i, lambda j, c: update(c, [kv_tile(j, None)]), (init,) * HEADS)
    key = lax.broadcasted_iota(jnp.int32, (tq, tq), 0)
    qry = lax.broadcasted_iota(jnp.int32, (tq, tq), 1)
    meta = ([km_ref[:, hh * HP:(hh + 1) * HP] for hh in range(HEADS)],
            [vmt_ref[0, hh * VDIM:(hh + 1) * VDIM, :] for hh in range(HEADS)], None)
    carry = update(carry, [meta, kv_tile(i, key <= qry)])

    outs = []
    for hh in range(HEADS):
        m, l, acc = carry[hh]
        outs.append(_dot_tn((acc / l).astype(BF16), eye_ref[...]))
    o_ref[...] = jnp.concatenate(outs, axis=-1).astype(o_ref.dtype)


def _attn(q, kpad, vt, km, vmt, batch, seq, tq):
    nq = seq // tq
    eye = jnp.eye(VDIM, dtype=BF16)
    kern = functools.partial(_attn_kernel, tq=tq)
    return pl.pallas_call(
        kern,
        grid=(batch, nq),
        in_specs=[
            pl.BlockSpec((1, HEADS * HP, tq), lambda b, i: (b * nq + i, 0, 0)),
            pl.BlockSpec((seq, HEADS * HP), lambda b, i: (b, 0)),
            pl.BlockSpec((nq, HEADS * VDIM, tq), lambda b, i: (b, 0, 0)),
            pl.BlockSpec(km.shape, lambda b, i: (0, 0)),
            pl.BlockSpec(vmt.shape, lambda b, i: (0, 0, 0)),
            pl.BlockSpec(eye.shape, lambda b, i: (0, 0)),
        ],
        out_specs=pl.BlockSpec((tq, HEADS * VDIM), lambda b, i: (b * nq + i, 0)),
        out_shape=jax.ShapeDtypeStruct((batch * seq, HEADS * VDIM), BF16),
        compiler_params=_params(("parallel", "arbitrary")),
        name="attn",
    )(q, kpad, vt, km, vmt, eye)


def _qlat_kernel(q_ref, wuk_ref, o_ref):
    for hh in range(HEADS):
        qn = q_ref[:, hh * HP:hh * HP + NOPE].astype(BF16)
        o_ref[hh, :, 0:KV_LORA] = _dot_nt(qn, wuk_ref[:, hh * NOPE:(hh + 1) * NOPE]) * SM_SCALE
        o_ref[hh, :, KV_LORA:KV_W] = q_ref[:, hh * HP + NOPE:hh * HP + NOPE + ROPE] * SM_SCALE


def _qlat(q, wuk):
    rows = q.shape[0]
    return pl.pallas_call(
        _qlat_kernel,
        out_shape=jax.ShapeDtypeStruct((HEADS, rows, KV_W), F32),
        name="qlat",
    )(q, wuk)


SATTN_SLOTS = 3
SATTN_CHUNK = 1024


def _sattn_kernel(pt_ref, qf_ref, kvn_ref, wuv_ref, cache_ref, o_ref, buf, sem, *, n_pages, chunk):
    b = pl.program_id(0)
    nb = pl.num_programs(0)
    ahead = SATTN_SLOTS - 1

    def page_copy(page, slot, j):
        return pltpu.make_async_copy(cache_ref.at[page], buf.at[slot, :, pl.ds(j * PAGE, PAGE)], sem.at[slot])

    def start_pages(bb, slot, lo, hi):
        for j in range(lo, hi):
            page_copy(pt_ref[bb, j], slot, j).start(priority=j % 2)

    def wait_slot(slot):
        for j in range(n_pages):
            page_copy(0, slot, j).wait()

    @pl.when(b == 0)
    def _():
        for first in range(ahead):
            start_pages(jnp.minimum(first, nb - 1), first, 0, n_pages)

    slot = b % SATTN_SLOTS
    wait_slot(slot)
    nxt = jnp.minimum(b + ahead, nb - 1)
    nslot = (b + ahead) % SATTN_SLOTS
    n_chunks = (n_pages * PAGE) // chunk
    per_chunk = n_pages // n_chunks

    qf = qf_ref[0].astype(BF16)
    kvn = kvn_ref[0].astype(BF16).astype(F32)
    s_own = jnp.sum(qf.astype(F32) * kvn, axis=-1, keepdims=True)
    kts, ss = [], []
    for ci in range(n_chunks):
        kt = buf[slot, :, ci * chunk:(ci + 1) * chunk].astype(BF16)
        kts.append(kt)
        ss.append(_dot(qf, kt))
        start_pages(nxt, nslot, ci * per_chunk, (ci + 1) * per_chunk)
    m = s_own
    for sc in ss:
        m = jnp.maximum(m, jnp.max(sc, axis=-1, keepdims=True))
    ps = [jnp.exp(sc - m) for sc in ss]
    p_own = jnp.exp(s_own - m)
    l = p_own
    for pc in ps:
        l = l + jnp.sum(pc, axis=-1, keepdims=True)
    acc = p_own.astype(BF16).astype(F32) * kvn[:, 0:KV_LORA]
    for pc, kt in zip(ps, kts):
        acc = acc + _dot_nt(pc.astype(BF16), kt[0:KV_LORA, :])

    @pl.when(b == nb - 1)
    def _():
        for extra in range(1, SATTN_SLOTS):
            wait_slot((b + extra) % SATTN_SLOTS)

    o_lat = (acc / l).astype(BF16)
    full = _dot(o_lat, wuv_ref[...])
    r = lax.broadcasted_iota(jnp.int32, full.shape, 0)
    cidx = lax.broadcasted_iota(jnp.int32, full.shape, 1)
    o_ref[0] = jnp.sum(jnp.where(cidx // VDIM == r, full, 0.0), axis=0, keepdims=True).astype(o_ref.dtype)


def _sattn(page_table, qf, kvn, wuv, cache):
    nb, n_pages = page_table.shape
    assert n_pages % 2 == 0
    chunk = min(SATTN_CHUNK, n_pages * PAGE)
    assert (n_pages * PAGE) % chunk == 0
    kern = functools.partial(_sattn_kernel, n_pages=n_pages, chunk=chunk)
    grid_spec = pltpu.PrefetchScalarGridSpec(
        num_scalar_prefetch=1,
        grid=(nb,),
        in_specs=[
            pl.BlockSpec((1, HEADS, KV_W), lambda b, pt: (b, 0, 0)),
            pl.BlockSpec((1, 1, KV_W), lambda b, pt: (b, 0, 0)),
            pl.BlockSpec(wuv.shape, lambda b, pt: (0, 0)),
            pl.BlockSpec(memory_space=pl.ANY),
        ],
        out_specs=pl.BlockSpec((1, 1, HEADS * VDIM), lambda b, pt: (b, 0, 0)),
        scratch_shapes=[pltpu.VMEM((SATTN_SLOTS, KV_W, n_pages * PAGE), F32),
                        pltpu.SemaphoreType.DMA((SATTN_SLOTS,))],
    )
    return pl.pallas_call(
        kern,
        grid_spec=grid_spec,
        out_shape=jax.ShapeDtypeStruct((nb, 1, HEADS * VDIM), BF16),
        compiler_params=_params(("arbitrary",)),
        name="sattn",
    )(page_table, qf, kvn, wuv, cache)


GROUP = 4
RWKV_BATCHES_PER_STEP = 8
GW = GROUP * HS
RW_PARAM_NAMES = ['mu_shift', 'w0', 'w2', 'a0', 'a2', 'g2', 'k_k', 'k_a', 'r_k', 'ln_w', 'ln_b', 'seg']


def _seg_sum(x, seg):
    return _dot(x.astype(BF16), seg)


def _rwkv_token_prep(cur, prev, p):
    z = cur + (prev - cur) * p['mu_shift']
    r = z[:, 0:RW_W]
    k = z[:, RW_W:2 * RW_W]
    v = z[:, 2 * RW_W:3 * RW_W]
    w_in = z[:, 3 * RW_W:3 * RW_W + W_LORA]
    a_in = z[:, 3 * RW_W + W_LORA:3 * RW_W + W_LORA + A_LORA]
    g_in = z[:, 3 * RW_W + W_LORA + A_LORA:RW_COLS]
    w_log = p['w0'] + _dot(jnp.tanh(w_in).astype(BF16), p['w2'])
    logw = -DECAY_SCALE * jax.nn.sigmoid(w_log)
    a_sig = jax.nn.sigmoid(p['a0'] + _dot(a_in.astype(BF16), p['a2']))
    g = _dot(jax.nn.sigmoid(g_in).astype(BF16), p['g2'])
    kk = k * p['k_k']
    kk = kk * lax.rsqrt(jnp.maximum(_seg_sum(kk * kk, p['seg']), 1e-24))
    k_mod = k * (1.0 + (a_sig - 1.0) * p['k_a'])
    bonus = _seg_sum(r * k_mod * p['r_k'], p['seg']) * v
    return r, k_mod, v, logw, a_sig, g, kk, bonus


def _group_norm_out(y, bonus, g, p):
    mean = _seg_sum(y, p['seg']) * (1.0 / HS)
    d = y - mean
    var = _seg_sum(d * d, p['seg']) * (1.0 / HS)
    yn = d * lax.rsqrt(var + GN_EPS) * p['ln_w'] + p['ln_b']
    return (yn + bonus) * g


def _rwkv_kernel(rw_ref, shift_ref, m0_ref, *rest, n_valid, nbat):
    prm_refs = rest[:len(RW_PARAM_NAMES)]
    tri_ref, ones_ref, o_ref, mout_ref, prev_scr, m_scr = rest[len(RW_PARAM_NAMES):]
    p = {n: ref[...] for n, ref in zip(RW_PARAM_NAMES, prm_refs)}
    chunk = HS
    ngrp = HEADS // GROUP
    c = pl.program_id(1)
    nc = pl.num_programs(1)

    @pl.when(c == 0)
    def _():
        for bi in range(nbat):
            prev_scr[bi] = shift_ref[0]
            m_scr[bi] = m0_ref[0]

    curs, prevs = [], []
    for bi in range(nbat):
        cur_b = rw_ref[bi]
        rowi = lax.broadcasted_iota(jnp.int32, cur_b.shape, 0)
        prevs.append(jnp.where(rowi == 0, prev_scr[bi], pltpu.roll(cur_b, 1, 0)))
        prev_scr[bi] = cur_b[chunk - 1:chunk, :]
        curs.append(cur_b)
    r, k_mod, v, logw, a_sig, g, kk, bonus = _rwkv_token_prep(
        jnp.concatenate(curs, axis=0), jnp.concatenate(prevs, axis=0), p)

    if n_valid is not None:
        step = lax.broadcasted_iota(jnp.int32, (nbat * chunk, RW_W), 0) % chunk
        live = (c * chunk + step) < n_valid
        logw = jnp.where(live, logw, 0.0)
        kk = jnp.where(live, kk, 0.0)
        k_mod = jnp.where(live, k_mod, 0.0)
        v = jnp.where(live, v, 0.0)

    lw3 = _split3(logw)
    seq_rows = [slice(bi * chunk, (bi + 1) * chunk) for bi in range(nbat)]
    gcum = jnp.concatenate([sum(_dot(tri_ref[...], t[rs]) for t in lw3) for rs in seq_rows], axis=0)
    g_end = jnp.concatenate([jnp.broadcast_to(gcum[rs][chunk - 1:chunk, :], (chunk, RW_W)) for rs in seq_rows], axis=0)
    r_t = r * jnp.exp(gcum)
    a_t = -kk * jnp.exp(gcum - logw)
    eng = jnp.exp(-gcum)
    b_t = kk * a_sig * eng
    k_t = k_mod * eng
    tail = jnp.exp(g_end - gcum)
    b_e = kk * a_sig * tail
    k_e = k_mod * tail
    decay_end = jnp.exp(g_end)

    ti = lax.broadcasted_iota(jnp.int32, (chunk, GW), 0)
    lane = lax.broadcasted_iota(jnp.int32, (chunk, GW), 1)
    si = lane % HS
    lane_head = lane // HS
    strict = si < ti
    incl = si <= ti
    eye = (si == ti).astype(F32)

    def bdr(zf):
        zb = zf.astype(BF16)
        return jnp.concatenate([jnp.where(lane_head == hh, zb, jnp.zeros_like(zb)) for hh in range(GROUP)], axis=0)

    streams = [(bi, gi) for bi in range(nbat) for gi in range(ngrp)]
    blk = lambda t, s: t[s[0] * chunk:(s[0] + 1) * chunk, s[1] * GW:(s[1] + 1) * GW]
    cat0 = lambda xs: jnp.concatenate(xs, axis=0).astype(BF16)
    cat1 = lambda xs: jnp.concatenate(xs, axis=1).astype(BF16)

    ar = [cat0([blk(a_t, s), blk(r_t, s)]) for s in streams]
    ab = [_dot_nt(ar[i], bdr(blk(b_t, s))) for i, s in enumerate(streams)]
    ak = [_dot_nt(ar[i], bdr(blk(k_t, s))) for i, s in enumerate(streams)]
    n = [jnp.where(strict, t[0:chunk], 0.0) for t in ab]
    a_rb = [jnp.where(incl, t[chunk:2 * chunk], 0.0) for t in ab]
    a_ak = [jnp.where(strict, t[0:chunk], 0.0) for t in ak]
    a_rk = [jnp.where(incl, t[chunk:2 * chunk], 0.0) for t in ak]

    x = [eye + t for t in n]
    pw = [_dot(t.astype(BF16), bdr(t)) for t in n]
    for _ in range(4):
        z = [_dot(cat0([xi, pi]), bdr(pi)) for xi, pi in zip(x, pw)]
        x = [xi + zi[0:chunk] for xi, zi in zip(x, z)]
        pw = [zi[chunk:2 * chunk] for zi in z]
    z = [_dot(xi.astype(BF16), bdr(pi)) for xi, pi in zip(x, pw)]
    x = [xi + zi for xi, zi in zip(x, z)]

    m0 = [m_scr[s[0], s[1]] for s in streams]
    bm = [bdr(t) for t in m0]
    bv = [bdr(blk(v, s)) for s in streams]
    rhs = [_dot(cat1([blk(a_t, s), a_ak[i]]), jnp.concatenate([bm[i], bv[i]], axis=0)) for i, s in enumerate(streams)]
    u = [_dot(xi.astype(BF16), bdr(ri)) for xi, ri in zip(x, rhs)]
    y = [_dot(cat1([blk(r_t, s), a_rb[i], a_rk[i]]), jnp.concatenate([bm[i], bdr(u[i]), bv[i]], axis=0))
         for i, s in enumerate(streams)]
    full = [_dot_tn(cat0([blk(b_e, s), blk(k_e, s)]), cat0([u[i], blk(v, s)])) for i, s in enumerate(streams)]
    dg = jnp.concatenate([eye * blk(decay_end, s)[0:1] for s in streams], axis=0)
    gm_all = sum(_dot(t, ones_ref[...]) for t in _split3(dg))
    gm = [gm_all[i * chunk:(i + 1) * chunk] for i in range(len(streams))]
    for i, s in enumerate(streams):
        upd = jnp.where(lane_head == 0, full[i][0:HS], 0.0)
        for hh in range(1, GROUP):
            upd = upd + jnp.where(lane_head == hh, full[i][hh * HS:(hh + 1) * HS], 0.0)
        m_scr[s[0], s[1]] = gm[i] * m0[i] + upd

    y_all = jnp.concatenate([jnp.concatenate([y[bi * ngrp + gi] for gi in range(ngrp)], axis=1)
                             for bi in range(nbat)], axis=0)
    out = _group_norm_out(y_all, bonus, g, p).astype(o_ref.dtype)
    for bi in range(nbat):
        o_ref[bi] = out[seq_rows[bi]]

    @pl.when(c == nc - 1)
    def _():
        mout_ref[...] = m_scr[...]


def _state_to_m(s):
    b = s.shape[0]
    return jnp.transpose(s.reshape(b, HEADS // GROUP, GROUP, HS, HS), (0, 1, 4, 2, 3)).reshape(b, HEADS // GROUP, HS, GW)


def _m_to_state(m):
    b = m.shape[0]
    return jnp.transpose(m.reshape(b, HEADS // GROUP, HS, GROUP, HS), (0, 1, 3, 4, 2)).reshape(b, HEADS, HS, HS)


def _rwkv(rw, shift0, s0, wts, batch, n_valid):
    chunk = HS
    rows = rw.shape[0]
    seq = rows // batch
    nc = seq // chunk
    nbat = RWKV_BATCHES_PER_STEP if batch % RWKV_BATCHES_PER_STEP == 0 else 1
    const2 = lambda b, c: (0, 0)
    consts = [wts[n] for n in RW_PARAM_NAMES]
    tri = (jnp.arange(chunk)[:, None] >= jnp.arange(chunk)[None, :]).astype(BF16)
    hd = jnp.arange(GW) // HS
    ones_bd = (hd[:, None] == hd[None, :]).astype(BF16)
    m0 = _state_to_m(s0)
    ngrp = HEADS // GROUP
    kern = functools.partial(_rwkv_kernel, n_valid=n_valid, nbat=nbat)
    o, m = pl.pallas_call(
        kern,
        grid=(batch // nbat, nc),
        in_specs=[pl.BlockSpec((nbat, chunk, RW_COLS), lambda b, c: (b, c, 0)),
                  pl.BlockSpec((1, 1, RW_COLS), lambda b, c: (0, 0, 0)),
                  pl.BlockSpec((1, ngrp, HS, GW), lambda b, c: (0, 0, 0, 0))]
        + [pl.BlockSpec(a.shape, const2) for a in consts]
        + [pl.BlockSpec(tri.shape, const2), pl.BlockSpec(ones_bd.shape, const2)],
        out_specs=[pl.BlockSpec((nbat, chunk, RW_W), lambda b, c: (b, c, 0)),
                   pl.BlockSpec((nbat, ngrp, HS, GW), lambda b, c: (b, 0, 0, 0))],
        out_shape=[jax.ShapeDtypeStruct((batch, seq, RW_W), BF16),
                   jax.ShapeDtypeStruct((batch, ngrp, HS, GW), F32)],
        scratch_shapes=[pltpu.VMEM((nbat, 1, RW_COLS), F32), pltpu.VMEM((nbat, ngrp, HS, GW), F32)],
        compiler_params=_params(("parallel", "arbitrary")),
        name="rwkv",
    )(rw.reshape(batch, seq, RW_COLS), shift0, m0, *consts, tri, ones_bd)
    return o.reshape(rows, RW_W), _m_to_state(m)


def _rwkv_step_kernel(rw_ref, shift_ref, s_ref, *rest):
    prm_refs = rest[:len(RW_PARAM_NAMES)]
    o_ref, sout_ref, t_scr, y_scr, tok_scr = rest[len(RW_PARAM_NAMES):]
    h = pl.program_id(0)

    @pl.when(h == 0)
    def _():
        p = {n: ref[...] for n, ref in zip(RW_PARAM_NAMES, prm_refs)}
        r, k_mod, v, logw, a_sig, g, kk, bonus = _rwkv_token_prep(rw_ref[...], shift_ref[...], p)
        t_scr[0] = r.T
        t_scr[1] = jnp.exp(logw).T
        t_scr[2] = k_mod.T
        t_scr[3] = v.T
        t_scr[4] = (-kk).T
        t_scr[5] = (kk * a_sig).T
        tok_scr[0] = bonus
        tok_scr[1] = g

    rows = pl.ds(pl.multiple_of(h * HS, HS), HS)
    r_h, w_h, k_h = t_scr[0, rows, :], t_scr[1, rows, :], t_scr[2, rows, :]
    a_h, b_h = t_scr[4, rows, :], t_scr[5, rows, :]

    def body(vv, carry):
        sv = s_ref[0, vv]
        vrow = t_scr[3, pl.ds(h * HS + vv, 1), :]
        sa = jnp.sum(sv * a_h, axis=0, keepdims=True)
        sn = sv * w_h + sa * b_h + vrow * k_h
        sout_ref[0, vv] = sn
        y_scr[pl.ds(h * HS + vv, 1), :] = jnp.sum(sn * r_h, axis=0, keepdims=True)
        return carry
    lax.fori_loop(0, HS, body, 0)

    @pl.when(h == HEADS - 1)
    def _():
        p = {n: ref[...] for n, ref in zip(RW_PARAM_NAMES, prm_refs)}
        o_ref[...] = _group_norm_out(y_scr[...].T, tok_scr[0], tok_scr[1], p).astype(o_ref.dtype)


def _rwkv_step(rw, shift, s_t, wts):
    nb = rw.shape[0]
    consts = [wts[n] for n in RW_PARAM_NAMES]
    const2 = lambda h: (0, 0)
    return pl.pallas_call(
        _rwkv_step_kernel,
        grid=(HEADS,),
        in_specs=[pl.BlockSpec((nb, RW_COLS), const2), pl.BlockSpec((nb, RW_COLS), const2),
                  pl.BlockSpec((1, HS, HS, nb), lambda h: (h, 0, 0, 0))]
        + [pl.BlockSpec(a.shape, const2) for a in consts],
        out_specs=[pl.BlockSpec((nb, RW_W), const2), pl.BlockSpec((1, HS, HS, nb), lambda h: (h, 0, 0, 0))],
        out_shape=[jax.ShapeDtypeStruct((nb, RW_W), BF16), jax.ShapeDtypeStruct(s_t.shape, F32)],
        scratch_shapes=[pltpu.VMEM((6, RW_W, nb), F32), pltpu.VMEM((RW_W, nb), F32), pltpu.VMEM((2, nb, RW_W), F32)],
        compiler_params=_params(("arbitrary",)),
        name="rwkv_step",
    )(rw, shift, s_t, *consts)


def _mlp_kernel(x_ref, om_ref, or_ref, gs_ref, wom_ref, wor_ref, wout_ref, gffn_ref, wup_ref, wdn_ref, gfin_ref,
                y_ref):
    a = _dot(om_ref[...], wom_ref[...])
    r = _dot(or_ref[...], wor_ref[...])
    mix = gs_ref[:, 0:D_MODEL] * a + gs_ref[:, D_MODEL:2 * D_MODEL] * r
    x1 = x_ref[...] + _dot(mix.astype(BF16), wout_ref[...])
    h2 = _rms(x1, gffn_ref[...]).astype(BF16)
    u = jnp.maximum(_dot(h2, wup_ref[...]), 0.0)
    x2 = x1 + _dot((u * u).astype(BF16), wdn_ref[...])
    y_ref[...] = _rms(x2, gfin_ref[...])


def _mlp(x, om, orw, gsig, wts, tm):
    rows = x.shape[0]
    row = lambda i: (i, 0)
    const = lambda i: (0, 0)
    full = lambda a: pl.BlockSpec(a.shape, const, pipeline_mode=pl.Buffered(1))
    names = ['w_o_mla', 'w_o_rwkv', 'w_out', 'g_ffn', 'w_up', 'w_down', 'g_final']
    return pl.pallas_call(
        _mlp_kernel,
        grid=(rows // tm,),
        in_specs=[pl.BlockSpec((tm, D_MODEL), row), pl.BlockSpec((tm, HEADS * VDIM), row),
                  pl.BlockSpec((tm, RW_W), row), pl.BlockSpec((tm, 2 * D_MODEL), row)]
        + [full(wts[n]) for n in names],
        out_specs=pl.BlockSpec((tm, D_MODEL), row),
        out_shape=jax.ShapeDtypeStruct((rows, D_MODEL), F32),
        compiler_params=_params(("parallel",)),
        name="mlp",
    )(x, om, orw, gsig, *[wts[n] for n in names])


SRC_QIN = 0
SRC_KVIN = Q_LORA
SRC_KR = Q_LORA + KV_LORA
SRC_RW = SRC_KR + ROPE
SRC_GATE = SRC_RW + RW_COLS


def _winprep_kernel(wt_ref, o_ref):
    def put(dst, src, n):
        for c in range(0, n, 256):
            w = min(256, n - c)
            o_ref[:, dst + c:dst + c + w] = wt_ref[src + c:src + c + w, :].T.astype(BF16)

    put(C_GATE, SRC_GATE, 2 * D_MODEL)
    put(C_RW, SRC_RW, RW_COLS)
    put(C_QIN, SRC_QIN, Q_LORA)
    put(C_KVIN, SRC_KVIN, KV_LORA)
    t = wt_ref[SRC_KR:SRC_KR + LANE, :].T
    lane = lax.broadcasted_iota(jnp.int32, t.shape, 1)
    kr0 = jnp.where(lane < ROPE, t, 0.0)
    o_ref[:, C_KR:C_KR + LANE] = (kr0 + pltpu.roll(kr0, NOPE, 1)).astype(BF16)


def _winprep(w_t):
    return pl.pallas_call(
        _winprep_kernel,
        out_shape=jax.ShapeDtypeStruct((D_MODEL, IN_PERM), BF16),
        compiler_params=pltpu.CompilerParams(vmem_limit_bytes=VMEM_LIMIT),
        name="winprep",
    )(w_t)


def _prep_weights(g_final, g_mix, w_in, g_q, w_uq, g_kv, w_uk, w_uv, w_o_mla, mu_shift, w0, w2, a0, a2, g2,
                  k_k, k_a, r_k, ln_w, ln_b, w_o_rwkv, w_out, g_ffn, w_up, w_down):
    row = lambda a: a.reshape(1, -1).astype(F32)
    w_in_p = _winprep(jnp.swapaxes(w_in, 0, 1))

    uq = w_uq.reshape(Q_LORA, HEADS, NOPE + ROPE)
    zq = jnp.zeros((Q_LORA, HEADS, HP - NOPE - ROPE), F32)
    w_uq_p = jnp.concatenate([uq, zq], axis=-1).reshape(Q_LORA, HEADS * HP).astype(BF16)

    uk = w_uk.reshape(KV_LORA, HEADS, NOPE)
    uk_pad = jnp.concatenate([uk, jnp.zeros((KV_LORA, HEADS, HP - NOPE), F32)], axis=-1).reshape(KV_LORA, HEADS * HP)
    w_ukv = uk_pad.astype(BF16)

    head = jnp.arange(RW_W) // HS
    seg = (head[:, None] == head[None, :]).astype(BF16)
    return dict(
        g_mix=row(g_mix), w_in=w_in_p, g_q=row(g_q), w_uq=w_uq_p, g_kv=row(g_kv), w_ukv=w_ukv,
        w_uk=w_uk.astype(BF16), w_uv=w_uv.astype(BF16), w_uvt=w_uv.T.astype(BF16), w_o_mla=w_o_mla.astype(BF16),
        mu_shift=row(mu_shift), w0=row(w0), w2=w2.astype(BF16), a0=row(a0), a2=a2.astype(BF16),
        g2=g2.astype(BF16), k_k=row(k_k), k_a=row(k_a), r_k=row(r_k), ln_w=row(ln_w), ln_b=row(ln_b), seg=seg,
        w_o_rwkv=w_o_rwkv.astype(BF16), w_out=w_out.astype(BF16), g_ffn=row(g_ffn),
        w_up=w_up.astype(BF16), w_down=w_down.astype(BF16), g_final=row(g_final))


def _rope_tables(pos):
    inv = ROPE_BASE ** (-jnp.arange(0, ROPE, 2, dtype=F32) / ROPE)
    ang = pos.astype(F32)[:, None] * inv[None, :]
    cos, sin = jnp.cos(ang), jnp.sin(ang)
    n = pos.shape[0]
    cc = jnp.concatenate([cos, cos], axis=1)
    ss = jnp.concatenate([-sin, sin], axis=1)
    z32 = jnp.zeros((n, ROPE), F32)
    ctq = jnp.concatenate([jnp.ones((n, NOPE), F32), cc, z32], axis=1)
    stq = jnp.concatenate([jnp.zeros((n, NOPE), F32), ss, z32], axis=1)
    ctk = jnp.concatenate([cc, z32, cc, z32], axis=1)
    stk = jnp.concatenate([ss, z32, ss, z32], axis=1)
    return ctq, stq, ctk, stk


def _row_tile(rows, cap):
    tm = min(rows, cap)
    assert rows % tm == 0
    return tm


def kernel(x_prompt, x_sample, cache_kv, page_table, state_wkv, state_shift, meta_tokens, g_final, g_mix, w_in, g_q, w_uq, g_kv, w_uk, w_uv, w_o_mla, mu_shift, w0, w2, a0, a2, g2, k_k, k_a, r_k, ln_w, ln_b, w_o_rwkv, w_out, g_ffn, w_up, w_down):
    depth = g_mix.shape[0]
    assert depth == 1
    bp, seq, _ = x_prompt.shape
    bd, s_s, _ = x_sample.shape
    assert s_s == 1
    n_pages = page_table.shape[1]
    past_len = n_pages * PAGE
    wts = _prep_weights(g_final, g_mix[0], w_in[0], g_q[0], w_uq[0], g_kv[0], w_uk[0], w_uv[0], w_o_mla[0],
                        mu_shift[0], w0[0], w2[0], a0[0], a2[0], g2[0], k_k[0], k_a[0], r_k[0], ln_w[0], ln_b[0],
                        w_o_rwkv[0], w_out[0], g_ffn[0], w_up[0], w_down[0])

    tab_m = _rope_tables(jnp.arange(N_META))
    _, rw_m, _, kvrow_m, kpad_m, vt_m = _proj(meta_tokens.astype(F32), tab_m, wts, N_META, BF16, False)
    rw_m_pad = jnp.concatenate([rw_m, jnp.zeros((HS - N_META, RW_COLS), F32)], axis=0)
    zero_shift = jnp.zeros((1, 1, RW_COLS), F32)
    zero_state = jnp.zeros((1, HEADS, HS, HS), F32)
    _, s_meta = _rwkv(rw_m_pad, zero_shift, zero_state, wts, 1, N_META)
    shift_meta = rw_m[N_META - 1:N_META].reshape(1, 1, RW_COLS)

    rows_p = bp * seq
    tm = _row_tile(seq, ROW_TILE)
    tq = _row_tile(seq, KV_TILE)
    xp = x_prompt.reshape(rows_p, D_MODEL)
    ctq, stq, ctk, stk = _rope_tables(N_META + jnp.arange(seq))
    tab_p = (ctq * Q_PRESCALE, stq * Q_PRESCALE, ctk, stk)
    gsig, rw, q, kvrow, kpad, vt = _proj(xp, tab_p, wts, tm, BF16, True)
    o_mla = _attn(q, kpad, vt, kpad_m, vt_m, bp, seq, tq)
    o_rwkv, s_p = _rwkv(rw, shift_meta, s_meta, wts, bp, None)
    y_prompt = _mlp(xp, o_mla, o_rwkv, gsig, wts, tm).reshape(bp, seq, D_MODEL)
    kv_prompt = jnp.concatenate([jnp.broadcast_to(kvrow_m[None], (bp, N_META, KV_W)),
                                 kvrow.reshape(bp, seq, KV_W)], axis=1)[None]
    shift_prompt = rw.reshape(bp, seq, RW_COLS)[:, -1][None]

    xs = x_sample.reshape(bd, D_MODEL)
    tab_s = _rope_tables(jnp.full((bd,), past_len, jnp.int32))
    gsig_s, rw_s, q_s, kvrow_s, _, _ = _proj(xs, tab_s, wts, bd, F32, False)
    qf = jnp.transpose(_qlat(q_s, wts['w_uk']), (1, 0, 2))
    cache = jnp.swapaxes(cache_kv.reshape(cache_kv.shape[1], PAGE, KV_W), 1, 2)
    o_mla_s = _sattn(page_table, qf, kvrow_s.reshape(bd, 1, KV_W), wts['w_uv'], cache).reshape(bd, HEADS * VDIM)
    s_t = jnp.transpose(state_wkv[0], (1, 2, 3, 0))
    o_rwkv_s, s_t_new = _rwkv_step(rw_s, state_shift[0], s_t, wts)
    s_s_new = jnp.transpose(s_t_new, (3, 0, 1, 2))
    y_sample = _mlp(xs, o_mla_s, o_rwkv_s, gsig_s, wts, bd).reshape(bd, 1, D_MODEL)

    return (y_prompt, y_sample,
            kv_prompt.astype(cache_kv.dtype), s_p[None].astype(state_wkv.dtype),
            shift_prompt.astype(state_shift.dtype),
            kvrow_s.reshape(1, bd, 1, KV_W).astype(cache_kv.dtype), s_s_new[None].astype(state_wkv.dtype),
            rw_s[None].astype(state_shift.dtype))
```

```python
import functools

import numpy as np
import jax
import jax.numpy as jnp
from jax import lax
from jax.experimental import pallas as pl
from jax.experimental.pallas import tpu as pltpu

F32 = jnp.float32
BF16 = jnp.bfloat16

D_MODEL = 1024
N_META = 16
PAGE = 128
HEADS = 8
NOPE = 64
ROPE = 32
VDIM = 64
Q_LORA = 384
KV_LORA = 256
KV_W = KV_LORA + ROPE
ROPE_BASE = 10000.0
SM_SCALE = (NOPE + ROPE) ** -0.5
HS = 64
RW_W = HEADS * HS
W_LORA = 64
A_LORA = 64
G_LORA = 128
RW_COLS = 3 * RW_W + W_LORA + A_LORA + G_LORA
GN_EPS = 64e-5
D_FF = 4 * D_MODEL
NORM_EPS = 1e-6
NEG_INF = -1e30
DECAY_SCALE = 0.6065306597126334
Q_PRESCALE = SM_SCALE * 1.4426950408889634

LANE = 128
HP = 128
VMEM_LIMIT = 56 * 1024 * 1024
KV_TILE = 256
ROW_TILE = 512

C_GATE = 0
C_RW = 2 * D_MODEL
C_QIN = C_RW + RW_COLS
C_KVIN = C_QIN + Q_LORA
C_KR = C_KVIN + KV_LORA
IN_PERM = C_KR + LANE


def _dot(a, b):
    return jnp.dot(a, b, preferred_element_type=F32)


def _dot_nt(a, b):
    return lax.dot_general(a, b, (((1,), (1,)), ((), ())), preferred_element_type=F32)


def _dot_tn(a, b):
    return lax.dot_general(a, b, (((0,), (0,)), ((), ())), preferred_element_type=F32)


def _split2(x):
    hi = x.astype(BF16)
    lo = (x - hi.astype(F32)).astype(BF16)
    return hi, lo


def _split3(x):
    x1 = x.astype(BF16)
    r1 = x - x1.astype(F32)
    x2 = r1.astype(BF16)
    x3 = (r1 - x2.astype(F32)).astype(BF16)
    return x1, x2, x3


def _rms(x, g):
    return x * lax.rsqrt(jnp.mean(x * x, axis=-1, keepdims=True) + NORM_EPS) * g


def _params(sem):
    return pltpu.CompilerParams(dimension_semantics=sem, vmem_limit_bytes=VMEM_LIMIT)


def _proj_kernel(x_ref, gmix_ref, win_ref, gq_ref, wuq_ref, gkv_ref, wukv_ref, wuvt_ref,
                 ctq_ref, stq_ref, ctk_ref, stk_ref,
                 gsig_ref, rw_ref, q_ref, kvrow_ref, kpad_ref, vt_ref, *, q_transposed):
    h = _rms(x_ref[...], gmix_ref[...]).astype(BF16)
    tw = vt_ref.shape[2]
    lane = lax.broadcasted_iota(jnp.int32, (x_ref.shape[0], LANE), 1)
    first_half = (lane % ROPE) < ROPE // 2

    def rope(t, cos_t, sin_t):
        swapped = jnp.where(first_half, pltpu.roll(t, LANE - ROPE // 2, 1), pltpu.roll(t, ROPE // 2, 1))
        return t * cos_t + swapped * sin_t

    zq = _dot(h, win_ref[:, C_QIN:C_QIN + Q_LORA])
    zkv = _dot(h, win_ref[:, C_KVIN:C_KVIN + KV_LORA])
    zkr = _dot(h, win_ref[:, C_KR:C_KR + LANE])
    step = 512
    for c in range(0, 2 * D_MODEL, step):
        gsig_ref[:, c:c + step] = jax.nn.sigmoid(_dot(h, win_ref[:, C_GATE + c:C_GATE + c + step])).astype(BF16)

    qn = _rms(zq, gq_ref[...]).astype(BF16)
    ckv = _rms(zkv, gkv_ref[...])
    cb = ckv.astype(BF16)
    qa = _dot(qn, wuq_ref[...])
    kn = _dot(cb, wukv_ref[...])
    for sub in range(vt_ref.shape[0]):
        vt_ref[sub] = _dot_nt(wuvt_ref[...], cb[sub * tw:(sub + 1) * tw]).astype(BF16)

    for c in range(0, RW_COLS, 256):
        rw_ref[:, c:c + 256] = _dot(h, win_ref[:, C_RW + c:C_RW + c + 256])

    ctq = ctq_ref[...]
    stq = stq_ref[...]
    for hh in range(HEADS):
        hs = slice(hh * HP, (hh + 1) * HP)
        qh = rope(qa[:, hs], ctq, stq)
        if q_transposed:
            for sub in range(q_ref.shape[0]):
                q_ref[sub, hs, :] = qh[sub * tw:(sub + 1) * tw].T.astype(q_ref.dtype)
        else:
            q_ref[:, hs] = qh.astype(q_ref.dtype)
    kvrow_ref[:, 0:KV_LORA] = ckv
    kr = rope(zkr, ctk_ref[...], stk_ref[...])
    kvrow_ref[:, KV_LORA:KV_W] = kr[:, 0:ROPE]
    kr_head = jnp.where((lane >= NOPE) & (lane < NOPE + ROPE), kr, 0.0)
    for hh in range(HEADS):
        hs = slice(hh * HP, (hh + 1) * HP)
        kpad_ref[:, hs] = (kn[:, hs] + kr_head).astype(BF16)


def _proj(x, tables, wts, tm, q_dtype, q_transposed):
    rows = x.shape[0]
    nt = tables[0].shape[0] // tm
    tw = min(tm, KV_TILE)
    row = lambda i: (i, 0)
    const = lambda i: (0, 0)
    tab = lambda i: (i % nt, 0)
    full = lambda a: pl.BlockSpec(a.shape, const, pipeline_mode=pl.Buffered(1))
    in_specs = [pl.BlockSpec((tm, D_MODEL), row), full(wts['g_mix']), full(wts['w_in']), full(wts['g_q']),
                full(wts['w_uq']), full(wts['g_kv']), full(wts['w_ukv']), full(wts['w_uvt'])]
    in_specs += [pl.BlockSpec((tm, LANE), tab)] * 4
    widths = [(2 * D_MODEL, BF16), (RW_COLS, F32), (HEADS * HP, q_dtype), (KV_W, F32), (HEADS * HP, BF16)]
    nt_rows = rows // tm
    tiled = lambda w: pl.BlockSpec((tm // tw, w, tw), lambda i: (i, 0, 0))
    out_specs = [pl.BlockSpec((tm, w), row) for w, _ in widths] + [tiled(HEADS * VDIM)]
    out_shape = [jax.ShapeDtypeStruct((rows, w), dt) for w, dt in widths]
    out_shape += [jax.ShapeDtypeStruct((rows // tw, HEADS * VDIM, tw), BF16)]
    if q_transposed:
        out_specs[2] = tiled(HEADS * HP)
        out_shape[2] = jax.ShapeDtypeStruct((rows // tw, HEADS * HP, tw), q_dtype)
    return pl.pallas_call(
        functools.partial(_proj_kernel, q_transposed=q_transposed),
        grid=(nt_rows,),
        in_specs=in_specs,
        out_specs=out_specs,
        out_shape=out_shape,
        compiler_params=_params(("parallel",)),
        name="proj",
    )(x, wts['g_mix'], wts['w_in'], wts['g_q'], wts['w_uq'], wts['g_kv'], wts['w_ukv'], wts['w_uvt'], *tables)


def _attn_kernel(q_ref, k_ref, vt_ref, km_ref, vmt_ref, eye_ref, o_ref, *, tq):
    i = pl.program_id(1)
    qs = [q_ref[0, hh * HP:(hh + 1) * HP, :] for hh in range(HEADS)]

    def update(carry, groups):
        ss = [[_dot(k_tiles[hh], qs[hh]) for hh in range(HEADS)] for k_tiles, _, _ in groups]
        ps, stats = [], []
        for hh in range(HEADS):
            m, l, _ = carry[hh]
            sm = [s_g[hh] if g[2] is None else jnp.where(g[2], s_g[hh], NEG_INF) for s_g, g in zip(ss, groups)]
            m_new = m
            for s in sm:
                m_new = jnp.maximum(m_new, jnp.max(s, axis=0, keepdims=True))
            alpha = jnp.exp2(m - m_new)
            p = [jnp.exp2(s - m_new) for s in sm]
            l = l * alpha
            for pg in p:
                l = l + jnp.sum(pg, axis=0, keepdims=True)
            stats.append((m_new, l, alpha))
            ps.append([pg.astype(BF16) for pg in p])
        pv = [[_dot(g[1][hh], ps[hh][gi]) for hh in range(HEADS)] for gi, g in enumerate(groups)]
        out = []
        for hh in range(HEADS):
            acc = carry[hh][2] * stats[hh][2]
            for gi in range(len(groups)):
                acc = acc + pv[gi][hh]
            out.append((stats[hh][0], stats[hh][1], acc))
        return tuple(out)

    def kv_tile(j, mask):
        rows = pl.ds(pl.multiple_of(j * tq, tq), tq)
        return ([k_ref[rows, hh * HP:(hh + 1) * HP] for hh in range(HEADS)],
                [vt_ref[j, hh * VDIM:(hh + 1) * VDIM, :] for hh in range(HEADS)], mask)

    init = (jnp.full((1, tq), NEG_INF, F32), jnp.zeros((1, tq), F32), jnp.zeros((VDIM, tq), F32))
    carry = lax.fori_loop(0, i, lambda j, c: update(c, [kv_tile(j, None)]), (init,) * HEADS)
    key = lax.broadcasted_iota(jnp.int32, (tq, tq), 0)
    qry = lax.broadcasted_iota(jnp.int32, (tq, tq), 1)
    meta = ([km_ref[:, hh * HP:(hh + 1) * HP] for hh in range(HEADS)],
            [vmt_ref[0, hh * VDIM:(hh + 1) * VDIM, :] for hh in range(HEADS)], None)
    carry = update(carry, [meta, kv_tile(i, key <= qry)])

    outs = []
    for hh in range(HEADS):
        m, l, acc = carry[hh]
        outs.append(_dot_tn((acc / l).astype(BF16), eye_ref[...]))
    o_ref[...] = jnp.concatenate(outs, axis=-1).astype(o_ref.dtype)


def _attn(q, kpad, vt, km, vmt, batch, seq, tq):
    nq = seq // tq
    eye = jnp.asarray(np.eye(VDIM), BF16)
    kern = functools.partial(_attn_kernel, tq=tq)
    return pl.pallas_call(
        kern,
        grid=(batch, nq),
        in_specs=[
            pl.BlockSpec((1, HEADS * HP, tq), lambda b, i: (b * nq + i, 0, 0)),
            pl.BlockSpec((seq, HEADS * HP), lambda b, i: (b, 0)),
            pl.BlockSpec((nq, HEADS * VDIM, tq), lambda b, i: (b, 0, 0)),
            pl.BlockSpec(km.shape, lambda b, i: (0, 0)),
            pl.BlockSpec(vmt.shape, lambda b, i: (0, 0, 0)),
            pl.BlockSpec(eye.shape, lambda b, i: (0, 0)),
        ],
        out_specs=pl.BlockSpec((tq, HEADS * VDIM), lambda b, i: (b * nq + i, 0)),
        out_shape=jax.ShapeDtypeStruct((batch * seq, HEADS * VDIM), BF16),
        compiler_params=_params(("parallel", "arbitrary")),
        name="attn",
    )(q, kpad, vt, km, vmt, eye)


def _qlat_kernel(q_ref, wuk_ref, o_ref):
    for hh in range(HEADS):
        qn = q_ref[:, hh * HP:hh * HP + NOPE].astype(BF16)
        o_ref[hh, :, 0:KV_LORA] = _dot_nt(qn, wuk_ref[:, hh * NOPE:(hh + 1) * NOPE]) * SM_SCALE
        o_ref[hh, :, KV_LORA:KV_W] = q_ref[:, hh * HP + NOPE:hh * HP + NOPE + ROPE] * SM_SCALE


def _qlat(q, wuk):
    rows = q.shape[0]
    return pl.pallas_call(
        _qlat_kernel,
        out_shape=jax.ShapeDtypeStruct((HEADS, rows, KV_W), F32),
        name="qlat",
    )(q, wuk)


SATTN_SLOTS = 3
SATTN_CHUNK = 1024


def _sattn_kernel(pt_ref, qf_ref, kvn_ref, wuv_ref, cache_ref, o_ref, buf, sem, *, n_pages, chunk):
    b = pl.program_id(0)
    nb = pl.num_programs(0)
    ahead = SATTN_SLOTS - 1

    def page_copy(page, slot, j):
        return pltpu.make_async_copy(cache_ref.at[page], buf.at[slot, :, pl.ds(j * PAGE, PAGE)], sem.at[slot])

    def start_pages(bb, slot, lo, hi):
        for j in range(lo, hi):
            page_copy(pt_ref[bb, j], slot, j).start(priority=j % 2)

    def wait_slot(slot):
        for j in range(n_pages):
            page_copy(0, slot, j).wait()

    @pl.when(b == 0)
    def _():
        for first in range(ahead):
            start_pages(jnp.minimum(first, nb - 1), first, 0, n_pages)

    slot = b % SATTN_SLOTS
    wait_slot(slot)
    nxt = jnp.minimum(b + ahead, nb - 1)
    nslot = (b + ahead) % SATTN_SLOTS
    n_chunks = (n_pages * PAGE) // chunk
    per_chunk = n_pages // n_chunks

    qf = qf_ref[0].astype(BF16)
    kvn = kvn_ref[0].astype(BF16).astype(F32)
    s_own = jnp.sum(qf.astype(F32) * kvn, axis=-1, keepdims=True)
    kts, ss = [], []
    for ci in range(n_chunks):
        kt = buf[slot, :, ci * chunk:(ci + 1) * chunk].astype(BF16)
        kts.append(kt)
        ss.append(_dot(qf, kt))
        start_pages(nxt, nslot, ci * per_chunk, (ci + 1) * per_chunk)
    m = s_own
    for sc in ss:
        m = jnp.maximum(m, jnp.max(sc, axis=-1, keepdims=True))
    ps = [jnp.exp(sc - m) for sc in ss]
    p_own = jnp.exp(s_own - m)
    l = p_own
    for pc in ps:
        l = l + jnp.sum(pc, axis=-1, keepdims=True)
    acc = p_own.astype(BF16).astype(F32) * kvn[:, 0:KV_LORA]
    for pc, kt in zip(ps, kts):
        acc = acc + _dot_nt(pc.astype(BF16), kt[0:KV_LORA, :])

    @pl.when(b == nb - 1)
    def _():
        for extra in range(1, SATTN_SLOTS):
            wait_slot((b + extra) % SATTN_SLOTS)

    o_lat = (acc / l).astype(BF16)
    full = _dot(o_lat, wuv_ref[...])
    r = lax.broadcasted_iota(jnp.int32, full.shape, 0)
    cidx = lax.broadcasted_iota(jnp.int32, full.shape, 1)
    o_ref[0] = jnp.sum(jnp.where(cidx // VDIM == r, full, 0.0), axis=0, keepdims=True).astype(o_ref.dtype)


def _sattn(page_table, qf, kvn, wuv, cache):
    nb, n_pages = page_table.shape
    assert n_pages % 2 == 0
    chunk = min(SATTN_CHUNK, n_pages * PAGE)
    assert (n_pages * PAGE) % chunk == 0
    kern = functools.partial(_sattn_kernel, n_pages=n_pages, chunk=chunk)
    grid_spec = pltpu.PrefetchScalarGridSpec(
        num_scalar_prefetch=1,
        grid=(nb,),
        in_specs=[
            pl.BlockSpec((1, HEADS, KV_W), lambda b, pt: (b, 0, 0)),
            pl.BlockSpec((1, 1, KV_W), lambda b, pt: (b, 0, 0)),
            pl.BlockSpec(wuv.shape, lambda b, pt: (0, 0)),
            pl.BlockSpec(memory_space=pl.ANY),
        ],
        out_specs=pl.BlockSpec((1, 1, HEADS * VDIM), lambda b, pt: (b, 0, 0)),
        scratch_shapes=[pltpu.VMEM((SATTN_SLOTS, KV_W, n_pages * PAGE), F32),
                        pltpu.SemaphoreType.DMA((SATTN_SLOTS,))],
    )
    return pl.pallas_call(
        kern,
        grid_spec=grid_spec,
        out_shape=jax.ShapeDtypeStruct((nb, 1, HEADS * VDIM), BF16),
        compiler_params=_params(("arbitrary",)),
        name="sattn",
    )(page_table, qf, kvn, wuv, cache)


GROUP = 4
RWKV_BATCHES_PER_STEP = 8
GW = GROUP * HS
RW_PARAM_NAMES = ['mu_shift', 'w0', 'w2', 'a0', 'a2', 'g2', 'k_k', 'k_a', 'r_k', 'ln_w', 'ln_b', 'seg']


def _seg_sum(x, seg):
    return _dot(x.astype(BF16), seg)


def _rwkv_token_prep(cur, prev, p):
    z = cur + (prev - cur) * p['mu_shift']
    r = z[:, 0:RW_W]
    k = z[:, RW_W:2 * RW_W]
    v = z[:, 2 * RW_W:3 * RW_W]
    w_in = z[:, 3 * RW_W:3 * RW_W + W_LORA]
    a_in = z[:, 3 * RW_W + W_LORA:3 * RW_W + W_LORA + A_LORA]
    g_in = z[:, 3 * RW_W + W_LORA + A_LORA:RW_COLS]
    w_log = p['w0'] + _dot(jnp.tanh(w_in).astype(BF16), p['w2'])
    logw = -DECAY_SCALE * jax.nn.sigmoid(w_log)
    a_sig = jax.nn.sigmoid(p['a0'] + _dot(a_in.astype(BF16), p['a2']))
    g = _dot(jax.nn.sigmoid(g_in).astype(BF16), p['g2'])
    kk = k * p['k_k']
    kk = kk * lax.rsqrt(jnp.maximum(_seg_sum(kk * kk, p['seg']), 1e-24))
    k_mod = k * (1.0 + (a_sig - 1.0) * p['k_a'])
    bonus = _seg_sum(r * k_mod * p['r_k'], p['seg']) * v
    return r, k_mod, v, logw, a_sig, g, kk, bonus


def _group_norm_out(y, bonus, g, p):
    mean = _seg_sum(y, p['seg']) * (1.0 / HS)
    d = y - mean
    var = _seg_sum(d * d, p['seg']) * (1.0 / HS)
    yn = d * lax.rsqrt(var + GN_EPS) * p['ln_w'] + p['ln_b']
    return (yn + bonus) * g


def _rwkv_kernel(rw_ref, shift_ref, m0_ref, *rest, n_valid, nbat):
    prm_refs = rest[:len(RW_PARAM_NAMES)]
    tri_ref, ones_ref, o_ref, mout_ref, prev_scr, m_scr = rest[len(RW_PARAM_NAMES):]
    p = {n: ref[...] for n, ref in zip(RW_PARAM_NAMES, prm_refs)}
    chunk = HS
    ngrp = HEADS // GROUP
    c = pl.program_id(1)
    nc = pl.num_programs(1)

    @pl.when(c == 0)
    def _():
        for bi in range(nbat):
            prev_scr[bi] = shift_ref[0]
            m_scr[bi] = m0_ref[0]

    curs, prevs = [], []
    for bi in range(nbat):
        cur_b = rw_ref[bi]
        rowi = lax.broadcasted_iota(jnp.int32, cur_b.shape, 0)
        prevs.append(jnp.where(rowi == 0, prev_scr[bi], pltpu.roll(cur_b, 1, 0)))
        prev_scr[bi] = cur_b[chunk - 1:chunk, :]
        curs.append(cur_b)
    r, k_mod, v, logw, a_sig, g, kk, bonus = _rwkv_token_prep(
        jnp.concatenate(curs, axis=0), jnp.concatenate(prevs, axis=0), p)

    if n_valid is not None:
        step = lax.broadcasted_iota(jnp.int32, (nbat * chunk, RW_W), 0) % chunk
        live = (c * chunk + step) < n_valid
        logw = jnp.where(live, logw, 0.0)
        kk = jnp.where(live, kk, 0.0)
        k_mod = jnp.where(live, k_mod, 0.0)
        v = jnp.where(live, v, 0.0)

    lw3 = _split3(logw)
    seq_rows = [slice(bi * chunk, (bi + 1) * chunk) for bi in range(nbat)]
    gcum = jnp.concatenate([sum(_dot(tri_ref[...], t[rs]) for t in lw3) for rs in seq_rows], axis=0)
    g_end = jnp.concatenate([jnp.broadcast_to(gcum[rs][chunk - 1:chunk, :], (chunk, RW_W)) for rs in seq_rows], axis=0)
    r_t = r * jnp.exp(gcum)
    a_t = -kk * jnp.exp(gcum - logw)
    eng = jnp.exp(-gcum)
    b_t = kk * a_sig * eng
    k_t = k_mod * eng
    decay_end = jnp.exp(g_end)
    b_e = b_t * decay_end
    k_e = k_t * decay_end

    ti = lax.broadcasted_iota(jnp.int32, (chunk, GW), 0)
    lane = lax.broadcasted_iota(jnp.int32, (chunk, GW), 1)
    si = lane % HS
    lane_head = lane // HS
    strict = si < ti
    incl = si <= ti
    eye = (si == ti).astype(F32)

    def bdr(zf):
        zb = zf.astype(BF16)
        return jnp.concatenate([jnp.where(lane_head == hh, zb, jnp.zeros_like(zb)) for hh in range(GROUP)], axis=0)

    streams = [(bi, gi) for bi in range(nbat) for gi in range(ngrp)]
    blk = lambda t, s: t[s[0] * chunk:(s[0] + 1) * chunk, s[1] * GW:(s[1] + 1) * GW]
    cat0 = lambda xs: jnp.concatenate(xs, axis=0).astype(BF16)
    cat1 = lambda xs: jnp.concatenate(xs, axis=1).astype(BF16)

    ar = [cat0([blk(a_t, s), blk(r_t, s)]) for s in streams]
    ab = [_dot_nt(ar[i], bdr(blk(b_t, s))) for i, s in enumerate(streams)]
    ak = [_dot_nt(ar[i], bdr(blk(k_t, s))) for i, s in enumerate(streams)]
    n = [jnp.where(strict, t[0:chunk], 0.0) for t in ab]
    a_rb = [jnp.where(incl, t[chunk:2 * chunk], 0.0) for t in ab]
    a_ak = [jnp.where(strict, t[0:chunk], 0.0) for t in ak]
    a_rk = [jnp.where(incl, t[chunk:2 * chunk], 0.0) for t in ak]

    x = [eye + t for t in n]
    pw = [_dot(t.astype(BF16), bdr(t)) for t in n]
    for _ in range(4):
        z = [_dot(cat0([xi, pi]), bdr(pi)) for xi, pi in zip(x, pw)]
        x = [xi + zi[0:chunk] for xi, zi in zip(x, z)]
        pw = [zi[chunk:2 * chunk] for zi in z]
    z = [_dot(xi.astype(BF16), bdr(pi)) for xi, pi in zip(x, pw)]
    x = [xi + zi for xi, zi in zip(x, z)]

    m0 = [m_scr[s[0], s[1]] for s in streams]
    bm = [bdr(t) for t in m0]
    bv = [bdr(blk(v, s)) for s in streams]
    rhs = [_dot(cat1([blk(a_t, s), a_ak[i]]), jnp.concatenate([bm[i], bv[i]], axis=0)) for i, s in enumerate(streams)]
    u = [_dot(xi.astype(BF16), bdr(ri)) for xi, ri in zip(x, rhs)]
    y = [_dot(cat1([blk(r_t, s), a_rb[i], a_rk[i]]), jnp.concatenate([bm[i], bdr(u[i]), bv[i]], axis=0))
         for i, s in enumerate(streams)]
    full = [_dot_tn(cat0([blk(b_e, s), blk(k_e, s)]), cat0([u[i], blk(v, s)])) for i, s in enumerate(streams)]
    dg = jnp.concatenate([eye * blk(decay_end, s)[0:1] for s in streams], axis=0)
    gm_all = sum(_dot(t, ones_ref[...]) for t in _split3(dg))
    gm = [gm_all[i * chunk:(i + 1) * chunk] for i in range(len(streams))]
    for i, s in enumerate(streams):
        upd = jnp.where(lane_head == 0, full[i][0:HS], 0.0)
        for hh in range(1, GROUP):
            upd = upd + jnp.where(lane_head == hh, full[i][hh * HS:(hh + 1) * HS], 0.0)
        m_scr[s[0], s[1]] = gm[i] * m0[i] + upd

    y_all = jnp.concatenate([jnp.concatenate([y[bi * ngrp + gi] for gi in range(ngrp)], axis=1)
                             for bi in range(nbat)], axis=0)
    out = _group_norm_out(y_all, bonus, g, p).astype(o_ref.dtype)
    for bi in range(nbat):
        o_ref[bi] = out[seq_rows[bi]]

    @pl.when(c == nc - 1)
    def _():
        mout_ref[...] = m_scr[...]


def _state_to_m(s):
    b = s.shape[0]
    return jnp.transpose(s.reshape(b, HEADS // GROUP, GROUP, HS, HS), (0, 1, 4, 2, 3)).reshape(b, HEADS // GROUP, HS, GW)


def _m_to_state(m):
    b = m.shape[0]
    return jnp.transpose(m.reshape(b, HEADS // GROUP, HS, GROUP, HS), (0, 1, 3, 4, 2)).reshape(b, HEADS, HS, HS)


def _rwkv(rw, shift0, s0, wts, batch, n_valid):
    chunk = HS
    rows = rw.shape[0]
    seq = rows // batch
    nc = seq // chunk
    nbat = RWKV_BATCHES_PER_STEP if batch % RWKV_BATCHES_PER_STEP == 0 else 1
    const2 = lambda b, c: (0, 0)
    consts = [wts[n] for n in RW_PARAM_NAMES]
    tri = jnp.asarray(np.arange(chunk)[:, None] >= np.arange(chunk)[None, :], BF16)
    hd = np.arange(GW) // HS
    ones_bd = jnp.asarray(hd[:, None] == hd[None, :], BF16)
    m0 = _state_to_m(s0)
    ngrp = HEADS // GROUP
    kern = functools.partial(_rwkv_kernel, n_valid=n_valid, nbat=nbat)
    o, m = pl.pallas_call(
        kern,
        grid=(batch // nbat, nc),
        in_specs=[pl.BlockSpec((nbat, chunk, RW_COLS), lambda b, c: (b, c, 0)),
                  pl.BlockSpec((1, 1, RW_COLS), lambda b, c: (0, 0, 0)),
                  pl.BlockSpec((1, ngrp, HS, GW), lambda b, c: (0, 0, 0, 0))]
        + [pl.BlockSpec(a.shape, const2) for a in consts]
        + [pl.BlockSpec(tri.shape, const2), pl.BlockSpec(ones_bd.shape, const2)],
        out_specs=[pl.BlockSpec((nbat, chunk, RW_W), lambda b, c: (b, c, 0)),
                   pl.BlockSpec((nbat, ngrp, HS, GW), lambda b, c: (b, 0, 0, 0))],
        out_shape=[jax.ShapeDtypeStruct((batch, seq, RW_W), BF16),
                   jax.ShapeDtypeStruct((batch, ngrp, HS, GW), F32)],
        scratch_shapes=[pltpu.VMEM((nbat, 1, RW_COLS), F32), pltpu.VMEM((nbat, ngrp, HS, GW), F32)],
        compiler_params=_params(("parallel", "arbitrary")),
        name="rwkv",
    )(rw.reshape(batch, seq, RW_COLS), shift0, m0, *consts, tri, ones_bd)
    return o.reshape(rows, RW_W), _m_to_state(m)


def _rwkv_step_kernel(rw_ref, shift_ref, s_ref, *rest):
    prm_refs = rest[:len(RW_PARAM_NAMES)]
    o_ref, sout_ref, t_scr, y_scr, tok_scr = rest[len(RW_PARAM_NAMES):]
    h = pl.program_id(0)

    @pl.when(h == 0)
    def _():
        p = {n: ref[...] for n, ref in zip(RW_PARAM_NAMES, prm_refs)}
        r, k_mod, v, logw, a_sig, g, kk, bonus = _rwkv_token_prep(rw_ref[...], shift_ref[...], p)
        t_scr[0] = r.T
        t_scr[1] = jnp.exp(logw).T
        t_scr[2] = k_mod.T
        t_scr[3] = v.T
        t_scr[4] = (-kk).T
        t_scr[5] = (kk * a_sig).T
        tok_scr[0] = bonus
        tok_scr[1] = g

    rows = pl.ds(pl.multiple_of(h * HS, HS), HS)
    r_h, w_h, k_h = t_scr[0, rows, :], t_scr[1, rows, :], t_scr[2, rows, :]
    a_h, b_h = t_scr[4, rows, :], t_scr[5, rows, :]

    def body(vv, carry):
        sv = s_ref[0, vv]
        vrow = t_scr[3, pl.ds(h * HS + vv, 1), :]
        sa = jnp.sum(sv * a_h, axis=0, keepdims=True)
        sn = sv * w_h + sa * b_h + vrow * k_h
        sout_ref[0, vv] = sn
        y_scr[pl.ds(h * HS + vv, 1), :] = jnp.sum(sn * r_h, axis=0, keepdims=True)
        return carry
    lax.fori_loop(0, HS, body, 0)

    @pl.when(h == HEADS - 1)
    def _():
        p = {n: ref[...] for n, ref in zip(RW_PARAM_NAMES, prm_refs)}
        o_ref[...] = _group_norm_out(y_scr[...].T, tok_scr[0], tok_scr[1], p).astype(o_ref.dtype)


def _rwkv_step(rw, shift, s_t, wts):
    nb = rw.shape[0]
    consts = [wts[n] for n in RW_PARAM_NAMES]
    const2 = lambda h: (0, 0)
    return pl.pallas_call(
        _rwkv_step_kernel,
        grid=(HEADS,),
        in_specs=[pl.BlockSpec((nb, RW_COLS), const2), pl.BlockSpec((nb, RW_COLS), const2),
                  pl.BlockSpec((1, HS, HS, nb), lambda h: (h, 0, 0, 0))]
        + [pl.BlockSpec(a.shape, const2) for a in consts],
        out_specs=[pl.BlockSpec((nb, RW_W), const2), pl.BlockSpec((1, HS, HS, nb), lambda h: (h, 0, 0, 0))],
        out_shape=[jax.ShapeDtypeStruct((nb, RW_W), BF16), jax.ShapeDtypeStruct(s_t.shape, F32)],
        scratch_shapes=[pltpu.VMEM((6, RW_W, nb), F32), pltpu.VMEM((RW_W, nb), F32), pltpu.VMEM((2, nb, RW_W), F32)],
        compiler_params=_params(("arbitrary",)),
        name="rwkv_step",
    )(rw, shift, s_t, *consts)


def _mlp_kernel(x_ref, om_ref, or_ref, gs_ref, wom_ref, wor_ref, wout_ref, gffn_ref, wup_ref, wdn_ref, gfin_ref,
                y_ref):
    a = _dot(om_ref[...], wom_ref[...])
    r = _dot(or_ref[...], wor_ref[...])
    mix = gs_ref[:, 0:D_MODEL] * a + gs_ref[:, D_MODEL:2 * D_MODEL] * r
    x1 = x_ref[...] + _dot(mix.astype(BF16), wout_ref[...])
    h2 = _rms(x1, gffn_ref[...]).astype(BF16)
    u = jnp.maximum(_dot(h2, wup_ref[...]), 0.0)
    x2 = x1 + _dot((u * u).astype(BF16), wdn_ref[...])
    y_ref[...] = _rms(x2, gfin_ref[...])


def _mlp(x, om, orw, gsig, wts, tm):
    rows = x.shape[0]
    row = lambda i: (i, 0)
    const = lambda i: (0, 0)
    full = lambda a: pl.BlockSpec(a.shape, const, pipeline_mode=pl.Buffered(1))
    names = ['w_o_mla', 'w_o_rwkv', 'w_out', 'g_ffn', 'w_up', 'w_down', 'g_final']
    return pl.pallas_call(
        _mlp_kernel,
        grid=(rows // tm,),
        in_specs=[pl.BlockSpec((tm, D_MODEL), row), pl.BlockSpec((tm, HEADS * VDIM), row),
                  pl.BlockSpec((tm, RW_W), row), pl.BlockSpec((tm, 2 * D_MODEL), row)]
        + [full(wts[n]) for n in names],
        out_specs=pl.BlockSpec((tm, D_MODEL), row),
        out_shape=jax.ShapeDtypeStruct((rows, D_MODEL), F32),
        compiler_params=_params(("parallel",)),
        name="mlp",
    )(x, om, orw, gsig, *[wts[n] for n in names])


SRC_QIN = 0
SRC_KVIN = Q_LORA
SRC_KR = Q_LORA + KV_LORA
SRC_RW = SRC_KR + ROPE
SRC_GATE = SRC_RW + RW_COLS


def _winprep_kernel(wt_ref, o_ref):
    def put(dst, src, n):
        for c in range(0, n, 256):
            w = min(256, n - c)
            o_ref[:, dst + c:dst + c + w] = wt_ref[src + c:src + c + w, :].T.astype(BF16)

    put(C_GATE, SRC_GATE, 2 * D_MODEL)
    put(C_RW, SRC_RW, RW_COLS)
    put(C_QIN, SRC_QIN, Q_LORA)
    put(C_KVIN, SRC_KVIN, KV_LORA)
    t = wt_ref[SRC_KR:SRC_KR + LANE, :].T
    lane = lax.broadcasted_iota(jnp.int32, t.shape, 1)
    kr0 = jnp.where(lane < ROPE, t, 0.0)
    o_ref[:, C_KR:C_KR + LANE] = (kr0 + pltpu.roll(kr0, NOPE, 1)).astype(BF16)


def _winprep(w_t):
    return pl.pallas_call(
        _winprep_kernel,
        out_shape=jax.ShapeDtypeStruct((D_MODEL, IN_PERM), BF16),
        compiler_params=pltpu.CompilerParams(vmem_limit_bytes=VMEM_LIMIT),
        name="winprep",
    )(w_t)


def _prep_weights(g_final, g_mix, w_in, g_q, w_uq, g_kv, w_uk, w_uv, w_o_mla, mu_shift, w0, w2, a0, a2, g2,
                  k_k, k_a, r_k, ln_w, ln_b, w_o_rwkv, w_out, g_ffn, w_up, w_down):
    row = lambda a: a.reshape(1, -1).astype(F32)
    w_in_p = _winprep(jnp.swapaxes(w_in, 0, 1))

    uq = w_uq.reshape(Q_LORA, HEADS, NOPE + ROPE)
    zq = jnp.zeros((Q_LORA, HEADS, HP - NOPE - ROPE), F32)
    w_uq_p = jnp.concatenate([uq, zq], axis=-1).reshape(Q_LORA, HEADS * HP).astype(BF16)

    uk = w_uk.reshape(KV_LORA, HEADS, NOPE)
    uk_pad = jnp.concatenate([uk, jnp.zeros((KV_LORA, HEADS, HP - NOPE), F32)], axis=-1).reshape(KV_LORA, HEADS * HP)
    w_ukv = uk_pad.astype(BF16)

    head = np.arange(RW_W) // HS
    seg = jnp.asarray(head[:, None] == head[None, :], BF16)
    return dict(
        g_mix=row(g_mix), w_in=w_in_p, g_q=row(g_q), w_uq=w_uq_p, g_kv=row(g_kv), w_ukv=w_ukv,
        w_uk=w_uk.astype(BF16), w_uv=w_uv.astype(BF16), w_uvt=w_uv.T.astype(BF16), w_o_mla=w_o_mla.astype(BF16),
        mu_shift=row(mu_shift), w0=row(w0), w2=w2.astype(BF16), a0=row(a0), a2=a2.astype(BF16),
        g2=g2.astype(BF16), k_k=row(k_k), k_a=row(k_a), r_k=row(r_k), ln_w=row(ln_w), ln_b=row(ln_b), seg=seg,
        w_o_rwkv=w_o_rwkv.astype(BF16), w_out=w_out.astype(BF16), g_ffn=row(g_ffn),
        w_up=w_up.astype(BF16), w_down=w_down.astype(BF16), g_final=row(g_final))


def _rope_tables(pos):
    inv = ROPE_BASE ** (-np.arange(0, ROPE, 2, dtype=np.float64) / ROPE)
    ang = np.asarray(pos, np.float64)[:, None] * inv[None, :]
    cos, sin = np.cos(ang), np.sin(ang)
    n = ang.shape[0]
    cc = np.concatenate([cos, cos], axis=1)
    ss = np.concatenate([-sin, sin], axis=1)
    z32 = np.zeros((n, ROPE))
    ctq = np.concatenate([np.ones((n, NOPE)), cc, z32], axis=1)
    stq = np.concatenate([np.zeros((n, NOPE)), ss, z32], axis=1)
    ctk = np.concatenate([cc, z32, cc, z32], axis=1)
    stk = np.concatenate([ss, z32, ss, z32], axis=1)
    return tuple(t.astype(np.float32) for t in (ctq, stq, ctk, stk))


def _row_tile(rows, cap):
    tm = min(rows, cap)
    assert rows % tm == 0
    return tm


def kernel(x_prompt, x_sample, cache_kv, page_table, state_wkv, state_shift, meta_tokens, g_final, g_mix, w_in, g_q, w_uq, g_kv, w_uk, w_uv, w_o_mla, mu_shift, w0, w2, a0, a2, g2, k_k, k_a, r_k, ln_w, ln_b, w_o_rwkv, w_out, g_ffn, w_up, w_down):
    depth = g_mix.shape[0]
    assert depth == 1
    bp, seq, _ = x_prompt.shape
    bd, s_s, _ = x_sample.shape
    assert s_s == 1
    n_pages = page_table.shape[1]
    past_len = n_pages * PAGE
    wts = _prep_weights(g_final, g_mix[0], w_in[0], g_q[0], w_uq[0], g_kv[0], w_uk[0], w_uv[0], w_o_mla[0],
                        mu_shift[0], w0[0], w2[0], a0[0], a2[0], g2[0], k_k[0], k_a[0], r_k[0], ln_w[0], ln_b[0],
                        w_o_rwkv[0], w_out[0], g_ffn[0], w_up[0], w_down[0])

    tab_m = _rope_tables(np.arange(N_META))
    _, rw_m, _, kvrow_m, kpad_m, vt_m = _proj(meta_tokens.astype(F32), tab_m, wts, N_META, BF16, False)
    rw_m_pad = jnp.concatenate([rw_m, jnp.zeros((HS - N_META, RW_COLS), F32)], axis=0)
    zero_shift = jnp.zeros((1, 1, RW_COLS), F32)
    zero_state = jnp.zeros((1, HEADS, HS, HS), F32)
    _, s_meta = _rwkv(rw_m_pad, zero_shift, zero_state, wts, 1, N_META)
    shift_meta = rw_m[N_META - 1:N_META].reshape(1, 1, RW_COLS)

    rows_p = bp * seq
    tm = _row_tile(seq, ROW_TILE)
    tq = _row_tile(seq, KV_TILE)
    xp = x_prompt.reshape(rows_p, D_MODEL)
    ctq, stq, ctk, stk = _rope_tables(N_META + np.arange(seq))
    tab_p = (ctq * np.float32(Q_PRESCALE), stq * np.float32(Q_PRESCALE), ctk, stk)
    gsig, rw, q, kvrow, kpad, vt = _proj(xp, tab_p, wts, tm, BF16, True)
    o_mla = _attn(q, kpad, vt, kpad_m, vt_m, bp, seq, tq)
    o_rwkv, s_p = _rwkv(rw, shift_meta, s_meta, wts, bp, None)
    y_prompt = _mlp(xp, o_mla, o_rwkv, gsig, wts, tm).reshape(bp, seq, D_MODEL)
    kv_prompt = jnp.concatenate([jnp.broadcast_to(kvrow_m[None], (bp, N_META, KV_W)),
                                 kvrow.reshape(bp, seq, KV_W)], axis=1)[None]
    shift_prompt = rw.reshape(bp, seq, RW_COLS)[:, -1][None]

    xs = x_sample.reshape(bd, D_MODEL)
    tab_s = _rope_tables(np.full((bd,), past_len))
    gsig_s, rw_s, q_s, kvrow_s, _, _ = _proj(xs, tab_s, wts, bd, F32, False)
    qf = jnp.transpose(_qlat(q_s, wts['w_uk']), (1, 0, 2))
    cache = jnp.swapaxes(cache_kv.reshape(cache_kv.shape[1], PAGE, KV_W), 1, 2)
    o_mla_s = _sattn(page_table, qf, kvrow_s.reshape(bd, 1, KV_W), wts['w_uv'], cache).reshape(bd, HEADS * VDIM)
    s_t = jnp.transpose(state_wkv[0], (1, 2, 3, 0))
    o_rwkv_s, s_t_new = _rwkv_step(rw_s, state_shift[0], s_t, wts)
    s_s_new = jnp.transpose(s_t_new, (3, 0, 1, 2))
    y_sample = _mlp(xs, o_mla_s, o_rwkv_s, gsig_s, wts, bd).reshape(bd, 1, D_MODEL)

    return (y_prompt, y_sample,
            kv_prompt.astype(cache_kv.dtype), s_p[None].astype(state_wkv.dtype),
            shift_prompt.astype(state_shift.dtype),
            kvrow_s.reshape(1, bd, 1, KV_W).astype(cache_kv.dtype), s_s_new[None].astype(state_wkv.dtype),
            rw_s[None].astype(state_shift.dtype))
```

```python
import functools

import numpy as np
import jax
import jax.numpy as jnp
from jax import lax
from jax.experimental import pallas as pl
from jax.experimental.pallas import tpu as pltpu

F32 = jnp.float32
BF16 = jnp.bfloat16

D_MODEL = 1024
N_META = 16
PAGE = 128
HEADS = 8
NOPE = 64
ROPE = 32
VDIM = 64
Q_LORA = 384
KV_LORA = 256
KV_W = KV_LORA + ROPE
ROPE_BASE = 10000.0
SM_SCALE = (NOPE + ROPE) ** -0.5
HS = 64
RW_W = HEADS * HS
W_LORA = 64
A_LORA = 64
G_LORA = 128
RW_COLS = 3 * RW_W + W_LORA + A_LORA + G_LORA
GN_EPS = 64e-5
D_FF = 4 * D_MODEL
NORM_EPS = 1e-6
NEG_INF = -1e30
DECAY_SCALE = 0.6065306597126334
Q_PRESCALE = SM_SCALE * 1.4426950408889634

LANE = 128
HP = 128
VMEM_LIMIT = 56 * 1024 * 1024
KV_TILE = 256
ROW_TILE = 512

C_GATE = 0
C_RW = 2 * D_MODEL
C_QIN = C_RW + RW_COLS
C_KVIN = C_QIN + Q_LORA
C_KR = C_KVIN + KV_LORA
IN_PERM = C_KR + LANE


def _dot(a, b):
    return jnp.dot(a, b, preferred_element_type=F32)


def _dot_nt(a, b):
    return lax.dot_general(a, b, (((1,), (1,)), ((), ())), preferred_element_type=F32)


def _dot_tn(a, b):
    return lax.dot_general(a, b, (((0,), (0,)), ((), ())), preferred_element_type=F32)


def _split2(x):
    hi = x.astype(BF16)
    lo = (x - hi.astype(F32)).astype(BF16)
    return hi, lo


def _split3(x):
    x1 = x.astype(BF16)
    r1 = x - x1.astype(F32)
    x2 = r1.astype(BF16)
    x3 = (r1 - x2.astype(F32)).astype(BF16)
    return x1, x2, x3


def _rms(x, g):
    return x * lax.rsqrt(jnp.mean(x * x, axis=-1, keepdims=True) + NORM_EPS) * g


def _params(sem):
    return pltpu.CompilerParams(dimension_semantics=sem, vmem_limit_bytes=VMEM_LIMIT)


def _proj_kernel(x_ref, gmix_ref, win_ref, gq_ref, wuq_ref, gkv_ref, wukv_ref, wuvt_ref,
                 ctq_ref, stq_ref, ctk_ref, stk_ref,
                 gsig_ref, rw_ref, q_ref, kvrow_ref, kpad_ref, vt_ref, *, q_transposed):
    h = _rms(x_ref[...], gmix_ref[...]).astype(BF16)
    tw = vt_ref.shape[2]
    lane = lax.broadcasted_iota(jnp.int32, (x_ref.shape[0], LANE), 1)
    first_half = (lane % ROPE) < ROPE // 2

    def rope(t, cos_t, sin_t):
        swapped = jnp.where(first_half, pltpu.roll(t, LANE - ROPE // 2, 1), pltpu.roll(t, ROPE // 2, 1))
        return t * cos_t + swapped * sin_t

    zq = _dot(h, win_ref[:, C_QIN:C_QIN + Q_LORA])
    zkv = _dot(h, win_ref[:, C_KVIN:C_KVIN + KV_LORA])
    zkr = _dot(h, win_ref[:, C_KR:C_KR + LANE])
    step = 512
    for c in range(0, 2 * D_MODEL, step):
        gsig_ref[:, c:c + step] = jax.nn.sigmoid(_dot(h, win_ref[:, C_GATE + c:C_GATE + c + step])).astype(BF16)

    qn = _rms(zq, gq_ref[...]).astype(BF16)
    ckv = _rms(zkv, gkv_ref[...])
    cb = ckv.astype(BF16)
    qa = _dot(qn, wuq_ref[...])
    kn = _dot(cb, wukv_ref[...])
    for sub in range(vt_ref.shape[0]):
        vt_ref[sub] = _dot_nt(wuvt_ref[...], cb[sub * tw:(sub + 1) * tw]).astype(BF16)

    for c in range(0, RW_COLS, 256):
        rw_ref[:, c:c + 256] = _dot(h, win_ref[:, C_RW + c:C_RW + c + 256])

    ctq = ctq_ref[...]
    stq = stq_ref[...]
    for hh in range(HEADS):
        hs = slice(hh * HP, (hh + 1) * HP)
        qh = rope(qa[:, hs], ctq, stq)
        if q_transposed:
            for sub in range(q_ref.shape[0]):
                q_ref[sub, hs, :] = qh[sub * tw:(sub + 1) * tw].T.astype(q_ref.dtype)
        else:
            q_ref[:, hs] = qh.astype(q_ref.dtype)
    kvrow_ref[:, 0:KV_LORA] = ckv
    kr = rope(zkr, ctk_ref[...], stk_ref[...])
    kvrow_ref[:, KV_LORA:KV_W] = kr[:, 0:ROPE]
    kr_head = jnp.where((lane >= NOPE) & (lane < NOPE + ROPE), kr, 0.0)
    for hh in range(HEADS):
        hs = slice(hh * HP, (hh + 1) * HP)
        kpad_ref[:, hs] = (kn[:, hs] + kr_head).astype(BF16)


def _proj(x, tables, wts, tm, q_dtype, q_transposed):
    rows = x.shape[0]
    nt = tables[0].shape[0] // tm
    tw = min(tm, KV_TILE)
    row = lambda i: (i, 0)
    const = lambda i: (0, 0)
    tab = lambda i: (i % nt, 0)
    full = lambda a: pl.BlockSpec(a.shape, const, pipeline_mode=pl.Buffered(1))
    in_specs = [pl.BlockSpec((tm, D_MODEL), row), full(wts['g_mix']), full(wts['w_in']), full(wts['g_q']),
                full(wts['w_uq']), full(wts['g_kv']), full(wts['w_ukv']), full(wts['w_uvt'])]
    in_specs += [pl.BlockSpec((tm, LANE), tab)] * 4
    widths = [(2 * D_MODEL, BF16), (RW_COLS, F32), (HEADS * HP, q_dtype), (KV_W, F32), (HEADS * HP, BF16)]
    nt_rows = rows // tm
    tiled = lambda w: pl.BlockSpec((tm // tw, w, tw), lambda i: (i, 0, 0))
    out_specs = [pl.BlockSpec((tm, w), row) for w, _ in widths] + [tiled(HEADS * VDIM)]
    out_shape = [jax.ShapeDtypeStruct((rows, w), dt) for w, dt in widths]
    out_shape += [jax.ShapeDtypeStruct((rows // tw, HEADS * VDIM, tw), BF16)]
    if q_transposed:
        out_specs[2] = tiled(HEADS * HP)
        out_shape[2] = jax.ShapeDtypeStruct((rows // tw, HEADS * HP, tw), q_dtype)
    return pl.pallas_call(
        functools.partial(_proj_kernel, q_transposed=q_transposed),
        grid=(nt_rows,),
        in_specs=in_specs,
        out_specs=out_specs,
        out_shape=out_shape,
        compiler_params=_params(("parallel",)),
        name="proj",
    )(x, wts['g_mix'], wts['w_in'], wts['g_q'], wts['w_uq'], wts['g_kv'], wts['w_ukv'], wts['w_uvt'], *tables)


def _attn_kernel(q_ref, k_ref, vt_ref, km_ref, vmt_ref, eye_ref, o_ref, *, tq):
    i = pl.program_id(1)
    qs = [q_ref[0, hh * HP:(hh + 1) * HP, :] for hh in range(HEADS)]

    def update(carry, groups):
        ss = [[_dot(k_tiles[hh], qs[hh]) for hh in range(HEADS)] for k_tiles, _, _ in groups]
        ps, stats = [], []
        for hh in range(HEADS):
            m, l, _ = carry[hh]
            sm = [s_g[hh] if g[2] is None else jnp.where(g[2], s_g[hh], NEG_INF) for s_g, g in zip(ss, groups)]
            m_new = m
            for s in sm:
                m_new = jnp.maximum(m_new, jnp.max(s, axis=0, keepdims=True))
            alpha = jnp.exp2(m - m_new)
            p = [jnp.exp2(s - m_new) for s in sm]
            l = l * alpha
            for pg in p:
                l = l + jnp.sum(pg, axis=0, keepdims=True)
            stats.append((m_new, l, alpha))
            ps.append([pg.astype(BF16) for pg in p])
        pv = [[_dot(g[1][hh], ps[hh][gi]) for hh in range(HEADS)] for gi, g in enumerate(groups)]
        out = []
        for hh in range(HEADS):
            acc = carry[hh][2] * stats[hh][2]
            for gi in range(len(groups)):
                acc = acc + pv[gi][hh]
            out.append((stats[hh][0], stats[hh][1], acc))
        return tuple(out)

    def kv_tile(j, mask):
        rows = pl.ds(pl.multiple_of(j * tq, tq), tq)
        return ([k_ref[rows, hh * HP:(hh + 1) * HP] for hh in range(HEADS)],
                [vt_ref[j, hh * VDIM:(hh + 1) * VDIM, :] for hh in range(HEADS)], mask)

    init = (jnp.full((1, tq), NEG_INF, F32), jnp.zeros((1, tq), F32), jnp.zeros((VDIM, tq), F32))
    carry = lax.fori_loop(0, i, lambda j, c: update(c, [kv_tile(j, None)]), (init,) * HEADS)
    key = lax.broadcasted_iota(jnp.int32, (tq, tq), 0)
    qry = lax.broadcasted_iota(jnp.int32, (tq, tq), 1)
    meta = ([km_ref[:, hh * HP:(hh + 1) * HP] for hh in range(HEADS)],
            [vmt_ref[0, hh * VDIM:(hh + 1) * VDIM, :] for hh in range(HEADS)], None)
    carry = update(carry, [meta, kv_tile(i, key <= qry)])

    outs = []
    for hh in range(HEADS):
        m, l, acc = carry[hh]
        outs.append(_dot_tn((acc / l).astype(BF16), eye_ref[...]))
    o_ref[...] = jnp.concatenate(outs, axis=-1).astype(o_ref.dtype)


def _attn(q, kpad, vt, km, vmt, batch, seq, tq):
    nq = seq // tq
    eye = jnp.asarray(np.eye(VDIM), BF16)
    kern = functools.partial(_attn_kernel, tq=tq)
    return pl.pallas_call(
        kern,
        grid=(batch, nq),
        in_specs=[
            pl.BlockSpec((1, HEADS * HP, tq), lambda b, i: (b * nq + i, 0, 0)),
            pl.BlockSpec((seq, HEADS * HP), lambda b, i: (b, 0)),
            pl.BlockSpec((nq, HEADS * VDIM, tq), lambda b, i: (b, 0, 0)),
            pl.BlockSpec(km.shape, lambda b, i: (0, 0)),
            pl.BlockSpec(vmt.shape, lambda b, i: (0, 0, 0)),
            pl.BlockSpec(eye.shape, lambda b, i: (0, 0)),
        ],
        out_specs=pl.BlockSpec((tq, HEADS * VDIM), lambda b, i: (b * nq + i, 0)),
        out_shape=jax.ShapeDtypeStruct((batch * seq, HEADS * VDIM), BF16),
        compiler_params=_params(("parallel", "arbitrary")),
        name="attn",
    )(q, kpad, vt, km, vmt, eye)


def _qlat_kernel(q_ref, wuk_ref, o_ref):
    for hh in range(HEADS):
        qn = q_ref[:, hh * HP:hh * HP + NOPE].astype(BF16)
        o_ref[hh, :, 0:KV_LORA] = _dot_nt(qn, wuk_ref[:, hh * NOPE:(hh + 1) * NOPE]) * SM_SCALE
        o_ref[hh, :, KV_LORA:KV_W] = q_ref[:, hh * HP + NOPE:hh * HP + NOPE + ROPE] * SM_SCALE


def _qlat(q, wuk):
    rows = q.shape[0]
    return pl.pallas_call(
        _qlat_kernel,
        out_shape=jax.ShapeDtypeStruct((HEADS, rows, KV_W), F32),
        name="qlat",
    )(q, wuk)


SATTN_SLOTS = 3
SATTN_CHUNK = 1024


def _sattn_kernel(pt_ref, qf_ref, kvn_ref, wuv_ref, cache_ref, o_ref, buf, sem, *, n_pages, chunk):
    b = pl.program_id(0)
    nb = pl.num_programs(0)
    ahead = SATTN_SLOTS - 1

    def page_copy(page, slot, j):
        return pltpu.make_async_copy(cache_ref.at[page], buf.at[slot, :, pl.ds(j * PAGE, PAGE)], sem.at[slot])

    def start_pages(bb, slot, lo, hi):
        for j in range(lo, hi):
            page_copy(pt_ref[bb, j], slot, j).start(priority=j % 2)

    def wait_slot(slot):
        for j in range(n_pages):
            page_copy(0, slot, j).wait()

    @pl.when(b == 0)
    def _():
        for first in range(ahead):
            start_pages(jnp.minimum(first, nb - 1), first, 0, n_pages)

    slot = b % SATTN_SLOTS
    wait_slot(slot)
    nxt = jnp.minimum(b + ahead, nb - 1)
    nslot = (b + ahead) % SATTN_SLOTS
    n_chunks = (n_pages * PAGE) // chunk
    per_chunk = n_pages // n_chunks

    qf = qf_ref[0].astype(BF16)
    kvn = kvn_ref[0].astype(BF16).astype(F32)
    s_own = jnp.sum(qf.astype(F32) * kvn, axis=-1, keepdims=True)
    kts, ss = [], []
    for ci in range(n_chunks):
        kt = buf[slot, :, ci * chunk:(ci + 1) * chunk].astype(BF16)
        kts.append(kt)
        ss.append(_dot(qf, kt))
        start_pages(nxt, nslot, ci * per_chunk, (ci + 1) * per_chunk)
    m = s_own
    for sc in ss:
        m = jnp.maximum(m, jnp.max(sc, axis=-1, keepdims=True))
    ps = [jnp.exp(sc - m) for sc in ss]
    p_own = jnp.exp(s_own - m)
    l = p_own
    for pc in ps:
        l = l + jnp.sum(pc, axis=-1, keepdims=True)
    acc = p_own.astype(BF16).astype(F32) * kvn[:, 0:KV_LORA]
    for pc, kt in zip(ps, kts):
        acc = acc + _dot_nt(pc.astype(BF16), kt[0:KV_LORA, :])

    @pl.when(b == nb - 1)
    def _():
        for extra in range(1, SATTN_SLOTS):
            wait_slot((b + extra) % SATTN_SLOTS)

    o_lat = (acc / l).astype(BF16)
    full = _dot(o_lat, wuv_ref[...])
    r = lax.broadcasted_iota(jnp.int32, full.shape, 0)
    cidx = lax.broadcasted_iota(jnp.int32, full.shape, 1)
    o_ref[0] = jnp.sum(jnp.where(cidx // VDIM == r, full, 0.0), axis=0, keepdims=True).astype(o_ref.dtype)


def _sattn(page_table, qf, kvn, wuv, cache):
    nb, n_pages = page_table.shape
    assert n_pages % 2 == 0
    chunk = min(SATTN_CHUNK, n_pages * PAGE)
    assert (n_pages * PAGE) % chunk == 0
    kern = functools.partial(_sattn_kernel, n_pages=n_pages, chunk=chunk)
    grid_spec = pltpu.PrefetchScalarGridSpec(
        num_scalar_prefetch=1,
        grid=(nb,),
        in_specs=[
            pl.BlockSpec((1, HEADS, KV_W), lambda b, pt: (b, 0, 0)),
            pl.BlockSpec((1, 1, KV_W), lambda b, pt: (b, 0, 0)),
            pl.BlockSpec(wuv.shape, lambda b, pt: (0, 0)),
            pl.BlockSpec(memory_space=pl.ANY),
        ],
        out_specs=pl.BlockSpec((1, 1, HEADS * VDIM), lambda b, pt: (b, 0, 0)),
        scratch_shapes=[pltpu.VMEM((SATTN_SLOTS, KV_W, n_pages * PAGE), F32),
                        pltpu.SemaphoreType.DMA((SATTN_SLOTS,))],
    )
    return pl.pallas_call(
        kern,
        grid_spec=grid_spec,
        out_shape=jax.ShapeDtypeStruct((nb, 1, HEADS * VDIM), BF16),
        compiler_params=_params(("arbitrary",)),
        name="sattn",
    )(page_table, qf, kvn, wuv, cache)


GROUP = 4
RWKV_BATCHES_PER_STEP = 8
GW = GROUP * HS
RW_PARAM_NAMES = ['mu_shift', 'w0', 'w2', 'a0', 'a2', 'g2', 'k_k', 'k_a', 'r_k', 'ln_w', 'ln_b', 'seg']


def _seg_sum(x, seg):
    return _dot(x.astype(BF16), seg)


def _rwkv_token_prep(cur, prev, p):
    z = cur + (prev - cur) * p['mu_shift']
    r = z[:, 0:RW_W]
    k = z[:, RW_W:2 * RW_W]
    v = z[:, 2 * RW_W:3 * RW_W]
    w_in = z[:, 3 * RW_W:3 * RW_W + W_LORA]
    a_in = z[:, 3 * RW_W + W_LORA:3 * RW_W + W_LORA + A_LORA]
    g_in = z[:, 3 * RW_W + W_LORA + A_LORA:RW_COLS]
    w_log = p['w0'] + _dot(jnp.tanh(w_in).astype(BF16), p['w2'])
    logw = -DECAY_SCALE * jax.nn.sigmoid(w_log)
    a_sig = jax.nn.sigmoid(p['a0'] + _dot(a_in.astype(BF16), p['a2']))
    g = _dot(jax.nn.sigmoid(g_in).astype(BF16), p['g2'])
    kk = k * p['k_k']
    kk = kk * lax.rsqrt(jnp.maximum(_seg_sum(kk * kk, p['seg']), 1e-24))
    k_mod = k * (1.0 + (a_sig - 1.0) * p['k_a'])
    bonus = _seg_sum(r * k_mod * p['r_k'], p['seg']) * v
    return r, k_mod, v, logw, a_sig, g, kk, bonus


def _group_norm_out(y, bonus, g, p):
    mean = _seg_sum(y, p['seg']) * (1.0 / HS)
    d = y - mean
    var = _seg_sum(d * d, p['seg']) * (1.0 / HS)
    yn = d * lax.rsqrt(var + GN_EPS) * p['ln_w'] + p['ln_b']
    return (yn + bonus) * g


def _rwkv_kernel(rw_ref, shift_ref, m0_ref, *rest, n_valid, nbat):
    prm_refs = rest[:len(RW_PARAM_NAMES)]
    tri_ref, ones_ref, o_ref, mout_ref, prev_scr, m_scr = rest[len(RW_PARAM_NAMES):]
    p = {n: ref[...] for n, ref in zip(RW_PARAM_NAMES, prm_refs)}
    chunk = HS
    ngrp = HEADS // GROUP
    c = pl.program_id(1)
    nc = pl.num_programs(1)

    @pl.when(c == 0)
    def _():
        for bi in range(nbat):
            prev_scr[bi] = shift_ref[0]
            m_scr[bi] = m0_ref[0]

    curs, prevs = [], []
    for bi in range(nbat):
        cur_b = rw_ref[bi]
        rowi = lax.broadcasted_iota(jnp.int32, cur_b.shape, 0)
        prevs.append(jnp.where(rowi == 0, prev_scr[bi], pltpu.roll(cur_b, 1, 0)))
        prev_scr[bi] = cur_b[chunk - 1:chunk, :]
        curs.append(cur_b)
    r, k_mod, v, logw, a_sig, g, kk, bonus = _rwkv_token_prep(
        jnp.concatenate(curs, axis=0), jnp.concatenate(prevs, axis=0), p)

    if n_valid is not None:
        step = lax.broadcasted_iota(jnp.int32, (nbat * chunk, RW_W), 0) % chunk
        live = (c * chunk + step) < n_valid
        logw = jnp.where(live, logw, 0.0)
        kk = jnp.where(live, kk, 0.0)
        k_mod = jnp.where(live, k_mod, 0.0)
        v = jnp.where(live, v, 0.0)

    lw3 = _split3(logw)
    seq_rows = [slice(bi * chunk, (bi + 1) * chunk) for bi in range(nbat)]
    gcum = jnp.concatenate([sum(_dot(tri_ref[...], t[rs]) for t in lw3) for rs in seq_rows], axis=0)
    g_end = jnp.concatenate([jnp.broadcast_to(gcum[rs][chunk - 1:chunk, :], (chunk, RW_W)) for rs in seq_rows], axis=0)
    r_t = r * jnp.exp(gcum)
    a_t = -kk * jnp.exp(gcum - logw)
    eng = jnp.exp(-gcum)
    b_t = kk * a_sig * eng
    k_t = k_mod * eng
    decay_end = jnp.exp(g_end)
    b_e = b_t * decay_end
    k_e = k_t * decay_end

    ti = lax.broadcasted_iota(jnp.int32, (chunk, GW), 0)
    lane = lax.broadcasted_iota(jnp.int32, (chunk, GW), 1)
    si = lane % HS
    lane_head = lane // HS
    strict = si < ti
    incl = si <= ti
    eye = (si == ti).astype(F32)

    def bdr(zf):
        zb = zf.astype(BF16)
        return jnp.concatenate([jnp.where(lane_head == hh, zb, jnp.zeros_like(zb)) for hh in range(GROUP)], axis=0)

    streams = [(bi, gi) for bi in range(nbat) for gi in range(ngrp)]
    blk = lambda t, s: t[s[0] * chunk:(s[0] + 1) * chunk, s[1] * GW:(s[1] + 1) * GW]
    cat0 = lambda xs: jnp.concatenate(xs, axis=0).astype(BF16)
    cat1 = lambda xs: jnp.concatenate(xs, axis=1).astype(BF16)

    ar = [cat0([blk(a_t, s), blk(r_t, s)]) for s in streams]
    ab = [_dot_nt(ar[i], bdr(blk(b_t, s))) for i, s in enumerate(streams)]
    ak = [_dot_nt(ar[i], bdr(blk(k_t, s))) for i, s in enumerate(streams)]
    n = [jnp.where(strict, t[0:chunk], 0.0) for t in ab]
    a_rb = [jnp.where(incl, t[chunk:2 * chunk], 0.0) for t in ab]
    a_ak = [jnp.where(strict, t[0:chunk], 0.0) for t in ak]
    a_rk = [jnp.where(incl, t[chunk:2 * chunk], 0.0) for t in ak]

    x = [eye + t for t in n]
    pw = [_dot(t.astype(BF16), bdr(t)) for t in n]
    for _ in range(4):
        z = [_dot(cat0([xi, pi]), bdr(pi)) for xi, pi in zip(x, pw)]
        x = [xi + zi[0:chunk] for xi, zi in zip(x, z)]
        pw = [zi[chunk:2 * chunk] for zi in z]
    z = [_dot(xi.astype(BF16), bdr(pi)) for xi, pi in zip(x, pw)]
    x = [xi + zi for xi, zi in zip(x, z)]

    m0 = [m_scr[s[0], s[1]] for s in streams]
    bm = [bdr(t) for t in m0]
    bv = [bdr(blk(v, s)) for s in streams]
    rhs = [_dot(cat1([blk(a_t, s), a_ak[i]]), jnp.concatenate([bm[i], bv[i]], axis=0)) for i, s in enumerate(streams)]
    u = [_dot(xi.astype(BF16), bdr(ri)) for xi, ri in zip(x, rhs)]
    y = [_dot(cat1([blk(r_t, s), a_rb[i], a_rk[i]]), jnp.concatenate([bm[i], bdr(u[i]), bv[i]], axis=0))
         for i, s in enumerate(streams)]
    full = [_dot_tn(cat0([blk(b_e, s), blk(k_e, s)]), cat0([u[i], blk(v, s)])) for i, s in enumerate(streams)]
    dg = jnp.concatenate([eye * blk(decay_end, s)[0:1] for s in streams], axis=0)
    gm_all = sum(_dot(t, ones_ref[...]) for t in _split3(dg))
    gm = [gm_all[i * chunk:(i + 1) * chunk] for i in range(len(streams))]
    for i, s in enumerate(streams):
        upd = jnp.where(lane_head == 0, full[i][0:HS], 0.0)
        for hh in range(1, GROUP):
            upd = upd + jnp.where(lane_head == hh, full[i][hh * HS:(hh + 1) * HS], 0.0)
        m_scr[s[0], s[1]] = gm[i] * m0[i] + upd

    y_all = jnp.concatenate([jnp.concatenate([y[bi * ngrp + gi] for gi in range(ngrp)], axis=1)
                             for bi in range(nbat)], axis=0)
    out = _group_norm_out(y_all, bonus, g, p).astype(o_ref.dtype)
    for bi in range(nbat):
        o_ref[bi] = out[seq_rows[bi]]

    @pl.when(c == nc - 1)
    def _():
        mout_ref[...] = m_scr[...]


def _state_to_m(s):
    b = s.shape[0]
    return jnp.transpose(s.reshape(b, HEADS // GROUP, GROUP, HS, HS), (0, 1, 4, 2, 3)).reshape(b, HEADS // GROUP, HS, GW)


def _m_to_state(m):
    b = m.shape[0]
    return jnp.transpose(m.reshape(b, HEADS // GROUP, HS, GROUP, HS), (0, 1, 3, 4, 2)).reshape(b, HEADS, HS, HS)


def _rwkv(rw, shift0, s0, wts, batch, n_valid):
    chunk = HS
    rows = rw.shape[0]
    seq = rows // batch
    nc = seq // chunk
    nbat = RWKV_BATCHES_PER_STEP if batch % RWKV_BATCHES_PER_STEP == 0 else 1
    const2 = lambda b, c: (0, 0)
    consts = [wts[n] for n in RW_PARAM_NAMES]
    tri = jnp.asarray(np.arange(chunk)[:, None] >= np.arange(chunk)[None, :], BF16)
    hd = np.arange(GW) // HS
    ones_bd = jnp.asarray(hd[:, None] == hd[None, :], BF16)
    m0 = _state_to_m(s0)
    ngrp = HEADS // GROUP
    kern = functools.partial(_rwkv_kernel, n_valid=n_valid, nbat=nbat)
    o, m = pl.pallas_call(
        kern,
        grid=(batch // nbat, nc),
        in_specs=[pl.BlockSpec((nbat, chunk, RW_COLS), lambda b, c: (b, c, 0)),
                  pl.BlockSpec((1, 1, RW_COLS), lambda b, c: (0, 0, 0)),
                  pl.BlockSpec((1, ngrp, HS, GW), lambda b, c: (0, 0, 0, 0))]
        + [pl.BlockSpec(a.shape, const2) for a in consts]
        + [pl.BlockSpec(tri.shape, const2), pl.BlockSpec(ones_bd.shape, const2)],
        out_specs=[pl.BlockSpec((nbat, chunk, RW_W), lambda b, c: (b, c, 0)),
                   pl.BlockSpec((nbat, ngrp, HS, GW), lambda b, c: (b, 0, 0, 0))],
        out_shape=[jax.ShapeDtypeStruct((batch, seq, RW_W), BF16),
                   jax.ShapeDtypeStruct((batch, ngrp, HS, GW), F32)],
        scratch_shapes=[pltpu.VMEM((nbat, 1, RW_COLS), F32), pltpu.VMEM((nbat, ngrp, HS, GW), F32)],
        compiler_params=_params(("parallel", "arbitrary")),
        name="rwkv",
    )(rw.reshape(batch, seq, RW_COLS), shift0, m0, *consts, tri, ones_bd)
    return o.reshape(rows, RW_W), _m_to_state(m)


def _rwkv_step_kernel(rw_ref, shift_ref, s_ref, *rest):
    prm_refs = rest[:len(RW_PARAM_NAMES)]
    o_ref, sout_ref, t_scr, y_scr, tok_scr = rest[len(RW_PARAM_NAMES):]
    h = pl.program_id(0)

    @pl.when(h == 0)
    def _():
        p = {n: ref[...] for n, ref in zip(RW_PARAM_NAMES, prm_refs)}
        r, k_mod, v, logw, a_sig, g, kk, bonus = _rwkv_token_prep(rw_ref[...], shift_ref[...], p)
        t_scr[0] = r.T
        t_scr[1] = jnp.exp(logw).T
        t_scr[2] = k_mod.T
        t_scr[3] = v.T
        t_scr[4] = (-kk).T
        t_scr[5] = (kk * a_sig).T
        tok_scr[0] = bonus
        tok_scr[1] = g

    rows = pl.ds(pl.multiple_of(h * HS, HS), HS)
    r_h, w_h, k_h = t_scr[0, rows, :], t_scr[1, rows, :], t_scr[2, rows, :]
    a_h, b_h = t_scr[4, rows, :], t_scr[5, rows, :]

    def body(vv, carry):
        sv = s_ref[0, vv]
        vrow = t_scr[3, pl.ds(h * HS + vv, 1), :]
        sa = jnp.sum(sv * a_h, axis=0, keepdims=True)
        sn = sv * w_h + sa * b_h + vrow * k_h
        sout_ref[0, vv] = sn
        y_scr[pl.ds(h * HS + vv, 1), :] = jnp.sum(sn * r_h, axis=0, keepdims=True)
        return carry
    lax.fori_loop(0, HS, body, 0, unroll=4)

    @pl.when(h == HEADS - 1)
    def _():
        p = {n: ref[...] for n, ref in zip(RW_PARAM_NAMES, prm_refs)}
        o_ref[...] = _group_norm_out(y_scr[...].T, tok_scr[0], tok_scr[1], p).astype(o_ref.dtype)


def _rwkv_step(rw, shift, s_t, wts):
    nb = rw.shape[0]
    consts = [wts[n] for n in RW_PARAM_NAMES]
    const2 = lambda h: (0, 0)
    return pl.pallas_call(
        _rwkv_step_kernel,
        grid=(HEADS,),
        in_specs=[pl.BlockSpec((nb, RW_COLS), const2), pl.BlockSpec((nb, RW_COLS), const2),
                  pl.BlockSpec((1, HS, HS, nb), lambda h: (h, 0, 0, 0))]
        + [pl.BlockSpec(a.shape, const2) for a in consts],
        out_specs=[pl.BlockSpec((nb, RW_W), const2), pl.BlockSpec((1, HS, HS, nb), lambda h: (h, 0, 0, 0))],
        out_shape=[jax.ShapeDtypeStruct((nb, RW_W), BF16), jax.ShapeDtypeStruct(s_t.shape, F32)],
        scratch_shapes=[pltpu.VMEM((6, RW_W, nb), F32), pltpu.VMEM((RW_W, nb), F32), pltpu.VMEM((2, nb, RW_W), F32)],
        compiler_params=_params(("arbitrary",)),
        name="rwkv_step",
    )(rw, shift, s_t, *consts)


def _mlp_kernel(x_ref, om_ref, or_ref, gs_ref, wom_ref, wor_ref, wout_ref, gffn_ref, wup_ref, wdn_ref, gfin_ref,
                y_ref):
    a = _dot(om_ref[...], wom_ref[...])
    r = _dot(or_ref[...], wor_ref[...])
    mix = gs_ref[:, 0:D_MODEL] * a + gs_ref[:, D_MODEL:2 * D_MODEL] * r
    x1 = x_ref[...] + _dot(mix.astype(BF16), wout_ref[...])
    h2 = _rms(x1, gffn_ref[...]).astype(BF16)
    u = jnp.maximum(_dot(h2, wup_ref[...]), 0.0)
    x2 = x1 + _dot((u * u).astype(BF16), wdn_ref[...])
    y_ref[...] = _rms(x2, gfin_ref[...])


def _mlp(x, om, orw, gsig, wts, tm):
    rows = x.shape[0]
    row = lambda i: (i, 0)
    const = lambda i: (0, 0)
    full = lambda a: pl.BlockSpec(a.shape, const, pipeline_mode=pl.Buffered(1))
    names = ['w_o_mla', 'w_o_rwkv', 'w_out', 'g_ffn', 'w_up', 'w_down', 'g_final']
    return pl.pallas_call(
        _mlp_kernel,
        grid=(rows // tm,),
        in_specs=[pl.BlockSpec((tm, D_MODEL), row), pl.BlockSpec((tm, HEADS * VDIM), row),
                  pl.BlockSpec((tm, RW_W), row), pl.BlockSpec((tm, 2 * D_MODEL), row)]
        + [full(wts[n]) for n in names],
        out_specs=pl.BlockSpec((tm, D_MODEL), row),
        out_shape=jax.ShapeDtypeStruct((rows, D_MODEL), F32),
        compiler_params=_params(("parallel",)),
        name="mlp",
    )(x, om, orw, gsig, *[wts[n] for n in names])


SRC_QIN = 0
SRC_KVIN = Q_LORA
SRC_KR = Q_LORA + KV_LORA
SRC_RW = SRC_KR + ROPE
SRC_GATE = SRC_RW + RW_COLS


def _winprep_kernel(wt_ref, o_ref):
    def put(dst, src, n):
        for c in range(0, n, 256):
            w = min(256, n - c)
            o_ref[:, dst + c:dst + c + w] = wt_ref[src + c:src + c + w, :].T.astype(BF16)

    put(C_GATE, SRC_GATE, 2 * D_MODEL)
    put(C_RW, SRC_RW, RW_COLS)
    put(C_QIN, SRC_QIN, Q_LORA)
    put(C_KVIN, SRC_KVIN, KV_LORA)
    t = wt_ref[SRC_KR:SRC_KR + LANE, :].T
    lane = lax.broadcasted_iota(jnp.int32, t.shape, 1)
    kr0 = jnp.where(lane < ROPE, t, 0.0)
    o_ref[:, C_KR:C_KR + LANE] = (kr0 + pltpu.roll(kr0, NOPE, 1)).astype(BF16)


def _winprep(w_t):
    return pl.pallas_call(
        _winprep_kernel,
        out_shape=jax.ShapeDtypeStruct((D_MODEL, IN_PERM), BF16),
        compiler_params=pltpu.CompilerParams(vmem_limit_bytes=VMEM_LIMIT),
        name="winprep",
    )(w_t)


def _prep_weights(g_final, g_mix, w_in, g_q, w_uq, g_kv, w_uk, w_uv, w_o_mla, mu_shift, w0, w2, a0, a2, g2,
                  k_k, k_a, r_k, ln_w, ln_b, w_o_rwkv, w_out, g_ffn, w_up, w_down):
    row = lambda a: a.reshape(1, -1).astype(F32)
    w_in_p = _winprep(jnp.swapaxes(w_in, 0, 1))

    uq = w_uq.reshape(Q_LORA, HEADS, NOPE + ROPE)
    zq = jnp.zeros((Q_LORA, HEADS, HP - NOPE - ROPE), F32)
    w_uq_p = jnp.concatenate([uq, zq], axis=-1).reshape(Q_LORA, HEADS * HP).astype(BF16)

    uk = w_uk.reshape(KV_LORA, HEADS, NOPE)
    uk_pad = jnp.concatenate([uk, jnp.zeros((KV_LORA, HEADS, HP - NOPE), F32)], axis=-1).reshape(KV_LORA, HEADS * HP)
    w_ukv = uk_pad.astype(BF16)

    head = np.arange(RW_W) // HS
    seg = jnp.asarray(head[:, None] == head[None, :], BF16)
    return dict(
        g_mix=row(g_mix), w_in=w_in_p, g_q=row(g_q), w_uq=w_uq_p, g_kv=row(g_kv), w_ukv=w_ukv,
        w_uk=w_uk.astype(BF16), w_uv=w_uv.astype(BF16), w_uvt=w_uv.T.astype(BF16), w_o_mla=w_o_mla.astype(BF16),
        mu_shift=row(mu_shift), w0=row(w0), w2=w2.astype(BF16), a0=row(a0), a2=a2.astype(BF16),
        g2=g2.astype(BF16), k_k=row(k_k), k_a=row(k_a), r_k=row(r_k), ln_w=row(ln_w), ln_b=row(ln_b), seg=seg,
        w_o_rwkv=w_o_rwkv.astype(BF16), w_out=w_out.astype(BF16), g_ffn=row(g_ffn),
        w_up=w_up.astype(BF16), w_down=w_down.astype(BF16), g_final=row(g_final))


def _rope_tables(pos):
    inv = ROPE_BASE ** (-np.arange(0, ROPE, 2, dtype=np.float64) / ROPE)
    ang = np.asarray(pos, np.float64)[:, None] * inv[None, :]
    cos, sin = np.cos(ang), np.sin(ang)
    n = ang.shape[0]
    cc = np.concatenate([cos, cos], axis=1)
    ss = np.concatenate([-sin, sin], axis=1)
    z32 = np.zeros((n, ROPE))
    ctq = np.concatenate([np.ones((n, NOPE)), cc, z32], axis=1)
    stq = np.concatenate([np.zeros((n, NOPE)), ss, z32], axis=1)
    ctk = np.concatenate([cc, z32, cc, z32], axis=1)
    stk = np.concatenate([ss, z32, ss, z32], axis=1)
    return tuple(t.astype(np.float32) for t in (ctq, stq, ctk, stk))


def _row_tile(rows, cap):
    tm = min(rows, cap)
    assert rows % tm == 0
    return tm


def kernel(x_prompt, x_sample, cache_kv, page_table, state_wkv, state_shift, meta_tokens, g_final, g_mix, w_in, g_q, w_uq, g_kv, w_uk, w_uv, w_o_mla, mu_shift, w0, w2, a0, a2, g2, k_k, k_a, r_k, ln_w, ln_b, w_o_rwkv, w_out, g_ffn, w_up, w_down):
    depth = g_mix.shape[0]
    assert depth == 1
    bp, seq, _ = x_prompt.shape
    bd, s_s, _ = x_sample.shape
    assert s_s == 1
    n_pages = page_table.shape[1]
    past_len = n_pages * PAGE
    wts = _prep_weights(g_final, g_mix[0], w_in[0], g_q[0], w_uq[0], g_kv[0], w_uk[0], w_uv[0], w_o_mla[0],
                        mu_shift[0], w0[0], w2[0], a0[0], a2[0], g2[0], k_k[0], k_a[0], r_k[0], ln_w[0], ln_b[0],
                        w_o_rwkv[0], w_out[0], g_ffn[0], w_up[0], w_down[0])

    tab_m = _rope_tables(np.arange(N_META))
    _, rw_m, _, kvrow_m, kpad_m, vt_m = _proj(meta_tokens.astype(F32), tab_m, wts, N_META, BF16, False)
    rw_m_pad = jnp.concatenate([rw_m, jnp.zeros((HS - N_META, RW_COLS), F32)], axis=0)
    zero_shift = jnp.zeros((1, 1, RW_COLS), F32)
    zero_state = jnp.zeros((1, HEADS, HS, HS), F32)
    _, s_meta = _rwkv(rw_m_pad, zero_shift, zero_state, wts, 1, N_META)
    shift_meta = rw_m[N_META - 1:N_META].reshape(1, 1, RW_COLS)

    rows_p = bp * seq
    tm = _row_tile(seq, ROW_TILE)
    tq = _row_tile(seq, KV_TILE)
    xp = x_prompt.reshape(rows_p, D_MODEL)
    ctq, stq, ctk, stk = _rope_tables(N_META + np.arange(seq))
    tab_p = (ctq * np.float32(Q_PRESCALE), stq * np.float32(Q_PRESCALE), ctk, stk)
    gsig, rw, q, kvrow, kpad, vt = _proj(xp, tab_p, wts, tm, BF16, True)
    o_mla = _attn(q, kpad, vt, kpad_m, vt_m, bp, seq, tq)
    o_rwkv, s_p = _rwkv(rw, shift_meta, s_meta, wts, bp, None)
    y_prompt = _mlp(xp, o_mla, o_rwkv, gsig, wts, tm).reshape(bp, seq, D_MODEL)
    kv_prompt = jnp.concatenate([jnp.broadcast_to(kvrow_m[None], (bp, N_META, KV_W)),
                                 kvrow.reshape(bp, seq, KV_W)], axis=1)[None]
    shift_prompt = rw.reshape(bp, seq, RW_COLS)[:, -1][None]

    xs = x_sample.reshape(bd, D_MODEL)
    tab_s = _rope_tables(np.full((bd,), past_len))
    gsig_s, rw_s, q_s, kvrow_s, _, _ = _proj(xs, tab_s, wts, bd, F32, False)
    qf = jnp.transpose(_qlat(q_s, wts['w_uk']), (1, 0, 2))
    cache = jnp.swapaxes(cache_kv.reshape(cache_kv.shape[1], PAGE, KV_W), 1, 2)
    o_mla_s = _sattn(page_table, qf, kvrow_s.reshape(bd, 1, KV_W), wts['w_uv'], cache).reshape(bd, HEADS * VDIM)
    s_t = jnp.transpose(state_wkv[0], (1, 2, 3, 0))
    o_rwkv_s, s_t_new = _rwkv_step(rw_s, state_shift[0], s_t, wts)
    s_s_new = jnp.transpose(s_t_new, (3, 0, 1, 2))
    y_sample = _mlp(xs, o_mla_s, o_rwkv_s, gsig_s, wts, bd).reshape(bd, 1, D_MODEL)

    return (y_prompt, y_sample,
            kv_prompt.astype(cache_kv.dtype), s_p[None].astype(state_wkv.dtype),
            shift_prompt.astype(state_shift.dtype),
            kvrow_s.reshape(1, bd, 1, KV_W).astype(cache_kv.dtype), s_s_new[None].astype(state_wkv.dtype),
            rw_s[None].astype(state_shift.dtype))
```

```python
import functools

import numpy as np
import jax
import jax.numpy as jnp
from jax import lax
from jax.experimental import pallas as pl
from jax.experimental.pallas import tpu as pltpu

F32 = jnp.float32
BF16 = jnp.bfloat16

D_MODEL = 1024
N_META = 16
PAGE = 128
HEADS = 8
NOPE = 64
ROPE = 32
VDIM = 64
Q_LORA = 384
KV_LORA = 256
KV_W = KV_LORA + ROPE
ROPE_BASE = 10000.0
SM_SCALE = (NOPE + ROPE) ** -0.5
HS = 64
RW_W = HEADS * HS
W_LORA = 64
A_LORA = 64
G_LORA = 128
RW_COLS = 3 * RW_W + W_LORA + A_LORA + G_LORA
GN_EPS = 64e-5
D_FF = 4 * D_MODEL
NORM_EPS = 1e-6
NEG_INF = -1e30
DECAY_SCALE = 0.6065306597126334
Q_PRESCALE = SM_SCALE * 1.4426950408889634

LANE = 128
HP = 128
VMEM_LIMIT = 56 * 1024 * 1024
KV_TILE = 256
ROW_TILE = 512

C_GATE = 0
C_RW = 2 * D_MODEL
C_QIN = C_RW + RW_COLS
C_KVIN = C_QIN + Q_LORA
C_KR = C_KVIN + KV_LORA
IN_PERM = C_KR + LANE


def _dot(a, b):
    return jnp.dot(a, b, preferred_element_type=F32)


def _dot_nt(a, b):
    return lax.dot_general(a, b, (((1,), (1,)), ((), ())), preferred_element_type=F32)


def _dot_tn(a, b):
    return lax.dot_general(a, b, (((0,), (0,)), ((), ())), preferred_element_type=F32)


def _split2(x):
    hi = x.astype(BF16)
    lo = (x - hi.astype(F32)).astype(BF16)
    return hi, lo


def _split3(x):
    x1 = x.astype(BF16)
    r1 = x - x1.astype(F32)
    x2 = r1.astype(BF16)
    x3 = (r1 - x2.astype(F32)).astype(BF16)
    return x1, x2, x3


def _rms(x, g):
    return x * lax.rsqrt(jnp.mean(x * x, axis=-1, keepdims=True) + NORM_EPS) * g


def _params(sem):
    return pltpu.CompilerParams(dimension_semantics=sem, vmem_limit_bytes=VMEM_LIMIT)


def _proj_kernel(x_ref, gmix_ref, win_ref, gq_ref, wuq_ref, gkv_ref, wukv_ref, wuvt_ref,
                 ctq_ref, stq_ref, ctk_ref, stk_ref,
                 gsig_ref, rw_ref, q_ref, kvrow_ref, kpad_ref, vt_ref, *, q_transposed):
    h = _rms(x_ref[...], gmix_ref[...]).astype(BF16)
    tw = vt_ref.shape[2]
    lane = lax.broadcasted_iota(jnp.int32, (x_ref.shape[0], LANE), 1)
    first_half = (lane % ROPE) < ROPE // 2

    def rope(t, cos_t, sin_t):
        swapped = jnp.where(first_half, pltpu.roll(t, LANE - ROPE // 2, 1), pltpu.roll(t, ROPE // 2, 1))
        return t * cos_t + swapped * sin_t

    zq = _dot(h, win_ref[:, C_QIN:C_QIN + Q_LORA])
    zkv = _dot(h, win_ref[:, C_KVIN:C_KVIN + KV_LORA])
    zkr = _dot(h, win_ref[:, C_KR:C_KR + LANE])
    step = 512
    for c in range(0, 2 * D_MODEL, step):
        gsig_ref[:, c:c + step] = jax.nn.sigmoid(_dot(h, win_ref[:, C_GATE + c:C_GATE + c + step])).astype(BF16)

    qn = _rms(zq, gq_ref[...]).astype(BF16)
    ckv = _rms(zkv, gkv_ref[...])
    cb = ckv.astype(BF16)
    qa = _dot(qn, wuq_ref[...])
    kn = _dot(cb, wukv_ref[...])
    for sub in range(vt_ref.shape[0]):
        vt_ref[sub] = _dot_nt(wuvt_ref[...], cb[sub * tw:(sub + 1) * tw]).astype(BF16)

    for c in range(0, RW_COLS, 256):
        rw_ref[:, c:c + 256] = _dot(h, win_ref[:, C_RW + c:C_RW + c + 256])

    ctq = ctq_ref[...]
    stq = stq_ref[...]
    for hh in range(HEADS):
        hs = slice(hh * HP, (hh + 1) * HP)
        qh = rope(qa[:, hs], ctq, stq)
        if q_transposed:
            for sub in range(q_ref.shape[0]):
                q_ref[sub, hs, :] = qh[sub * tw:(sub + 1) * tw].T.astype(q_ref.dtype)
        else:
            q_ref[:, hs] = qh.astype(q_ref.dtype)
    kvrow_ref[:, 0:KV_LORA] = ckv
    kr = rope(zkr, ctk_ref[...], stk_ref[...])
    kvrow_ref[:, KV_LORA:KV_W] = kr[:, 0:ROPE]
    kr_head = jnp.where((lane >= NOPE) & (lane < NOPE + ROPE), kr, 0.0)
    for hh in range(HEADS):
        hs = slice(hh * HP, (hh + 1) * HP)
        kpad_ref[:, hs] = (kn[:, hs] + kr_head).astype(BF16)


def _proj(x, tables, wts, tm, q_dtype, q_transposed):
    rows = x.shape[0]
    nt = tables[0].shape[0] // tm
    tw = min(tm, KV_TILE)
    row = lambda i: (i, 0)
    const = lambda i: (0, 0)
    tab = lambda i: (i % nt, 0)
    full = lambda a: pl.BlockSpec(a.shape, const, pipeline_mode=pl.Buffered(1))
    in_specs = [pl.BlockSpec((tm, D_MODEL), row), full(wts['g_mix']), full(wts['w_in']), full(wts['g_q']),
                full(wts['w_uq']), full(wts['g_kv']), full(wts['w_ukv']), full(wts['w_uvt'])]
    in_specs += [pl.BlockSpec((tm, LANE), tab)] * 4
    widths = [(2 * D_MODEL, BF16), (RW_COLS, F32), (HEADS * HP, q_dtype), (KV_W, F32), (HEADS * HP, BF16)]
    nt_rows = rows // tm
    tiled = lambda w: pl.BlockSpec((tm // tw, w, tw), lambda i: (i, 0, 0))
    out_specs = [pl.BlockSpec((tm, w), row) for w, _ in widths] + [tiled(HEADS * VDIM)]
    out_shape = [jax.ShapeDtypeStruct((rows, w), dt) for w, dt in widths]
    out_shape += [jax.ShapeDtypeStruct((rows // tw, HEADS * VDIM, tw), BF16)]
    if q_transposed:
        out_specs[2] = tiled(HEADS * HP)
        out_shape[2] = jax.ShapeDtypeStruct((rows // tw, HEADS * HP, tw), q_dtype)
    return pl.pallas_call(
        functools.partial(_proj_kernel, q_transposed=q_transposed),
        grid=(nt_rows,),
        in_specs=in_specs,
        out_specs=out_specs,
        out_shape=out_shape,
        compiler_params=_params(("parallel",)),
        name="proj",
    )(x, wts['g_mix'], wts['w_in'], wts['g_q'], wts['w_uq'], wts['g_kv'], wts['w_ukv'], wts['w_uvt'], *tables)


def _attn_kernel(q_ref, k_ref, vt_ref, km_ref, vmt_ref, eye_ref, o_ref, *, tq):
    i = pl.program_id(1)
    qs = [q_ref[0, hh * HP:(hh + 1) * HP, :] for hh in range(HEADS)]

    def update(carry, groups):
        ss = [[_dot(k_tiles[hh], qs[hh]) for hh in range(HEADS)] for k_tiles, _, _ in groups]
        ps, stats = [], []
        for hh in range(HEADS):
            m, l, _ = carry[hh]
            sm = [s_g[hh] if g[2] is None else jnp.where(g[2], s_g[hh], NEG_INF) for s_g, g in zip(ss, groups)]
            m_new = m
            for s in sm:
                m_new = jnp.maximum(m_new, jnp.max(s, axis=0, keepdims=True))
            alpha = jnp.exp2(m - m_new)
            p = [jnp.exp2(s - m_new) for s in sm]
            l = l * alpha
            for pg in p:
                l = l + jnp.sum(pg, axis=0, keepdims=True)
            stats.append((m_new, l, alpha))
            ps.append([pg.astype(BF16) for pg in p])
        pv = [[_dot(g[1][hh], ps[hh][gi]) for hh in range(HEADS)] for gi, g in enumerate(groups)]
        out = []
        for hh in range(HEADS):
            acc = carry[hh][2] * stats[hh][2]
            for gi in range(len(groups)):
                acc = acc + pv[gi][hh]
            out.append((stats[hh][0], stats[hh][1], acc))
        return tuple(out)

    def kv_tile(j, mask):
        rows = pl.ds(pl.multiple_of(j * tq, tq), tq)
        return ([k_ref[rows, hh * HP:(hh + 1) * HP] for hh in range(HEADS)],
                [vt_ref[j, hh * VDIM:(hh + 1) * VDIM, :] for hh in range(HEADS)], mask)

    init = (jnp.full((1, tq), NEG_INF, F32), jnp.zeros((1, tq), F32), jnp.zeros((VDIM, tq), F32))
    carry = lax.fori_loop(0, i, lambda j, c: update(c, [kv_tile(j, None)]), (init,) * HEADS)
    key = lax.broadcasted_iota(jnp.int32, (tq, tq), 0)
    qry = lax.broadcasted_iota(jnp.int32, (tq, tq), 1)
    meta = ([km_ref[:, hh * HP:(hh + 1) * HP] for hh in range(HEADS)],
            [vmt_ref[0, hh * VDIM:(hh + 1) * VDIM, :] for hh in range(HEADS)], None)
    carry = update(carry, [meta, kv_tile(i, key <= qry)])

    outs = []
    for hh in range(HEADS):
        m, l, acc = carry[hh]
        outs.append(_dot_tn((acc / l).astype(BF16), eye_ref[...]))
    o_ref[...] = jnp.concatenate(outs, axis=-1).astype(o_ref.dtype)


def _attn(q, kpad, vt, km, vmt, batch, seq, tq):
    nq = seq // tq
    eye = jnp.asarray(np.eye(VDIM), BF16)
    kern = functools.partial(_attn_kernel, tq=tq)
    return pl.pallas_call(
        kern,
        grid=(batch, nq),
        in_specs=[
            pl.BlockSpec((1, HEADS * HP, tq), lambda b, i: (b * nq + i, 0, 0)),
            pl.BlockSpec((seq, HEADS * HP), lambda b, i: (b, 0)),
            pl.BlockSpec((nq, HEADS * VDIM, tq), lambda b, i: (b, 0, 0)),
            pl.BlockSpec(km.shape, lambda b, i: (0, 0)),
            pl.BlockSpec(vmt.shape, lambda b, i: (0, 0, 0)),
            pl.BlockSpec(eye.shape, lambda b, i: (0, 0)),
        ],
        out_specs=pl.BlockSpec((tq, HEADS * VDIM), lambda b, i: (b * nq + i, 0)),
        out_shape=jax.ShapeDtypeStruct((batch * seq, HEADS * VDIM), BF16),
        compiler_params=_params(("parallel", "arbitrary")),
        name="attn",
    )(q, kpad, vt, km, vmt, eye)


def _qlat_kernel(q_ref, wuk_ref, o_ref):
    for hh in range(HEADS):
        qn = q_ref[:, hh * HP:hh * HP + NOPE].astype(BF16)
        o_ref[hh, :, 0:KV_LORA] = _dot_nt(qn, wuk_ref[:, hh * NOPE:(hh + 1) * NOPE]) * SM_SCALE
        o_ref[hh, :, KV_LORA:KV_W] = q_ref[:, hh * HP + NOPE:hh * HP + NOPE + ROPE] * SM_SCALE


def _qlat(q, wuk):
    rows = q.shape[0]
    return pl.pallas_call(
        _qlat_kernel,
        out_shape=jax.ShapeDtypeStruct((HEADS, rows, KV_W), F32),
        name="qlat",
    )(q, wuk)


SATTN_SLOTS = 3
SATTN_CHUNK = 1024


def _sattn_kernel(pt_ref, qf_ref, kvn_ref, wuv_ref, cache_ref, o_ref, buf, sem, *, n_pages, chunk):
    b = pl.program_id(0)
    nb = pl.num_programs(0)
    ahead = SATTN_SLOTS - 1

    def page_copy(page, slot, j):
        return pltpu.make_async_copy(cache_ref.at[page], buf.at[slot, :, pl.ds(j * PAGE, PAGE)], sem.at[slot])

    def start_pages(bb, slot, lo, hi):
        for j in range(lo, hi):
            page_copy(pt_ref[bb, j], slot, j).start(priority=j % 2)

    def wait_slot(slot):
        for j in range(n_pages):
            page_copy(0, slot, j).wait()

    @pl.when(b == 0)
    def _():
        for first in range(ahead):
            start_pages(jnp.minimum(first, nb - 1), first, 0, n_pages)

    slot = b % SATTN_SLOTS
    wait_slot(slot)
    nxt = jnp.minimum(b + ahead, nb - 1)
    nslot = (b + ahead) % SATTN_SLOTS
    n_chunks = (n_pages * PAGE) // chunk
    per_chunk = n_pages // n_chunks

    qf = qf_ref[b].astype(BF16)
    kvn = kvn_ref[b].astype(BF16).astype(F32)
    s_own = jnp.sum(qf.astype(F32) * kvn, axis=-1, keepdims=True)
    kts, ss = [], []
    for ci in range(n_chunks):
        kt = buf[slot, :, ci * chunk:(ci + 1) * chunk].astype(BF16)
        kts.append(kt)
        ss.append(_dot(qf, kt))
        start_pages(nxt, nslot, ci * per_chunk, (ci + 1) * per_chunk)
    m = s_own
    for sc in ss:
        m = jnp.maximum(m, jnp.max(sc, axis=-1, keepdims=True))
    ps = [jnp.exp(sc - m) for sc in ss]
    p_own = jnp.exp(s_own - m)
    l = p_own
    for pc in ps:
        l = l + jnp.sum(pc, axis=-1, keepdims=True)
    acc = p_own.astype(BF16).astype(F32) * kvn[:, 0:KV_LORA]
    for pc, kt in zip(ps, kts):
        acc = acc + _dot_nt(pc.astype(BF16), kt[0:KV_LORA, :])

    @pl.when(b == nb - 1)
    def _():
        for extra in range(1, SATTN_SLOTS):
            wait_slot((b + extra) % SATTN_SLOTS)

    o_lat = (acc / l).astype(BF16)
    full = _dot(o_lat, wuv_ref[...])
    r = lax.broadcasted_iota(jnp.int32, full.shape, 0)
    cidx = lax.broadcasted_iota(jnp.int32, full.shape, 1)
    o_ref[b] = jnp.sum(jnp.where(cidx // VDIM == r, full, 0.0), axis=0, keepdims=True).astype(o_ref.dtype)


def _sattn(page_table, qf, kvn, wuv, cache):
    nb, n_pages = page_table.shape
    assert n_pages % 2 == 0
    chunk = min(SATTN_CHUNK, n_pages * PAGE)
    assert (n_pages * PAGE) % chunk == 0
    kern = functools.partial(_sattn_kernel, n_pages=n_pages, chunk=chunk)
    grid_spec = pltpu.PrefetchScalarGridSpec(
        num_scalar_prefetch=1,
        grid=(nb,),
        in_specs=[
            pl.BlockSpec((nb, HEADS, KV_W), lambda b, pt: (0, 0, 0)),
            pl.BlockSpec((nb, 1, KV_W), lambda b, pt: (0, 0, 0)),
            pl.BlockSpec(wuv.shape, lambda b, pt: (0, 0)),
            pl.BlockSpec(memory_space=pl.ANY),
        ],
        out_specs=pl.BlockSpec((nb, 1, HEADS * VDIM), lambda b, pt: (0, 0, 0)),
        scratch_shapes=[pltpu.VMEM((SATTN_SLOTS, KV_W, n_pages * PAGE), F32),
                        pltpu.SemaphoreType.DMA((SATTN_SLOTS,))],
    )
    return pl.pallas_call(
        kern,
        grid_spec=grid_spec,
        out_shape=jax.ShapeDtypeStruct((nb, 1, HEADS * VDIM), BF16),
        compiler_params=_params(("arbitrary",)),
        name="sattn",
    )(page_table, qf, kvn, wuv, cache)


GROUP = 4
RWKV_BATCHES_PER_STEP = 8
GW = GROUP * HS
RW_PARAM_NAMES = ['mu_shift', 'w0', 'w2', 'a0', 'a2', 'g2', 'k_k', 'k_a', 'r_k', 'ln_w', 'ln_b', 'seg']


def _seg_sum(x, seg):
    return _dot(x.astype(BF16), seg)


def _rwkv_token_prep(cur, prev, p):
    z = cur + (prev - cur) * p['mu_shift']
    r = z[:, 0:RW_W]
    k = z[:, RW_W:2 * RW_W]
    v = z[:, 2 * RW_W:3 * RW_W]
    w_in = z[:, 3 * RW_W:3 * RW_W + W_LORA]
    a_in = z[:, 3 * RW_W + W_LORA:3 * RW_W + W_LORA + A_LORA]
    g_in = z[:, 3 * RW_W + W_LORA + A_LORA:RW_COLS]
    w_log = p['w0'] + _dot(jnp.tanh(w_in).astype(BF16), p['w2'])
    logw = -DECAY_SCALE * jax.nn.sigmoid(w_log)
    a_sig = jax.nn.sigmoid(p['a0'] + _dot(a_in.astype(BF16), p['a2']))
    g = _dot(jax.nn.sigmoid(g_in).astype(BF16), p['g2'])
    kk = k * p['k_k']
    kk = kk * lax.rsqrt(jnp.maximum(_seg_sum(kk * kk, p['seg']), 1e-24))
    k_mod = k * (1.0 + (a_sig - 1.0) * p['k_a'])
    bonus = _seg_sum(r * k_mod * p['r_k'], p['seg']) * v
    return r, k_mod, v, logw, a_sig, g, kk, bonus


def _group_norm_out(y, bonus, g, p):
    mean = _seg_sum(y, p['seg']) * (1.0 / HS)
    d = y - mean
    var = _seg_sum(d * d, p['seg']) * (1.0 / HS)
    yn = d * lax.rsqrt(var + GN_EPS) * p['ln_w'] + p['ln_b']
    return (yn + bonus) * g


def _rwkv_kernel(rw_ref, shift_ref, m0_ref, *rest, n_valid, nbat):
    prm_refs = rest[:len(RW_PARAM_NAMES)]
    tri_ref, ones_ref, o_ref, mout_ref, prev_scr, m_scr = rest[len(RW_PARAM_NAMES):]
    p = {n: ref[...] for n, ref in zip(RW_PARAM_NAMES, prm_refs)}
    chunk = HS
    ngrp = HEADS // GROUP
    c = pl.program_id(1)
    nc = pl.num_programs(1)

    @pl.when(c == 0)
    def _():
        for bi in range(nbat):
            prev_scr[bi] = shift_ref[0]
            m_scr[bi] = m0_ref[0]

    curs, prevs = [], []
    for bi in range(nbat):
        cur_b = rw_ref[bi]
        rowi = lax.broadcasted_iota(jnp.int32, cur_b.shape, 0)
        prevs.append(jnp.where(rowi == 0, prev_scr[bi], pltpu.roll(cur_b, 1, 0)))
        prev_scr[bi] = cur_b[chunk - 1:chunk, :]
        curs.append(cur_b)
    r, k_mod, v, logw, a_sig, g, kk, bonus = _rwkv_token_prep(
        jnp.concatenate(curs, axis=0), jnp.concatenate(prevs, axis=0), p)

    if n_valid is not None:
        step = lax.broadcasted_iota(jnp.int32, (nbat * chunk, RW_W), 0) % chunk
        live = (c * chunk + step) < n_valid
        logw = jnp.where(live, logw, 0.0)
        kk = jnp.where(live, kk, 0.0)
        k_mod = jnp.where(live, k_mod, 0.0)
        v = jnp.where(live, v, 0.0)

    lw3 = _split3(logw)
    seq_rows = [slice(bi * chunk, (bi + 1) * chunk) for bi in range(nbat)]
    gcum = jnp.concatenate([sum(_dot(tri_ref[...], t[rs]) for t in lw3) for rs in seq_rows], axis=0)
    g_end = jnp.concatenate([jnp.broadcast_to(gcum[rs][chunk - 1:chunk, :], (chunk, RW_W)) for rs in seq_rows], axis=0)
    r_t = r * jnp.exp(gcum)
    a_t = -kk * jnp.exp(gcum - logw)
    eng = jnp.exp(-gcum)
    b_t = kk * a_sig * eng
    k_t = k_mod * eng
    decay_end = jnp.exp(g_end)
    b_e = b_t * decay_end
    k_e = k_t * decay_end

    ti = lax.broadcasted_iota(jnp.int32, (chunk, GW), 0)
    lane = lax.broadcasted_iota(jnp.int32, (chunk, GW), 1)
    si = lane % HS
    lane_head = lane // HS
    strict = si < ti
    incl = si <= ti
    eye = (si == ti).astype(F32)

    def bdr(zf):
        zb = zf.astype(BF16)
        return jnp.concatenate([jnp.where(lane_head == hh, zb, jnp.zeros_like(zb)) for hh in range(GROUP)], axis=0)

    streams = [(bi, gi) for bi in range(nbat) for gi in range(ngrp)]
    blk = lambda t, s: t[s[0] * chunk:(s[0] + 1) * chunk, s[1] * GW:(s[1] + 1) * GW]
    cat0 = lambda xs: jnp.concatenate(xs, axis=0).astype(BF16)
    cat1 = lambda xs: jnp.concatenate(xs, axis=1).astype(BF16)

    ar = [cat0([blk(a_t, s), blk(r_t, s)]) for s in streams]
    ab = [_dot_nt(ar[i], bdr(blk(b_t, s))) for i, s in enumerate(streams)]
    ak = [_dot_nt(ar[i], bdr(blk(k_t, s))) for i, s in enumerate(streams)]
    n = [jnp.where(strict, t[0:chunk], 0.0) for t in ab]
    a_rb = [jnp.where(incl, t[chunk:2 * chunk], 0.0) for t in ab]
    a_ak = [jnp.where(strict, t[0:chunk], 0.0) for t in ak]
    a_rk = [jnp.where(incl, t[chunk:2 * chunk], 0.0) for t in ak]

    x = [eye + t for t in n]
    pw = [_dot(t.astype(BF16), bdr(t)) for t in n]
    for _ in range(4):
        z = [_dot(cat0([xi, pi]), bdr(pi)) for xi, pi in zip(x, pw)]
        x = [xi + zi[0:chunk] for xi, zi in zip(x, z)]
        pw = [zi[chunk:2 * chunk] for zi in z]
    z = [_dot(xi.astype(BF16), bdr(pi)) for xi, pi in zip(x, pw)]
    x = [xi + zi for xi, zi in zip(x, z)]

    m0 = [m_scr[s[0], s[1]] for s in streams]
    bm = [bdr(t) for t in m0]
    bv = [bdr(blk(v, s)) for s in streams]
    rhs = [_dot(cat1([blk(a_t, s), a_ak[i]]), jnp.concatenate([bm[i], bv[i]], axis=0)) for i, s in enumerate(streams)]
    u = [_dot(xi.astype(BF16), bdr(ri)) for xi, ri in zip(x, rhs)]
    y = [_dot(cat1([blk(r_t, s), a_rb[i], a_rk[i]]), jnp.concatenate([bm[i], bdr(u[i]), bv[i]], axis=0))
         for i, s in enumerate(streams)]
    full = [_dot_tn(cat0([blk(b_e, s), blk(k_e, s)]), cat0([u[i], blk(v, s)])) for i, s in enumerate(streams)]
    dg = jnp.concatenate([eye * blk(decay_end, s)[0:1] for s in streams], axis=0)
    gm_all = sum(_dot(t, ones_ref[...]) for t in _split3(dg))
    gm = [gm_all[i * chunk:(i + 1) * chunk] for i in range(len(streams))]
    for i, s in enumerate(streams):
        upd = jnp.where(lane_head == 0, full[i][0:HS], 0.0)
        for hh in range(1, GROUP):
            upd = upd + jnp.where(lane_head == hh, full[i][hh * HS:(hh + 1) * HS], 0.0)
        m_scr[s[0], s[1]] = gm[i] * m0[i] + upd

    y_all = jnp.concatenate([jnp.concatenate([y[bi * ngrp + gi] for gi in range(ngrp)], axis=1)
                             for bi in range(nbat)], axis=0)
    out = _group_norm_out(y_all, bonus, g, p).astype(o_ref.dtype)
    for bi in range(nbat):
        o_ref[bi] = out[seq_rows[bi]]

    @pl.when(c == nc - 1)
    def _():
        mout_ref[...] = m_scr[...]


def _state_to_m(s):
    b = s.shape[0]
    return jnp.transpose(s.reshape(b, HEADS // GROUP, GROUP, HS, HS), (0, 1, 4, 2, 3)).reshape(b, HEADS // GROUP, HS, GW)


def _m_to_state(m):
    b = m.shape[0]
    return jnp.transpose(m.reshape(b, HEADS // GROUP, HS, GROUP, HS), (0, 1, 3, 4, 2)).reshape(b, HEADS, HS, HS)


def _rwkv(rw, shift0, s0, wts, batch, n_valid):
    chunk = HS
    rows = rw.shape[0]
    seq = rows // batch
    nc = seq // chunk
    nbat = RWKV_BATCHES_PER_STEP if batch % RWKV_BATCHES_PER_STEP == 0 else 1
    const2 = lambda b, c: (0, 0)
    consts = [wts[n] for n in RW_PARAM_NAMES]
    tri = jnp.asarray(np.arange(chunk)[:, None] >= np.arange(chunk)[None, :], BF16)
    hd = np.arange(GW) // HS
    ones_bd = jnp.asarray(hd[:, None] == hd[None, :], BF16)
    m0 = _state_to_m(s0)
    ngrp = HEADS // GROUP
    kern = functools.partial(_rwkv_kernel, n_valid=n_valid, nbat=nbat)
    o, m = pl.pallas_call(
        kern,
        grid=(batch // nbat, nc),
        in_specs=[pl.BlockSpec((nbat, chunk, RW_COLS), lambda b, c: (b, c, 0)),
                  pl.BlockSpec((1, 1, RW_COLS), lambda b, c: (0, 0, 0)),
                  pl.BlockSpec((1, ngrp, HS, GW), lambda b, c: (0, 0, 0, 0))]
        + [pl.BlockSpec(a.shape, const2) for a in consts]
        + [pl.BlockSpec(tri.shape, const2), pl.BlockSpec(ones_bd.shape, const2)],
        out_specs=[pl.BlockSpec((nbat, chunk, RW_W), lambda b, c: (b, c, 0)),
                   pl.BlockSpec((nbat, ngrp, HS, GW), lambda b, c: (b, 0, 0, 0))],
        out_shape=[jax.ShapeDtypeStruct((batch, seq, RW_W), BF16),
                   jax.ShapeDtypeStruct((batch, ngrp, HS, GW), F32)],
        scratch_shapes=[pltpu.VMEM((nbat, 1, RW_COLS), F32), pltpu.VMEM((nbat, ngrp, HS, GW), F32)],
        compiler_params=_params(("parallel", "arbitrary")),
        name="rwkv",
    )(rw.reshape(batch, seq, RW_COLS), shift0, m0, *consts, tri, ones_bd)
    return o.reshape(rows, RW_W), _m_to_state(m)


def _rwkv_step_kernel(rw_ref, shift_ref, s_ref, *rest):
    prm_refs = rest[:len(RW_PARAM_NAMES)]
    o_ref, sout_ref, t_scr, y_scr, tok_scr = rest[len(RW_PARAM_NAMES):]
    h = pl.program_id(0)

    @pl.when(h == 0)
    def _():
        p = {n: ref[...] for n, ref in zip(RW_PARAM_NAMES, prm_refs)}
        r, k_mod, v, logw, a_sig, g, kk, bonus = _rwkv_token_prep(rw_ref[...], shift_ref[...], p)
        t_scr[0] = r.T
        t_scr[1] = jnp.exp(logw).T
        t_scr[2] = k_mod.T
        t_scr[3] = v.T
        t_scr[4] = (-kk).T
        t_scr[5] = (kk * a_sig).T
        tok_scr[0] = bonus
        tok_scr[1] = g

    rows = pl.ds(pl.multiple_of(h * HS, HS), HS)
    r_h, w_h, k_h = t_scr[0, rows, :], t_scr[1, rows, :], t_scr[2, rows, :]
    a_h, b_h = t_scr[4, rows, :], t_scr[5, rows, :]

    def body(vv, carry):
        sv = s_ref[0, vv]
        vrow = t_scr[3, pl.ds(h * HS + vv, 1), :]
        sa = jnp.sum(sv * a_h, axis=0, keepdims=True)
        sn = sv * w_h + sa * b_h + vrow * k_h
        sout_ref[0, vv] = sn
        y_scr[pl.ds(h * HS + vv, 1), :] = jnp.sum(sn * r_h, axis=0, keepdims=True)
        return carry
    lax.fori_loop(0, HS, body, 0, unroll=4)

    @pl.when(h == HEADS - 1)
    def _():
        p = {n: ref[...] for n, ref in zip(RW_PARAM_NAMES, prm_refs)}
        o_ref[...] = _group_norm_out(y_scr[...].T, tok_scr[0], tok_scr[1], p).astype(o_ref.dtype)


def _rwkv_step(rw, shift, s_t, wts):
    nb = rw.shape[0]
    consts = [wts[n] for n in RW_PARAM_NAMES]
    const2 = lambda h: (0, 0)
    return pl.pallas_call(
        _rwkv_step_kernel,
        grid=(HEADS,),
        in_specs=[pl.BlockSpec((nb, RW_COLS), const2), pl.BlockSpec((nb, RW_COLS), const2),
                  pl.BlockSpec((1, HS, HS, nb), lambda h: (h, 0, 0, 0))]
        + [pl.BlockSpec(a.shape, const2) for a in consts],
        out_specs=[pl.BlockSpec((nb, RW_W), const2), pl.BlockSpec((1, HS, HS, nb), lambda h: (h, 0, 0, 0))],
        out_shape=[jax.ShapeDtypeStruct((nb, RW_W), BF16), jax.ShapeDtypeStruct(s_t.shape, F32)],
        scratch_shapes=[pltpu.VMEM((6, RW_W, nb), F32), pltpu.VMEM((RW_W, nb), F32), pltpu.VMEM((2, nb, RW_W), F32)],
        compiler_params=_params(("arbitrary",)),
        name="rwkv_step",
    )(rw, shift, s_t, *consts)


def _mlp_kernel(x_ref, om_ref, or_ref, gs_ref, wom_ref, wor_ref, wout_ref, gffn_ref, wup_ref, wdn_ref, gfin_ref,
                y_ref):
    a = _dot(om_ref[...], wom_ref[...])
    r = _dot(or_ref[...], wor_ref[...])
    mix = gs_ref[:, 0:D_MODEL] * a + gs_ref[:, D_MODEL:2 * D_MODEL] * r
    x1 = x_ref[...] + _dot(mix.astype(BF16), wout_ref[...])
    h2 = _rms(x1, gffn_ref[...]).astype(BF16)
    u = jnp.maximum(_dot(h2, wup_ref[...]), 0.0)
    x2 = x1 + _dot((u * u).astype(BF16), wdn_ref[...])
    y_ref[...] = _rms(x2, gfin_ref[...])


def _mlp(x, om, orw, gsig, wts, tm):
    rows = x.shape[0]
    row = lambda i: (i, 0)
    const = lambda i: (0, 0)
    full = lambda a: pl.BlockSpec(a.shape, const, pipeline_mode=pl.Buffered(1))
    names = ['w_o_mla', 'w_o_rwkv', 'w_out', 'g_ffn', 'w_up', 'w_down', 'g_final']
    return pl.pallas_call(
        _mlp_kernel,
        grid=(rows // tm,),
        in_specs=[pl.BlockSpec((tm, D_MODEL), row), pl.BlockSpec((tm, HEADS * VDIM), row),
                  pl.BlockSpec((tm, RW_W), row), pl.BlockSpec((tm, 2 * D_MODEL), row)]
        + [full(wts[n]) for n in names],
        out_specs=pl.BlockSpec((tm, D_MODEL), row),
        out_shape=jax.ShapeDtypeStruct((rows, D_MODEL), F32),
        compiler_params=_params(("parallel",)),
        name="mlp",
    )(x, om, orw, gsig, *[wts[n] for n in names])


SRC_QIN = 0
SRC_KVIN = Q_LORA
SRC_KR = Q_LORA + KV_LORA
SRC_RW = SRC_KR + ROPE
SRC_GATE = SRC_RW + RW_COLS


def _winprep_kernel(wt_ref, o_ref):
    def put(dst, src, n):
        for c in range(0, n, 256):
            w = min(256, n - c)
            o_ref[:, dst + c:dst + c + w] = wt_ref[src + c:src + c + w, :].T.astype(BF16)

    put(C_GATE, SRC_GATE, 2 * D_MODEL)
    put(C_RW, SRC_RW, RW_COLS)
    put(C_QIN, SRC_QIN, Q_LORA)
    put(C_KVIN, SRC_KVIN, KV_LORA)
    t = wt_ref[SRC_KR:SRC_KR + LANE, :].T
    lane = lax.broadcasted_iota(jnp.int32, t.shape, 1)
    kr0 = jnp.where(lane < ROPE, t, 0.0)
    o_ref[:, C_KR:C_KR + LANE] = (kr0 + pltpu.roll(kr0, NOPE, 1)).astype(BF16)


def _winprep(w_t):
    return pl.pallas_call(
        _winprep_kernel,
        out_shape=jax.ShapeDtypeStruct((D_MODEL, IN_PERM), BF16),
        compiler_params=pltpu.CompilerParams(vmem_limit_bytes=VMEM_LIMIT),
        name="winprep",
    )(w_t)


def _prep_weights(g_final, g_mix, w_in, g_q, w_uq, g_kv, w_uk, w_uv, w_o_mla, mu_shift, w0, w2, a0, a2, g2,
                  k_k, k_a, r_k, ln_w, ln_b, w_o_rwkv, w_out, g_ffn, w_up, w_down):
    row = lambda a: a.reshape(1, -1).astype(F32)
    w_in_p = _winprep(jnp.swapaxes(w_in, 0, 1))

    uq = w_uq.reshape(Q_LORA, HEADS, NOPE + ROPE)
    zq = jnp.zeros((Q_LORA, HEADS, HP - NOPE - ROPE), F32)
    w_uq_p = jnp.concatenate([uq, zq], axis=-1).reshape(Q_LORA, HEADS * HP).astype(BF16)

    uk = w_uk.reshape(KV_LORA, HEADS, NOPE)
    uk_pad = jnp.concatenate([uk, jnp.zeros((KV_LORA, HEADS, HP - NOPE), F32)], axis=-1).reshape(KV_LORA, HEADS * HP)
    w_ukv = uk_pad.astype(BF16)

    head = np.arange(RW_W) // HS
    seg = jnp.asarray(head[:, None] == head[None, :], BF16)
    return dict(
        g_mix=row(g_mix), w_in=w_in_p, g_q=row(g_q), w_uq=w_uq_p, g_kv=row(g_kv), w_ukv=w_ukv,
        w_uk=w_uk.astype(BF16), w_uv=w_uv.astype(BF16), w_uvt=w_uv.T.astype(BF16), w_o_mla=w_o_mla.astype(BF16),
        mu_shift=row(mu_shift), w0=row(w0), w2=w2.astype(BF16), a0=row(a0), a2=a2.astype(BF16),
        g2=g2.astype(BF16), k_k=row(k_k), k_a=row(k_a), r_k=row(r_k), ln_w=row(ln_w), ln_b=row(ln_b), seg=seg,
        w_o_rwkv=w_o_rwkv.astype(BF16), w_out=w_out.astype(BF16), g_ffn=row(g_ffn),
        w_up=w_up.astype(BF16), w_down=w_down.astype(BF16), g_final=row(g_final))


def _rope_tables(pos):
    inv = ROPE_BASE ** (-np.arange(0, ROPE, 2, dtype=np.float64) / ROPE)
    ang = np.asarray(pos, np.float64)[:, None] * inv[None, :]
    cos, sin = np.cos(ang), np.sin(ang)
    n = ang.shape[0]
    cc = np.concatenate([cos, cos], axis=1)
    ss = np.concatenate([-sin, sin], axis=1)
    z32 = np.zeros((n, ROPE))
    ctq = np.concatenate([np.ones((n, NOPE)), cc, z32], axis=1)
    stq = np.concatenate([np.zeros((n, NOPE)), ss, z32], axis=1)
    ctk = np.concatenate([cc, z32, cc, z32], axis=1)
    stk = np.concatenate([ss, z32, ss, z32], axis=1)
    return tuple(t.astype(np.float32) for t in (ctq, stq, ctk, stk))


def _row_tile(rows, cap):
    tm = min(rows, cap)
    assert rows % tm == 0
    return tm


def kernel(x_prompt, x_sample, cache_kv, page_table, state_wkv, state_shift, meta_tokens, g_final, g_mix, w_in, g_q, w_uq, g_kv, w_uk, w_uv, w_o_mla, mu_shift, w0, w2, a0, a2, g2, k_k, k_a, r_k, ln_w, ln_b, w_o_rwkv, w_out, g_ffn, w_up, w_down):
    depth = g_mix.shape[0]
    assert depth == 1
    bp, seq, _ = x_prompt.shape
    bd, s_s, _ = x_sample.shape
    assert s_s == 1
    n_pages = page_table.shape[1]
    past_len = n_pages * PAGE
    wts = _prep_weights(g_final, g_mix[0], w_in[0], g_q[0], w_uq[0], g_kv[0], w_uk[0], w_uv[0], w_o_mla[0],
                        mu_shift[0], w0[0], w2[0], a0[0], a2[0], g2[0], k_k[0], k_a[0], r_k[0], ln_w[0], ln_b[0],
                        w_o_rwkv[0], w_out[0], g_ffn[0], w_up[0], w_down[0])

    tab_m = _rope_tables(np.arange(N_META))
    _, rw_m, _, kvrow_m, kpad_m, vt_m = _proj(meta_tokens.astype(F32), tab_m, wts, N_META, BF16, False)
    rw_m_pad = jnp.concatenate([rw_m, jnp.zeros((HS - N_META, RW_COLS), F32)], axis=0)
    zero_shift = jnp.zeros((1, 1, RW_COLS), F32)
    zero_state = jnp.zeros((1, HEADS, HS, HS), F32)
    _, s_meta = _rwkv(rw_m_pad, zero_shift, zero_state, wts, 1, N_META)
    shift_meta = rw_m[N_META - 1:N_META].reshape(1, 1, RW_COLS)

    rows_p = bp * seq
    tm = _row_tile(seq, ROW_TILE)
    tq = _row_tile(seq, KV_TILE)
    xp = x_prompt.reshape(rows_p, D_MODEL)
    ctq, stq, ctk, stk = _rope_tables(N_META + np.arange(seq))
    tab_p = (ctq * np.float32(Q_PRESCALE), stq * np.float32(Q_PRESCALE), ctk, stk)
    gsig, rw, q, kvrow, kpad, vt = _proj(xp, tab_p, wts, tm, BF16, True)
    o_mla = _attn(q, kpad, vt, kpad_m, vt_m, bp, seq, tq)
    o_rwkv, s_p = _rwkv(rw, shift_meta, s_meta, wts, bp, None)
    y_prompt = _mlp(xp, o_mla, o_rwkv, gsig, wts, tm).reshape(bp, seq, D_MODEL)
    kv_prompt = jnp.concatenate([jnp.broadcast_to(kvrow_m[None], (bp, N_META, KV_W)),
                                 kvrow.reshape(bp, seq, KV_W)], axis=1)[None]
    shift_prompt = rw.reshape(bp, seq, RW_COLS)[:, -1][None]

    xs = x_sample.reshape(bd, D_MODEL)
    tab_s = _rope_tables(np.full((bd,), past_len))
    gsig_s, rw_s, q_s, kvrow_s, _, _ = _proj(xs, tab_s, wts, bd, F32, False)
    qf = jnp.transpose(_qlat(q_s, wts['w_uk']), (1, 0, 2))
    cache = jnp.swapaxes(cache_kv.reshape(cache_kv.shape[1], PAGE, KV_W), 1, 2)
    o_mla_s = _sattn(page_table, qf, kvrow_s.reshape(bd, 1, KV_W), wts['w_uv'], cache).reshape(bd, HEADS * VDIM)
    s_t = jnp.transpose(state_wkv[0], (1, 2, 3, 0))
    o_rwkv_s, s_t_new = _rwkv_step(rw_s, state_shift[0], s_t, wts)
    s_s_new = jnp.transpose(s_t_new, (3, 0, 1, 2))
    y_sample = _mlp(xs, o_mla_s, o_rwkv_s, gsig_s, wts, bd).reshape(bd, 1, D_MODEL)

    return (y_prompt, y_sample,
            kv_prompt.astype(cache_kv.dtype), s_p[None].astype(state_wkv.dtype),
            shift_prompt.astype(state_shift.dtype),
            kvrow_s.reshape(1, bd, 1, KV_W).astype(cache_kv.dtype), s_s_new[None].astype(state_wkv.dtype),
            rw_s[None].astype(state_shift.dtype))
```
